```python
import math
import jax
import jax.numpy as jnp
from jax import lax
import numpy as np

D_MODEL = 1024
BATCH = 8
SEQ = 2048
DEPTH = 4

CTX_LEN = 256
GRID_W = 64

RET_HEADS = 4
RET_DK = 128
RET_DV = 128
RET_CHUNK = 128
WIN_HEADS = 8
WIN_KV_HEADS = 2
WIN_HD = 64
WINDOW = 128
WIN_BLOCK = 128
NA_HEADS = 8
NA_HD = 64
NA_ROWS_MAX = 8
NA_COLS = 16

BRANCH_W = 512
N_BRANCH = 3
D_FF_DENSE = 2816
N_EXPERTS = 8
TOP_K = 2
D_FF_EXPERT = 3584
ROPE_BASE = 10000.0
NORM_EPS = 1e-6
NEG_INF = -1e30

IN_SPLIT = (
    RET_HEADS * RET_DK,
    RET_HEADS * RET_DK,
    RET_HEADS * RET_DV,
    RET_HEADS * RET_DV,
    WIN_HEADS * WIN_HD,
    WIN_KV_HEADS * WIN_HD,
    WIN_KV_HEADS * WIN_HD,
    NA_HEADS * NA_HD,
    NA_HEADS * NA_HD,
    NA_HEADS * NA_HD,
    N_BRANCH * D_MODEL,
)
D_IN = sum(IN_SPLIT)

kernel_name = "hybrid_flow_backbone_ret_win_na_moe"


def rms_norm(x, g):
    xf = x.astype(jnp.float32)
    y = xf * lax.rsqrt(jnp.mean(xf * xf, axis=-1, keepdims=True) + NORM_EPS)
    return (y * g.astype(jnp.float32)).astype(x.dtype)


def split_cols(a):
    idx = [int(v) for v in np.cumsum(IN_SPLIT)[:-1]]
    return jnp.split(a, idx, axis=-1)


def heads(t, n):
    return t.reshape(t.shape[:2] + (n, -1))


def axial_angles(n_tok, head_dim):
    t = jnp.arange(n_tok)
    n_freq = head_dim // 4
    inv = ROPE_BASE ** (-jnp.arange(n_freq, dtype=jnp.float32) / n_freq)
    row = (t // GRID_W).astype(jnp.float32)[:, None] * inv
    col = (t % GRID_W).astype(jnp.float32)[:, None] * inv
    return jnp.concatenate([row, col], axis=-1)


def linear_angles(pos, head_dim):
    n_freq = head_dim // 2
    inv = ROPE_BASE ** (-jnp.arange(n_freq, dtype=jnp.float32) / n_freq)
    return pos.astype(jnp.float32)[:, None] * inv


def apply_rope(x, angles):
    cos = jnp.cos(angles)[:, None, :]
    sin = jnp.sin(angles)[:, None, :]
    x1, x2 = jnp.split(x.astype(jnp.float32), 2, axis=-1)
    return jnp.concatenate([x1 * cos - x2 * sin, x1 * sin + x2 * cos], axis=-1).astype(x.dtype)


def sink_softmax(logits, sink):
    full = jnp.concatenate([jnp.broadcast_to(sink, logits.shape[:-1] + (1,)), logits], axis=-1)
    return jax.nn.softmax(full, axis=-1)[..., 1:]


def retention_scan(q, k, v, log_gamma, state0, inclusive):
    B, H, L, dk = q.shape
    dv = v.shape[-1]
    C = RET_CHUNK
    nc = L // C
    qc = q.reshape(B, H, nc, C, dk)
    kc = k.reshape(B, H, nc, C, dk)
    vc = v.reshape(B, H, nc, C, dv)
    i = jnp.arange(C, dtype=jnp.float32)
    diff = i[:, None] - i[None, :]
    mask = (diff >= 0) if inclusive else (diff > 0)
    lg = log_gamma[:, None, None]
    d_intra = jnp.where(mask, jnp.exp(lg * jnp.where(mask, diff, 0.0)), 0.0)
    scores = jnp.einsum('bhncd,bhnsd->bhncs', qc, kc) * d_intra[None, :, None]
    y = jnp.einsum('bhncs,bhnse->bhnce', scores, vc)
    lg2 = log_gamma[:, None]
    q_dec = jnp.exp(lg2 * (i + 1.0))
    k_dec = jnp.exp(lg2 * (C - 1.0 - i))
    chunk_dec = jnp.exp(log_gamma * C)[None, :, None, None]
    kv = jnp.einsum('bhncd,bhnce->nbhde', kc * k_dec[None, :, None, :, None], vc)

    def step(s, kv_n):
        return s * chunk_dec + kv_n, s

    _, s_prev = lax.scan(step, state0, kv)
    y = y + jnp.einsum('bhncd,nbhde->bhnce', qc * q_dec[None, :, None, :, None], s_prev)
    return y.reshape(B, H, L, dv)


def retention_final_state(k, v, log_gamma):
    L = k.shape[2]
    w = jnp.exp(log_gamma[:, None] * (L - 1.0 - jnp.arange(L, dtype=jnp.float32)))
    return jnp.einsum('bhld,bhle->bhde', k * w[None, :, :, None], v)


def bidir_retention(q, k, v, log_gamma, state_f, state_b):
    flip = lambda a: jnp.flip(a, axis=2)
    y_f = retention_scan(q, k, v, log_gamma[0], state_f, True)
    y_b = flip(retention_scan(flip(q), flip(k), flip(v), log_gamma[1], state_b, False))
    return y_f + y_b


def retention_out(y, g):
    B, H, L, dv = y.shape
    mu = jnp.mean(y, axis=-1, keepdims=True)
    var = jnp.mean(jnp.square(y - mu), axis=-1, keepdims=True)
    yn = (y - mu) * lax.rsqrt(var + NORM_EPS)
    yn = jnp.swapaxes(yn, 1, 2).reshape(B, L, H * dv)
    return (jax.nn.silu(g.astype(jnp.float32)) * yn).astype(g.dtype)


def window_attention(q, k, v, k_c, v_c, sink):
    B, S, _, hd = q.shape
    G = WIN_HEADS // WIN_KV_HEADS
    nb = S // WIN_BLOCK
    qb = q.reshape(B, nb, WIN_BLOCK, WIN_KV_HEADS, G, hd)

    def band(a):
        ap = jnp.pad(a, ((0, 0), (WIN_BLOCK, WIN_BLOCK), (0, 0), (0, 0)))
        ap = ap.reshape(B, nb + 2, WIN_BLOCK, WIN_KV_HEADS, hd)
        return jnp.concatenate([ap[:, :-2], ap[:, 1:-1], ap[:, 2:]], axis=2)

    kw, vw = band(k), band(v)
    scale = hd ** -0.5
    s_win = jnp.einsum('bnqkgd,bnskd->bnkgqs', qb, kw, preferred_element_type=jnp.float32) * scale
    qpos = (jnp.arange(nb)[:, None] * WIN_BLOCK + jnp.arange(WIN_BLOCK))[:, :, None]
    kpos = (jnp.arange(nb)[:, None] * WIN_BLOCK - WIN_BLOCK + jnp.arange(3 * WIN_BLOCK))[:, None, :]
    valid = (kpos >= 0) & (kpos < S) & (jnp.abs(qpos - kpos) <= WINDOW)
    s_win = jnp.where(valid[None, :, None, None], s_win, NEG_INF)
    s_ctx = jnp.einsum('bnqkgd,bckd->bnkgqc', qb, k_c, preferred_element_type=jnp.float32) * scale
    sink_b = sink.astype(jnp.float32).reshape(WIN_KV_HEADS, G, 1, 1)
    p = sink_softmax(jnp.concatenate([s_win, s_ctx], axis=-1), sink_b)
    nw = 3 * WIN_BLOCK
    y = (jnp.einsum('bnkgqs,bnskd->bnqkgd', p[..., :nw].astype(v.dtype), vw)
         + jnp.einsum('bnkgqc,bckd->bnqkgd', p[..., nw:].astype(v.dtype), v_c))
    return y.reshape(B, S, WIN_HEADS * hd)


def neighbourhood_attention(q, k, v, k_c, v_c, rpb):
    B, S, H, hd = q.shape
    rows = S // GRID_W
    kr = min(NA_ROWS_MAX, rows)
    r = jnp.arange(rows)
    row_idx = jnp.clip(r - kr // 2, 0, rows - kr)[:, None] + jnp.arange(kr)
    cq = jnp.arange(GRID_W)
    c_start = jnp.clip(cq - NA_COLS // 2, 0, GRID_W - NA_COLS)
    ck = jnp.arange(GRID_W)
    col_ok = (ck[None, :] >= c_start[:, None]) & (ck[None, :] < c_start[:, None] + NA_COLS)
    qg = q.reshape(B, rows, GRID_W, H, hd)
    kg = k.reshape(B, rows, GRID_W, H, hd)[:, row_idx]
    vg = v.reshape(B, rows, GRID_W, H, hd)[:, row_idx]
    scale = hd ** -0.5
    s_nb = jnp.einsum('brqhd,brjkhd->bhrqjk', qg, kg, preferred_element_type=jnp.float32) * scale
    dr = row_idx - r[:, None] + (NA_ROWS_MAX - 1)
    dc = jnp.clip(ck[None, :] - cq[:, None] + (NA_COLS - 1), 0, 2 * NA_COLS - 2)
    bias = rpb.astype(jnp.float32)[:, dr[:, None, :, None], dc[None, :, None, :]]
    s_nb = jnp.where(col_ok[:, None, :], s_nb + bias, NEG_INF)
    s_ctx = jnp.einsum('brqhd,bchd->bhrqc', qg, k_c, preferred_element_type=jnp.float32) * scale
    n_nb = kr * GRID_W
    p = jax.nn.softmax(jnp.concatenate([s_nb.reshape(B, H, rows, GRID_W, n_nb), s_ctx], axis=-1), axis=-1)
    p_nb = p[..., :n_nb].reshape(B, H, rows, GRID_W, kr, GRID_W).astype(v.dtype)
    y = (jnp.einsum('bhrqjk,brjkhd->brqhd', p_nb, vg)
         + jnp.einsum('bhrqc,bchd->brqhd', p[..., n_nb:].astype(v.dtype), v_c))
    return y.reshape(B, S, H * hd)


def context_attention(q, k, v, sink):
    B, L = q.shape[:2]
    s = jnp.einsum('bqkgd,bskd->bkgqs', q, k, preferred_element_type=jnp.float32) * q.shape[-1] ** -0.5
    p = jax.nn.softmax(s, axis=-1) if sink is None else sink_softmax(s, sink)
    y = jnp.einsum('bkgqs,bskd->bqkgd', p.astype(v.dtype), v)
    return y.reshape(B, L, -1)


def merge_branches(ys, gate_cols, w_branch, w_out):
    y = jnp.stack(ys, axis=2)
    z = jnp.einsum('blre,red->blrd', y, w_branch)
    g = jax.nn.sigmoid(gate_cols.reshape(gate_cols.shape[:2] + (N_BRANCH, D_MODEL)))
    return jnp.sum(g * z, axis=2) @ w_out


def token_mixing(a_l, a_c, w_in, ret_decay, win_sink, na_rpb, w_branch, w_out,
                 ang_ret_l, ang_ret_c, ang_ax, need_ctx):
    B, S, _ = a_l.shape
    Lc = a_c.shape[1]
    rq, rk, rv, rg, wq, wk, wv, nq, nk, nv, gate = split_cols(a_l @ w_in)
    rq_c, rk_c, rv_c, rg_c, wq_c, wk_c, wv_c, nq_c, nk_c, nv_c, gate_c = split_cols(a_c @ w_in)

    log_gamma = jnp.log1p(-jnp.exp(ret_decay.astype(jnp.float32)))

    def ret_in(q, k, v, ang):
        q = apply_rope(heads(q, RET_HEADS), ang) * RET_DK ** -0.5
        k = apply_rope(heads(k, RET_HEADS), ang)
        v = heads(v, RET_HEADS)
        return (jnp.swapaxes(q, 1, 2).astype(jnp.float32), jnp.swapaxes(k, 1, 2).astype(jnp.float32),
                jnp.swapaxes(v, 1, 2).astype(jnp.float32))

    q_l, k_l, v_l = ret_in(rq, rk, rv, ang_ret_l)
    q_c, k_c, v_c = ret_in(rq_c, rk_c, rv_c, ang_ret_c)
    s_f = retention_final_state(k_c, v_c, log_gamma[0])
    s_b = retention_final_state(jnp.flip(k_c, 2), jnp.flip(v_c, 2), log_gamma[1])
    y_ret = retention_out(bidir_retention(q_l, k_l, v_l, log_gamma, s_f, s_b), rg)

    wk_ch = heads(wk_c, WIN_KV_HEADS)
    wv_ch = heads(wv_c, WIN_KV_HEADS)
    y_win = window_attention(apply_rope(heads(wq, WIN_HEADS), ang_ax),
                             apply_rope(heads(wk, WIN_KV_HEADS), ang_ax),
                             heads(wv, WIN_KV_HEADS), wk_ch, wv_ch, win_sink)

    nk_ch = heads(nk_c, NA_HEADS)
    nv_ch = heads(nv_c, NA_HEADS)
    y_na = neighbourhood_attention(heads(nq, NA_HEADS), heads(nk, NA_HEADS), heads(nv, NA_HEADS),
                                   nk_ch, nv_ch, na_rpb)

    y_l = merge_branches([y_ret, y_win.astype(a_l.dtype), y_na.astype(a_l.dtype)], gate, w_branch, w_out)
    if not need_ctx:
        return y_l, None

    zero = jnp.zeros_like(s_f)
    yc_ret = retention_out(bidir_retention(q_c, k_c, v_c, log_gamma, zero, zero), rg_c)
    G = WIN_HEADS // WIN_KV_HEADS
    yc_win = context_attention(heads(wq_c, WIN_HEADS).reshape(B, Lc, WIN_KV_HEADS, G, WIN_HD), wk_ch, wv_ch,
                               win_sink.astype(jnp.float32).reshape(WIN_KV_HEADS, G, 1, 1))
    yc_na = context_attention(heads(nq_c, NA_HEADS)[:, :, :, None, :], nk_ch, nv_ch, None)
    y_c = merge_branches([yc_ret, yc_win.astype(a_c.dtype), yc_na.astype(a_c.dtype)], gate_c, w_branch, w_out)
    return y_l, y_c


def swiglu(x, w1, w3, w2):
    return (jax.nn.silu(x @ w1) * (x @ w3)) @ w2


def moe_ffn(x, w_router, w1, w3, w2):
    logits = jnp.einsum('bld,de->ble', x, w_router, preferred_element_type=jnp.float32)
    top_val, top_idx = lax.top_k(logits, TOP_K)
    top_w = jax.nn.softmax(top_val, axis=-1)
    gates = jnp.sum(jax.nn.one_hot(top_idx, N_EXPERTS, dtype=jnp.float32) * top_w[..., None], axis=-2)
    y = jnp.zeros_like(x)
    for e in range(N_EXPERTS):
        y = y + gates[..., e:e + 1].astype(x.dtype) * swiglu(x, w1[e], w3[e], w2[e])
    return y


def channel_mixer(a, layer, ffn_w1, ffn_w3, ffn_w2, moe_router, moe_w1, moe_w3, moe_w2):
    i = layer // 2
    if layer % 2 == 0:
        return swiglu(a, ffn_w1[i], ffn_w3[i], ffn_w2[i])
    return moe_ffn(a, moe_router[i], moe_w1[i], moe_w3[i], moe_w2[i])


def setup_inputs(seed: int = 0) -> dict:
    key = jax.random.key(seed)
    ks = jax.random.split(key, 24)
    f32 = jnp.float32
    D = D_MODEL
    nd = (DEPTH + 1) // 2
    nm = DEPTH // 2

    def normal(k, shape, scale):
        return jax.random.normal(k, shape, f32) * scale

    base_decay = jnp.linspace(math.log(1.0 / 32.0), math.log(1.0 / 512.0), RET_HEADS).astype(f32)
    return {
        "x": normal(ks[0], (BATCH, SEQ, D), 1.0),
        "c": normal(ks[1], (BATCH, D), 1.0),
        "ctx": normal(ks[2], (BATCH, CTX_LEN, D), 1.0),
        "c_ctx": normal(ks[3], (D,), 1.0),
        "w_mod": normal(ks[4], (DEPTH, D, 6 * D), 0.5 * D ** -0.5),
        "b_mod": normal(ks[5], (DEPTH, 6 * D), 0.02),
        "norm_gains": 1.0 + normal(ks[6], (DEPTH, 4, D), 0.05),
        "w_in": normal(ks[7], (DEPTH, D, D_IN), D ** -0.5),
        "ret_decay": base_decay + normal(ks[8], (DEPTH, 2, RET_HEADS), 0.1),
        "win_sink": normal(ks[9], (DEPTH, WIN_HEADS), 0.5),
        "na_rpb": normal(ks[10], (DEPTH, NA_HEADS, 2 * NA_ROWS_MAX - 1, 2 * NA_COLS - 1), 0.2),
        "w_branch": normal(ks[11], (DEPTH, N_BRANCH, BRANCH_W, D), BRANCH_W ** -0.5),
        "w_out": normal(ks[12], (DEPTH, D, D), D ** -0.5),
        "ffn_w1": normal(ks[13], (nd, D, D_FF_DENSE), D ** -0.5),
        "ffn_w3": normal(ks[14], (nd, D, D_FF_DENSE), D ** -0.5),
        "ffn_w2": normal(ks[15], (nd, D_FF_DENSE, D), D_FF_DENSE ** -0.5),
        "moe_router": normal(ks[16], (nm, D, N_EXPERTS), D ** -0.5),
        "moe_w1": normal(ks[17], (nm, N_EXPERTS, D, D_FF_EXPERT), D ** -0.5),
        "moe_w3": normal(ks[18], (nm, N_EXPERTS, D, D_FF_EXPERT), D ** -0.5),
        "moe_w2": normal(ks[19], (nm, N_EXPERTS, D_FF_EXPERT, D), D_FF_EXPERT ** -0.5),
    }


def reference(x, c, ctx, c_ctx, w_mod, b_mod, norm_gains, w_in, ret_decay, win_sink, na_rpb,
              w_branch, w_out, ffn_w1, ffn_w3, ffn_w2, moe_router, moe_w1, moe_w3, moe_w2):
    B, S, D = x.shape
    Lc = ctx.shape[1]
    ang_ax = axial_angles(S, WIN_HD)
    ang_ret_c = linear_angles(jnp.arange(Lc), RET_DK)
    ang_ret_l = linear_angles(Lc + jnp.arange(S), RET_DK)
    h, hc = x, ctx
    for layer in range(DEPTH):
        need_ctx = layer < DEPTH - 1
        g = norm_gains[layer]
        mod_l = (jax.nn.silu(c) @ w_mod[layer] + b_mod[layer]).reshape(B, 6, D)
        mod_c = (jax.nn.silu(c_ctx) @ w_mod[layer] + b_mod[layer]).reshape(6, D)
        ml = [mod_l[:, j, None, :] for j in range(6)]
        mc = [mod_c[j] for j in range(6)]

        a_l = rms_norm(h, g[0]) * (1.0 + ml[1]) + ml[0]
        a_c = rms_norm(hc, g[0]) * (1.0 + mc[1]) + mc[0]
        y_l, y_c = token_mixing(a_l, a_c, w_in[layer], ret_decay[layer], win_sink[layer], na_rpb[layer],
                                w_branch[layer], w_out[layer], ang_ret_l, ang_ret_c, ang_ax, need_ctx)
        h = h + ml[2] * rms_norm(y_l, g[1])
        if need_ctx:
            hc = hc + mc[2] * rms_norm(y_c, g[1])

        f_l = rms_norm(h, g[2]) * (1.0 + ml[4]) + ml[3]
        h = h + ml[5] * rms_norm(channel_mixer(f_l, layer, ffn_w1, ffn_w3, ffn_w2,
                                               moe_router, moe_w1, moe_w3, moe_w2), g[3])
        if need_ctx:
            f_c = rms_norm(hc, g[2]) * (1.0 + mc[4]) + mc[3]
            hc = hc + mc[5] * rms_norm(channel_mixer(f_c, layer, ffn_w1, ffn_w3, ffn_w2,
                                                     moe_router, moe_w1, moe_w3, moe_w2), g[3])
    return h
```

```python
import functools
import math

import jax
import jax.numpy as jnp
import numpy as np
from jax import lax
from jax.experimental import pallas as pl
from jax.experimental.pallas import tpu as pltpu

F32 = jnp.float32
BF16 = jnp.bfloat16

D_MODEL = 1024
GRID_W = 64
RET_HEADS = 4
RET_DK = 128
WIN_HEADS = 8
WIN_KV_HEADS = 2
WIN_HD = 64
WINDOW = 128
WIN_BLOCK = 128
NA_HEADS = 8
NA_HD = 64
NA_ROWS_MAX = 8
NA_COLS = 16
BRANCH_W = 512
N_BRANCH = 3
N_EXPERTS = 8
ROPE_BASE = 10000.0
NORM_EPS = 1e-6
NEG_INF = -1e30

LANES = 128
VMEM_LIMIT = 56 * 1024 * 1024

C_RQ, C_RK, C_RV, C_RG = 0, 512, 1024, 1536
C_WQ, C_WK, C_WV = 2048, 2560, 2688
C_NQ, C_NK, C_NV = 2816, 3328, 3840
C_GATE = 4352
D_IN = 7424

RET_CHUNK = 256
NA_QROWS = 4
NA_KROWS = 12


def _cparams(sem):
    return pltpu.CompilerParams(dimension_semantics=sem, vmem_limit_bytes=VMEM_LIMIT)


def _rms(x, g):
    return x * lax.rsqrt(jnp.mean(x * x, axis=-1, keepdims=True) + NORM_EPS) * g


def _dot(a, b):
    return jnp.dot(a, b, preferred_element_type=F32)


def _dot_nt(a, b):
    return lax.dot_general(a, b, (((1,), (1,)), ((), ())), preferred_element_type=F32)


def _dot_tn(a, b):
    return lax.dot_general(a, b, (((0,), (0,)), ((), ())), preferred_element_type=F32)


def _mod_kernel(c_ref, w_ref, b_ref, o_ref):
    c = c_ref[...]
    s = c * jax.nn.sigmoid(c)
    o_ref[0] = jnp.dot(s, w_ref[0], preferred_element_type=F32,
                       precision=lax.Precision.HIGHEST) + b_ref[0]


def _modulation(c_all, w_mod, b_mod):
    depth, d, n = w_mod.shape
    tn = 1536
    return pl.pallas_call(
        _mod_kernel,
        out_shape=jax.ShapeDtypeStruct((depth, c_all.shape[0], n), F32),
        grid=(depth, n // tn),
        in_specs=[pl.BlockSpec(c_all.shape, lambda l, j: (0, 0)),
                  pl.BlockSpec((1, d, tn), lambda l, j: (l, 0, j)),
                  pl.BlockSpec((1, 1, tn), lambda l, j: (l, 0, j))],
        out_specs=pl.BlockSpec((1, c_all.shape[0], tn), lambda l, j: (l, 0, j)),
        compiler_params=_cparams(("arbitrary", "arbitrary")),
        name="modulation",
    )(c_all, w_mod, b_mod.reshape(depth, 1, n))


def _inproj_kernel(h_ref, mod_ref, g_ref, w_ref, cr_ref, sr_ref, ca_ref, sa_ref,
                   oret_ref, owin_ref, ona_ref, og_ref, *, rope_win):
    x = h_ref[...]
    a = _rms(x, g_ref[0:1, :]) * (1.0 + mod_ref[0, 1:2, :]) + mod_ref[0, 0:1, :]
    a = a.astype(BF16)
    tm = x.shape[0]
    lane = lax.broadcasted_iota(jnp.int32, (tm, LANES), 1)
    even = lane < 64
    first_half = (lane % 64) < 32

    def mm(c0, cw):
        return _dot(a, w_ref[:, c0:c0 + cw])

    def rope_lin(v):
        return v * cr_ref[...] + pltpu.roll(v, 64, 1) * sr_ref[...]

    def rope_ax(v):
        rot = jnp.where(first_half, pltpu.roll(v, 96, 1), pltpu.roll(v, 32, 1))
        return v * ca_ref[...] + rot * sa_ref[...]

    r = mm(C_RQ, 512)
    for h in range(RET_HEADS):
        blk = rope_lin(r[:, h * 128:(h + 1) * 128]) * (RET_DK ** -0.5)
        oret_ref[:, h * 128:(h + 1) * 128] = blk.astype(BF16)
    r = mm(C_RK, 512)
    for h in range(RET_HEADS):
        blk = rope_lin(r[:, h * 128:(h + 1) * 128])
        oret_ref[:, 512 + h * 128:512 + (h + 1) * 128] = blk.astype(BF16)
    oret_ref[:, 1024:1536] = mm(C_RV, 512).astype(BF16)
    oret_ref[:, 1536:2048] = mm(C_RG, 512).astype(BF16)

    r = mm(C_WQ, 512)
    for p in range(WIN_HEADS // 2):
        blk = r[:, p * 128:(p + 1) * 128]
        if rope_win:
            blk = rope_ax(blk)
        blk = blk * (WIN_HD ** -0.5)
        owin_ref[:, (2 * p) * 128:(2 * p + 1) * 128] = jnp.where(even, blk, 0.0).astype(BF16)
        owin_ref[:, (2 * p + 1) * 128:(2 * p + 2) * 128] = jnp.where(even, 0.0, blk).astype(BF16)
    r = mm(C_WK, 256)
    k2 = r[:, 0:128]
    if rope_win:
        k2 = rope_ax(k2)
    v2 = r[:, 128:256]
    for i, t in enumerate((k2, v2)):
        sw = pltpu.roll(t, 64, 1)
        owin_ref[:, 1024 + 256 * i:1024 + 256 * i + 128] = jnp.where(even, t, sw).astype(BF16)
        owin_ref[:, 1024 + 256 * i + 128:1024 + 256 * i + 256] = jnp.where(even, sw, t).astype(BF16)

    r = mm(C_NQ, 512) * (NA_HD ** -0.5)
    for p in range(NA_HEADS // 2):
        blk = r[:, p * 128:(p + 1) * 128]
        ona_ref[:, (2 * p) * 128:(2 * p + 1) * 128] = jnp.where(even, blk, 0.0).astype(BF16)
        ona_ref[:, (2 * p + 1) * 128:(2 * p + 2) * 128] = jnp.where(even, 0.0, blk).astype(BF16)
    ona_ref[:, 1024:1536] = mm(C_NK, 512).astype(BF16)
    ona_ref[:, 1536:2048] = mm(C_NV, 512).astype(BF16)

    for j in range(6):
        og_ref[:, j * 512:(j + 1) * 512] = mm(C_GATE + j * 512, 512).astype(BF16)


def _inproj(h, mod, gains, w_in, cr, sr, ca, sa, *, is_ctx, tm=256):
    t, d = h.shape
    nt = t // tm
    per_batch = cr.shape[0] // tm
    if is_ctx:
        mod_map = lambda i: (mod.shape[0] - 1, 0, 0)
    else:
        mod_map = lambda i: (i // per_batch, 0, 0)
    pos_map = lambda i: (i % per_batch, 0)
    full = lambda i: (0, 0)
    row = lambda i: (i, 0)
    outs = (jax.ShapeDtypeStruct((t, 2048), BF16), jax.ShapeDtypeStruct((t, 1536), BF16),
            jax.ShapeDtypeStruct((t, 2048), BF16), jax.ShapeDtypeStruct((t, 3072), BF16))
    return pl.pallas_call(
        functools.partial(_inproj_kernel, rope_win=not is_ctx),
        out_shape=outs,
        grid=(nt,),
        in_specs=[pl.BlockSpec((tm, d), row),
                  pl.BlockSpec((1, 6, d), mod_map),
                  pl.BlockSpec(gains.shape, full),
                  pl.BlockSpec(w_in.shape, full),
                  pl.BlockSpec((tm, LANES), pos_map), pl.BlockSpec((tm, LANES), pos_map),
                  pl.BlockSpec((tm, LANES), pos_map), pl.BlockSpec((tm, LANES), pos_map)],
        out_specs=(pl.BlockSpec((tm, 2048), row), pl.BlockSpec((tm, 1536), row),
                   pl.BlockSpec((tm, 2048), row), pl.BlockSpec((tm, 3072), row)),
        compiler_params=_cparams(("parallel",)),
        name="inproj_ctx" if is_ctx else "inproj_lat",
    )(h, mod, gains, w_in, cr, sr, ca, sa)


def _ret_kernel(lg_ref, ql, kl, vl, gl, qc, kc, vc, gc, yl_ref, yc_ref, acc_ref):
    C = RET_CHUNK
    h = pl.program_id(1)
    lgf = lg_ref[0, h]
    lgb = lg_ref[1, h]
    ii = lax.broadcasted_iota(jnp.int32, (C, C), 0).astype(F32)
    jj = lax.broadcasted_iota(jnp.int32, (C, C), 1).astype(F32)
    diff = ii - jj
    dmat = jnp.where(diff >= 0, jnp.exp(lgf * jnp.maximum(diff, 0.0)),
                     jnp.exp(lgb * jnp.maximum(-diff, 0.0)))
    r = lax.broadcasted_iota(jnp.int32, (C, RET_DK), 0).astype(F32)
    qdf = jnp.exp(lgf * (r + 1.0))
    kdf = jnp.exp(lgf * (C - 1.0 - r))
    qdb = jnp.exp(lgb * (C - r))
    kdb = jnp.exp(lgb * r)
    ones = jnp.ones((RET_DK, RET_DK), F32)
    cdf = jnp.exp(ones * (lgf * C))
    cdb = jnp.exp(ones * (lgb * C))

    n_lat = ql.shape[0] // C
    lc = qc.shape[0]
    chunks = [(qc, kc, vc, gc, yc_ref, 0, 0)]
    chunks += [(ql, kl, vl, gl, yl_ref, j * C, lc + j * C) for j in range(n_lat)]

    state = jnp.zeros((RET_DK, RET_DK), F32)
    for (qr, kr, vr, _, _, off, aoff) in chunks:
        q = qr[off:off + C, :]
        k = kr[off:off + C, :]
        v = vr[off:off + C, :]
        s = _dot_nt(q, k) * dmat
        y = _dot(s.astype(BF16), v)
        y = y + _dot((q.astype(F32) * qdf).astype(BF16), state.astype(BF16))
        state = state * cdf + _dot_tn((k.astype(F32) * kdf).astype(BF16), v)
        acc_ref[aoff:aoff + C, :] = y

    state = jnp.zeros((RET_DK, RET_DK), F32)
    for (qr, kr, vr, gr, yr, off, aoff) in [chunks[0]] + chunks[:0:-1]:
        q = qr[off:off + C, :]
        k = kr[off:off + C, :]
        v = vr[off:off + C, :]
        y = acc_ref[aoff:aoff + C, :] + _dot((q.astype(F32) * qdb).astype(BF16), state.astype(BF16))
        state = state * cdb + _dot_tn((k.astype(F32) * kdb).astype(BF16), v)
        mu = jnp.mean(y, axis=-1, keepdims=True)
        dlt = y - mu
        var = jnp.mean(dlt * dlt, axis=-1, keepdims=True)
        yn = dlt * lax.rsqrt(var + NORM_EPS)
        g = gr[off:off + C, :].astype(F32)
        yr[off:off + C, :] = (g * jax.nn.sigmoid(g) * yn).astype(BF16)


def _retention(lg, pl_ret, pc_ret, batch):
    tl = pl_ret.shape[0]
    tc = pc_ret.shape[0]
    s = tl // batch
    lc = tc // batch
    assert lc == RET_CHUNK and s % RET_CHUNK == 0
    lat = lambda off: pl.BlockSpec((s, 128), lambda b, h, off=off: (b, off + h))
    ctx = lambda off: pl.BlockSpec((lc, 128), lambda b, h, off=off: (b, off + h))
    return pl.pallas_call(
        _ret_kernel,
        out_shape=(jax.ShapeDtypeStruct((tl, 512), BF16), jax.ShapeDtypeStruct((tc, 512), BF16)),
        grid=(batch, RET_HEADS),
        in_specs=[pl.BlockSpec(memory_space=pltpu.SMEM),
                  lat(0), lat(4), lat(8), lat(12), ctx(0), ctx(4), ctx(8), ctx(12)],
        out_specs=(pl.BlockSpec((s, 128), lambda b, h: (b, h)),
                   pl.BlockSpec((lc, 128), lambda b, h: (b, h))),
        scratch_shapes=[pltpu.VMEM((s + lc, 128), F32)],
        compiler_params=_cparams(("parallel", "parallel")),
        name="retention",
    )(lg, pl_ret, pl_ret, pl_ret, pl_ret, pc_ret, pc_ret, pc_ret, pc_ret)


def _softmax_parts(parts, sink=None):
    m = functools.reduce(jnp.maximum, [jnp.max(p, axis=-1, keepdims=True) for p in parts])
    if sink is not None:
        m = jnp.maximum(m, sink)
    es = [jnp.exp(p - m) for p in parts]
    den = functools.reduce(lambda a, b: a + b, [jnp.sum(e, axis=-1, keepdims=True) for e in es])
    if sink is not None:
        den = den + jnp.exp(sink - m)
    inv = 1.0 / den
    return [(e * inv).astype(BF16) for e in es]


def _win_kernel(sink_ref, q_ref, k0, k1, v0, v1, ck0, ck1, cv0, cv1, o_ref):
    blk = WIN_BLOCK
    nw = 3 * blk
    s_len = k0.shape[0]
    j = pl.program_id(1)
    start = pl.multiple_of(jnp.clip((j - 1) * blk, 0, s_len - nw), blk)
    qpos = j * blk + lax.broadcasted_iota(jnp.int32, (blk, nw), 0)
    kpos = start + lax.broadcasted_iota(jnp.int32, (blk, nw), 1)
    bias = jnp.where(jnp.abs(qpos - kpos) <= WINDOW, 0.0, NEG_INF)
    even = lax.broadcasted_iota(jnp.int32, (blk, LANES), 1) < 64
    g = WIN_HEADS // WIN_KV_HEADS
    for kv, (kr, vr, ckr, cvr) in enumerate(((k0, v0, ck0, cv0), (k1, v1, ck1, cv1))):
        kcat = jnp.concatenate([kr[pl.ds(start, nw), :], ckr[...]], axis=0)
        vcat = jnp.concatenate([vr[pl.ds(start, nw), :], cvr[...]], axis=0)
        qst = jnp.concatenate([q_ref[:, (g * kv + i) * 128:(g * kv + i + 1) * 128] for i in range(g)], axis=0)
        s = _dot_nt(qst, kcat)
        ps = []
        for i in range(g):
            sb = s[i * blk:(i + 1) * blk]
            pw, pc = _softmax_parts([sb[:, :nw] + bias, sb[:, nw:]], sink_ref[g * kv + i])
            ps.append(jnp.concatenate([pw, pc], axis=1))
        pv = _dot(jnp.concatenate(ps, axis=0), vcat)
        for p in range(g // 2):
            o_ref[:, (2 * kv + p) * 128:(2 * kv + p + 1) * 128] = jnp.where(
                even, pv[(2 * p) * blk:(2 * p + 1) * blk], pv[(2 * p + 1) * blk:(2 * p + 2) * blk]).astype(BF16)


def _window(sink, pl_win, pc_win, batch):
    tl = pl_win.shape[0]
    tc = pc_win.shape[0]
    s = tl // batch
    lc = tc // batch
    nb = s // WIN_BLOCK
    lat = lambda cb: pl.BlockSpec((s, 128), lambda b, j, cb=cb: (b, cb))
    ctx = lambda cb: pl.BlockSpec((lc, 128), lambda b, j, cb=cb: (b, cb))
    return pl.pallas_call(
        _win_kernel,
        out_shape=jax.ShapeDtypeStruct((tl, 512), BF16),
        grid=(batch, nb),
        in_specs=[pl.BlockSpec(memory_space=pltpu.SMEM),
                  pl.BlockSpec((WIN_BLOCK, 1024), lambda b, j: (b * nb + j, 0)),
                  lat(8), lat(9), lat(10), lat(11), ctx(8), ctx(9), ctx(10), ctx(11)],
        out_specs=pl.BlockSpec((WIN_BLOCK, 512), lambda b, j: (b * nb + j, 0)),
        compiler_params=_cparams(("parallel", "arbitrary")),
        name="window_attn",
    )(sink, pl_win, pl_win, pl_win, pl_win, pl_win, pc_win, pc_win, pc_win, pc_win)


def _na_kernel(q_ref, ka, kb, kc, va, vb, vc, ck, cv, bias_ref, o_ref):
    nq = q_ref.shape[0]
    qst = jnp.concatenate([q_ref[:, 0:128], q_ref[:, 128:256]], axis=0)
    kcat = jnp.concatenate([ka[...], kb[...], kc[...], ck[...]], axis=0)
    vcat = jnp.concatenate([va[...], vb[...], vc[...], cv[...]], axis=0)
    n_nb = ka.shape[0] * 3
    s = _dot_nt(qst, kcat)
    ps = []
    for par in range(2):
        sb = s[par * nq:(par + 1) * nq]
        pn, pc = _softmax_parts([sb[:, :n_nb] + bias_ref[0, 0, par], sb[:, n_nb:]])
        ps.append(jnp.concatenate([pn, pc], axis=1))
    pv = _dot(jnp.concatenate(ps, axis=0), vcat)
    even = lax.broadcasted_iota(jnp.int32, (nq, LANES), 1) < 64
    o_ref[...] = jnp.where(even, pv[:nq], pv[nq:]).astype(BF16)


def _na_pattern(i):
    return jnp.where(i == 0, 0, jnp.where(i == 7, 2, 1))


def _neighbourhood(pl_na, pc_na, bias, batch):
    tl = pl_na.shape[0]
    tc = pc_na.shape[0]
    s = tl // batch
    lc = tc // batch
    nq = NA_QROWS * GRID_W
    ng = s // nq
    kblk = nq
    nkb = s // kblk
    assert ng == 8 and NA_KROWS * GRID_W == 3 * kblk
    hp = NA_HEADS // 2

    def kspec(cb0, d):
        return pl.BlockSpec(
            (kblk, 128),
            lambda p, i, b, d=d, cb0=cb0: (b * nkb + jnp.clip(i - 1, 0, nkb - 3) + d, cb0 + p))

    return pl.pallas_call(
        _na_kernel,
        out_shape=jax.ShapeDtypeStruct((tl, 512), BF16),
        grid=(hp, ng, batch),
        in_specs=[pl.BlockSpec((nq, 256), lambda p, i, b: (b * ng + i, p)),
                  kspec(8, 0), kspec(8, 1), kspec(8, 2),
                  kspec(12, 0), kspec(12, 1), kspec(12, 2),
                  pl.BlockSpec((lc, 128), lambda p, i, b: (b, 8 + p)),
                  pl.BlockSpec((lc, 128), lambda p, i, b: (b, 12 + p)),
                  pl.BlockSpec((1, 1, 2, nq, 3 * kblk), lambda p, i, b: (p, _na_pattern(i), 0, 0, 0))],
        out_specs=pl.BlockSpec((nq, 128), lambda p, i, b: (b * ng + i, p)),
        compiler_params=_cparams(("parallel", "parallel", "arbitrary")),
        name="neighbourhood_attn",
    )(pl_na, pl_na, pl_na, pl_na, pl_na, pl_na, pl_na, pc_na, pc_na, bias)


def _na_bias_table(rpb, s):
    rows = s // GRID_W
    kr = min(NA_ROWS_MAX, rows)
    tabs = []
    for i in (0, 1, rows // NA_QROWS - 1):
        r = NA_QROWS * i + np.arange(NA_QROWS)
        k0 = NA_QROWS * int(np.clip(i - 1, 0, rows // NA_QROWS - 3))
        krow = k0 + np.arange(NA_KROWS)
        r_start = np.clip(r - kr // 2, 0, rows - kr)
        row_ok = (krow[None, :] >= r_start[:, None]) & (krow[None, :] < r_start[:, None] + kr)
        dr = np.clip(krow[None, :] - r[:, None] + (NA_ROWS_MAX - 1), 0, 2 * NA_ROWS_MAX - 2)
        cq = np.arange(GRID_W)
        ck = np.arange(GRID_W)
        c_start = np.clip(cq - NA_COLS // 2, 0, GRID_W - NA_COLS)
        col_ok = (ck[None, :] >= c_start[:, None]) & (ck[None, :] < c_start[:, None] + NA_COLS)
        dc = np.clip(ck[None, :] - cq[:, None] + (NA_COLS - 1), 0, 2 * NA_COLS - 2)
        ok = row_ok[:, None, :, None] & col_ok[None, :, None, :]
        b = rpb.astype(F32)[:, dr[:, None, :, None], dc[None, :, None, :]]
        b = jnp.where(ok[None], b, NEG_INF)
        tabs.append(b.reshape(NA_HEADS, NA_QROWS * GRID_W, NA_KROWS * GRID_W))
    t = jnp.stack(tabs, axis=1)
    t = t.reshape(NA_HEADS // 2, 2, 3, t.shape[2], t.shape[3])
    return jnp.swapaxes(t, 1, 2)


def _ctx_attn_kernel(sink_ref, w_ref, n_ref, ow_ref, on_ref):
    lc = w_ref.shape[0]
    even = lax.broadcasted_iota(jnp.int32, (lc, LANES), 1) < 64
    g = WIN_HEADS // WIN_KV_HEADS
    for kv in range(WIN_KV_HEADS):
        k = w_ref[:, 1024 + kv * 128:1024 + (kv + 1) * 128]
        v = w_ref[:, 1280 + kv * 128:1280 + (kv + 1) * 128]
        qst = jnp.concatenate([w_ref[:, (g * kv + i) * 128:(g * kv + i + 1) * 128] for i in range(g)], axis=0)
        s = _dot_nt(qst, k)
        ps = [_softmax_parts([s[i * lc:(i + 1) * lc]], sink_ref[g * kv + i])[0] for i in range(g)]
        pv = _dot(jnp.concatenate(ps, axis=0), v)
        for p in range(g // 2):
            ow_ref[:, (2 * kv + p) * 128:(2 * kv + p + 1) * 128] = jnp.where(
                even, pv[(2 * p) * lc:(2 * p + 1) * lc], pv[(2 * p + 1) * lc:(2 * p + 2) * lc]).astype(BF16)
    for p in range(NA_HEADS // 2):
        k = n_ref[:, 1024 + p * 128:1024 + (p + 1) * 128]
        v = n_ref[:, 1536 + p * 128:1536 + (p + 1) * 128]
        qst = jnp.concatenate([n_ref[:, (2 * p) * 128:(2 * p + 1) * 128],
                               n_ref[:, (2 * p + 1) * 128:(2 * p + 2) * 128]], axis=0)
        s = _dot_nt(qst, k)
        ps = [_softmax_parts([s[i * lc:(i + 1) * lc]])[0] for i in range(2)]
        pv = _dot(jnp.concatenate(ps, axis=0), v)
        on_ref[:, p * 128:(p + 1) * 128] = jnp.where(even, pv[:lc], pv[lc:]).astype(BF16)


def _ctx_attention(sink, pc_win, pc_na, batch):
    tc = pc_win.shape[0]
    lc = tc // batch
    row = lambda b: (b, 0)
    return pl.pallas_call(
        _ctx_attn_kernel,
        out_shape=(jax.ShapeDtypeStruct((tc, 512), BF16), jax.ShapeDtypeStruct((tc, 512), BF16)),
        grid=(batch,),
        in_specs=[pl.BlockSpec(memory_space=pltpu.SMEM),
                  pl.BlockSpec((lc, pc_win.shape[1]), row), pl.BlockSpec((lc, pc_na.shape[1]), row)],
        out_specs=(pl.BlockSpec((lc, 512), row), pl.BlockSpec((lc, 512), row)),
        compiler_params=_cparams(("parallel",)),
        name="context_attn",
    )(sink, pc_win, pc_na)


def _merge_kernel(h_ref, yr_ref, yw_ref, yn_ref, gt_ref, mod_ref, g_ref, wb_ref, wo_ref, wr_ref,
                  ho_ref, f_ref, gate_ref, *, with_router):
    z = None
    for r, y_ref in enumerate((yr_ref, yw_ref, yn_ref)):
        gate = jax.nn.sigmoid(gt_ref[:, r * D_MODEL:(r + 1) * D_MODEL].astype(F32))
        t = gate * _dot(y_ref[...], wb_ref[r])
        z = t if z is None else z + t
    o = _dot(z.astype(BF16), wo_ref[...])
    h = h_ref[...] + mod_ref[0, 2:3, :] * _rms(o, g_ref[1:2, :])
    ho_ref[...] = h
    f = _rms(h, g_ref[2:3, :]) * (1.0 + mod_ref[0, 4:5, :]) + mod_ref[0, 3:4, :]
    f_ref[...] = f.astype(BF16)
    if not with_router:
        gate_ref[...] = jnp.zeros_like(gate_ref)
        return
    logits = jnp.dot(f, wr_ref[...], preferred_element_type=F32, precision=lax.Precision.HIGHEST)
    idx = lax.broadcasted_iota(jnp.int32, logits.shape, 1)
    m1 = jnp.max(logits, axis=-1, keepdims=True)
    i1 = jnp.min(jnp.where(logits == m1, idx, N_EXPERTS), axis=-1, keepdims=True)
    sel1 = idx == i1
    rest = jnp.where(sel1, -jnp.inf, logits)
    m2 = jnp.max(rest, axis=-1, keepdims=True)
    i2 = jnp.min(jnp.where(rest == m2, idx, N_EXPERTS), axis=-1, keepdims=True)
    sel2 = idx == i2
    e2 = jnp.exp(m2 - m1)
    inv = 1.0 / (1.0 + e2)
    gate_ref[...] = jnp.where(sel1, inv, 0.0) + jnp.where(sel2, e2 * inv, 0.0)


def _merge(h, y_ret, y_win, y_na, gates, mod, gains, w_branch, w_out, w_router, *, with_router, is_ctx,
           per_batch, tm=256):
    t, d = h.shape
    if is_ctx:
        mod_map = lambda i: (mod.shape[0] - 1, 0, 0)
    else:
        mod_map = lambda i: (i // (per_batch // tm), 0, 0)
    row = lambda i: (i, 0)
    full2 = lambda i: (0, 0)
    return pl.pallas_call(
        functools.partial(_merge_kernel, with_router=with_router),
        out_shape=(jax.ShapeDtypeStruct((t, d), F32), jax.ShapeDtypeStruct((t, d), BF16),
                   jax.ShapeDtypeStruct((t, N_EXPERTS), F32)),
        grid=(t // tm,),
        in_specs=[pl.BlockSpec((tm, d), row),
                  pl.BlockSpec((tm, BRANCH_W), row), pl.BlockSpec((tm, BRANCH_W), row),
                  pl.BlockSpec((tm, BRANCH_W), row),
                  pl.BlockSpec((tm, N_BRANCH * d), row),
                  pl.BlockSpec((1, 6, d), mod_map),
                  pl.BlockSpec(gains.shape, full2),
                  pl.BlockSpec(w_branch.shape, lambda i: (0, 0, 0)),
                  pl.BlockSpec(w_out.shape, full2),
                  pl.BlockSpec(w_router.shape, full2)],
        out_specs=(pl.BlockSpec((tm, d), row), pl.BlockSpec((tm, d), row),
                   pl.BlockSpec((tm, N_EXPERTS), row)),
        compiler_params=_cparams(("parallel",)),
        name="merge_ctx" if is_ctx else "merge_lat",
    )(h, y_ret, y_win, y_na, gates, mod, gains, w_branch, w_out, w_router)


def _ffn_kernel(h_ref, f_ref, gate_ref, mod_ref, g_ref, w1_ref, w3_ref, w2_ref, o_ref, acc_ref, *, use_gates):
    e = pl.program_id(1)
    c = pl.program_id(2)

    @pl.when((e == 0) & (c == 0))
    def _():
        acc_ref[...] = jnp.zeros_like(acc_ref)

    f = f_ref[...]
    a1 = _dot(f, w1_ref[0])
    a3 = _dot(f, w3_ref[0])
    act = (a1 * jax.nn.sigmoid(a1) * a3).astype(BF16)
    y = _dot(act, w2_ref[0])
    if use_gates:
        gts = gate_ref[...]
        lane = lax.broadcasted_iota(jnp.int32, gts.shape, 1)
        y = y * jnp.sum(jnp.where(lane == e, gts, 0.0), axis=-1, keepdims=True)
    acc_ref[...] += y

    @pl.when((e == pl.num_programs(1) - 1) & (c == pl.num_programs(2) - 1))
    def _():
        o_ref[...] = h_ref[...] + mod_ref[0, 5:6, :] * _rms(acc_ref[...], g_ref[3:4, :])


def _ffn(h, f, gates, mod, gains, w1, w3, w2, *, use_gates, is_ctx, per_batch, tm, fc):
    t, d = h.shape
    ne, _, dff = w1.shape
    if is_ctx:
        mod_map = lambda i, e, c: (mod.shape[0] - 1, 0, 0)
    else:
        mod_map = lambda i, e, c: (i // (per_batch // tm), 0, 0)
    row = lambda i, e, c: (i, 0)
    return pl.pallas_call(
        functools.partial(_ffn_kernel, use_gates=use_gates),
        out_shape=jax.ShapeDtypeStruct((t, d), F32),
        grid=(t // tm, ne, dff // fc),
        in_specs=[pl.BlockSpec((tm, d), row), pl.BlockSpec((tm, d), row),
                  pl.BlockSpec((tm, N_EXPERTS), row),
                  pl.BlockSpec((1, 6, d), mod_map),
                  pl.BlockSpec(gains.shape, lambda i, e, c: (0, 0)),
                  pl.BlockSpec((1, d, fc), lambda i, e, c: (e, 0, c)),
                  pl.BlockSpec((1, d, fc), lambda i, e, c: (e, 0, c)),
                  pl.BlockSpec((1, fc, d), lambda i, e, c: (e, c, 0))],
        out_specs=pl.BlockSpec((tm, d), row),
        scratch_shapes=[pltpu.VMEM((tm, d), F32)],
        compiler_params=_cparams(("parallel", "arbitrary", "arbitrary")),
        name=("moe" if use_gates else "ffn") + ("_ctx" if is_ctx else "_lat"),
    )(h, f, gates, mod, gains, w1, w3, w2)


def _rope_tables(s, lc):
    def lin(pos):
        n_freq = RET_DK // 2
        inv = ROPE_BASE ** (-jnp.arange(n_freq, dtype=F32) / n_freq)
        ang = pos.astype(F32)[:, None] * inv
        cos, sin = jnp.cos(ang), jnp.sin(ang)
        return jnp.concatenate([cos, cos], -1), jnp.concatenate([-sin, sin], -1)

    t = jnp.arange(s)
    n_freq = WIN_HD // 4
    inv = ROPE_BASE ** (-jnp.arange(n_freq, dtype=F32) / n_freq)
    rowa = (t // GRID_W).astype(F32)[:, None] * inv
    cola = (t % GRID_W).astype(F32)[:, None] * inv
    ang = jnp.concatenate([rowa, cola], axis=-1)
    cos, sin = jnp.cos(ang), jnp.sin(ang)
    ca = jnp.tile(jnp.concatenate([cos, cos], -1), (1, 2))
    sa = jnp.tile(jnp.concatenate([-sin, sin], -1), (1, 2))
    cr_l, sr_l = lin(lc + jnp.arange(s))
    cr_c, sr_c = lin(jnp.arange(lc))
    return (cr_l, sr_l, ca, sa), (cr_c, sr_c, ca[:lc], sa[:lc])


def kernel(x, c, ctx, c_ctx, w_mod, b_mod, norm_gains, w_in, ret_decay, win_sink, na_rpb, w_branch, w_out,
           ffn_w1, ffn_w3, ffn_w2, moe_router, moe_w1, moe_w3, moe_w2):
    batch, s, d = x.shape
    lc = ctx.shape[1]
    depth = w_mod.shape[0]

    c_all = jnp.zeros((16, d), F32).at[:batch].set(c).at[batch].set(c_ctx)
    mods = _modulation(c_all, w_mod, b_mod)[:, :batch + 1].reshape(depth, batch + 1, 6, d)

    tabs_l, tabs_c = _rope_tables(s, lc)
    log_gamma = jnp.log1p(-jnp.exp(ret_decay.astype(F32)))

    h_l = x.reshape(batch * s, d)
    h_c = ctx.reshape(batch * lc, d)
    zero_router = jnp.zeros((d, N_EXPERTS), F32)

    for layer in range(depth):
        need_ctx = layer < depth - 1
        mod = mods[layer]
        gains = norm_gains[layer]
        w_in_b = w_in[layer].astype(BF16)
        wb_b = w_branch[layer].astype(BF16)
        wo_b = w_out[layer].astype(BF16)
        i = layer // 2
        is_moe = layer % 2 == 1
        if is_moe:
            w1, w3, w2 = moe_w1[i].astype(BF16), moe_w3[i].astype(BF16), moe_w2[i].astype(BF16)
            w_router = moe_router[i]
            fc, tm_ffn = 512, 1024
        else:
            w1, w3, w2 = (ffn_w1[i][None].astype(BF16), ffn_w3[i][None].astype(BF16),
                          ffn_w2[i][None].astype(BF16))
            w_router = zero_router
            fc, tm_ffn = 1408, 512

        p_ret_l, p_win_l, p_na_l, p_gt_l = _inproj(h_l, mod, gains, w_in_b, *tabs_l, is_ctx=False)
        p_ret_c, p_win_c, p_na_c, p_gt_c = _inproj(h_c, mod, gains, w_in_b, *tabs_c, is_ctx=True)

        y_ret_l, y_ret_c = _retention(log_gamma[layer], p_ret_l, p_ret_c, batch)
        y_win_l = _window(win_sink[layer], p_win_l, p_win_c, batch)
        y_na_l = _neighbourhood(p_na_l, p_na_c, _na_bias_table(na_rpb[layer], s), batch)

        h_l, f_l, g_l = _merge(h_l, y_ret_l, y_win_l, y_na_l, p_gt_l, mod, gains, wb_b, wo_b, w_router,
                               with_router=is_moe, is_ctx=False, per_batch=s)
        h_l = _ffn(h_l, f_l, g_l, mod, gains, w1, w3, w2, use_gates=is_moe, is_ctx=False,
                   per_batch=s, tm=tm_ffn, fc=fc)
        if need_ctx:
            y_win_c, y_na_c = _ctx_attention(win_sink[layer], p_win_c, p_na_c, batch)
            h_c, f_c, g_c = _merge(h_c, y_ret_c, y_win_c, y_na_c, p_gt_c, mod, gains, wb_b, wo_b, w_router,
                                   with_router=is_moe, is_ctx=True, per_batch=lc)
            h_c = _ffn(h_c, f_c, g_c, mod, gains, w1, w3, w2, use_gates=is_moe, is_ctx=True,
                       per_batch=lc, tm=tm_ffn, fc=fc)
    return h_l.reshape(batch, s, d)
```

```python
import functools
import math

import jax
import jax.numpy as jnp
import numpy as np
from jax import lax
from jax.experimental import pallas as pl
from jax.experimental.pallas import tpu as pltpu

F32 = jnp.float32
BF16 = jnp.bfloat16

D_MODEL = 1024
GRID_W = 64
RET_HEADS = 4
RET_DK = 128
WIN_HEADS = 8
WIN_KV_HEADS = 2
WIN_HD = 64
WINDOW = 128
WIN_BLOCK = 128
NA_HEADS = 8
NA_HD = 64
NA_ROWS_MAX = 8
NA_COLS = 16
BRANCH_W = 512
N_BRANCH = 3
N_EXPERTS = 8
ROPE_BASE = 10000.0
NORM_EPS = 1e-6
NEG_INF = -1e30

LANES = 128
VMEM_LIMIT = 56 * 1024 * 1024

C_RQ, C_RK, C_RV, C_RG = 0, 512, 1024, 1536
C_WQ, C_WK, C_WV = 2048, 2560, 2688
C_NQ, C_NK, C_NV = 2816, 3328, 3840
C_GATE = 4352
D_IN = 7424

RET_CHUNK = 256
NA_QROWS = 4
NA_KROWS = 12


def _cparams(sem):
    return pltpu.CompilerParams(dimension_semantics=sem, vmem_limit_bytes=VMEM_LIMIT)


def _rms(x, g):
    return x * lax.rsqrt(jnp.mean(x * x, axis=-1, keepdims=True) + NORM_EPS) * g


def _dot(a, b):
    return jnp.dot(a, b, preferred_element_type=F32)


def _dot_nt(a, b):
    return lax.dot_general(a, b, (((1,), (1,)), ((), ())), preferred_element_type=F32)


def _dot_tn(a, b):
    return lax.dot_general(a, b, (((0,), (0,)), ((), ())), preferred_element_type=F32)


def _mod_kernel(c_ref, w_ref, b_ref, o_ref):
    c = c_ref[...]
    s = c * jax.nn.sigmoid(c)
    o_ref[0] = jnp.dot(s, w_ref[0], preferred_element_type=F32,
                       precision=lax.Precision.HIGHEST) + b_ref[0]


def _modulation(c_all, w_mod, b_mod):
    depth, d, n = w_mod.shape
    tn = 1536
    return pl.pallas_call(
        _mod_kernel,
        out_shape=jax.ShapeDtypeStruct((depth, c_all.shape[0], n), F32),
        grid=(depth, n // tn),
        in_specs=[pl.BlockSpec(c_all.shape, lambda l, j: (0, 0)),
                  pl.BlockSpec((1, d, tn), lambda l, j: (l, 0, j)),
                  pl.BlockSpec((1, 1, tn), lambda l, j: (l, 0, j))],
        out_specs=pl.BlockSpec((1, c_all.shape[0], tn), lambda l, j: (l, 0, j)),
        compiler_params=_cparams(("arbitrary", "arbitrary")),
        name="modulation",
    )(c_all, w_mod, b_mod.reshape(depth, 1, n))


def _inproj_kernel(h_ref, mod_ref, g_ref, w_ref, cr_ref, sr_ref, ca_ref, sa_ref,
                   oret_ref, owin_ref, ona_ref, og_ref, *, rope_win):
    x = h_ref[...]
    a = _rms(x, g_ref[0:1, :]) * (1.0 + mod_ref[0, 1:2, :]) + mod_ref[0, 0:1, :]
    a = a.astype(BF16)
    tm = x.shape[0]
    lane = lax.broadcasted_iota(jnp.int32, (tm, LANES), 1)
    even = lane < 64
    first_half = (lane % 64) < 32

    def mm(c0, cw):
        return _dot(a, w_ref[:, c0:c0 + cw])

    def rope_lin(v):
        return v * cr_ref[...] + pltpu.roll(v, 64, 1) * sr_ref[...]

    def rope_ax(v):
        rot = jnp.where(first_half, pltpu.roll(v, 96, 1), pltpu.roll(v, 32, 1))
        return v * ca_ref[...] + rot * sa_ref[...]

    r = mm(C_RQ, 512)
    for h in range(RET_HEADS):
        blk = rope_lin(r[:, h * 128:(h + 1) * 128]) * (RET_DK ** -0.5)
        oret_ref[:, h * 128:(h + 1) * 128] = blk.astype(BF16)
    r = mm(C_RK, 512)
    for h in range(RET_HEADS):
        blk = rope_lin(r[:, h * 128:(h + 1) * 128])
        oret_ref[:, 512 + h * 128:512 + (h + 1) * 128] = blk.astype(BF16)
    oret_ref[:, 1024:1536] = mm(C_RV, 512).astype(BF16)
    oret_ref[:, 1536:2048] = mm(C_RG, 512).astype(BF16)

    r = mm(C_WQ, 512)
    for p in range(WIN_HEADS // 2):
        blk = r[:, p * 128:(p + 1) * 128]
        if rope_win:
            blk = rope_ax(blk)
        blk = blk * (WIN_HD ** -0.5)
        owin_ref[:, (2 * p) * 128:(2 * p + 1) * 128] = jnp.where(even, blk, 0.0).astype(BF16)
        owin_ref[:, (2 * p + 1) * 128:(2 * p + 2) * 128] = jnp.where(even, 0.0, blk).astype(BF16)
    r = mm(C_WK, 256)
    k2 = r[:, 0:128]
    if rope_win:
        k2 = rope_ax(k2)
    v2 = r[:, 128:256]
    for i, t in enumerate((k2, v2)):
        sw = pltpu.roll(t, 64, 1)
        owin_ref[:, 1024 + 256 * i:1024 + 256 * i + 128] = jnp.where(even, t, sw).astype(BF16)
        owin_ref[:, 1024 + 256 * i + 128:1024 + 256 * i + 256] = jnp.where(even, sw, t).astype(BF16)

    r = mm(C_NQ, 512) * (NA_HD ** -0.5)
    for p in range(NA_HEADS // 2):
        blk = r[:, p * 128:(p + 1) * 128]
        ona_ref[:, (2 * p) * 128:(2 * p + 1) * 128] = jnp.where(even, blk, 0.0).astype(BF16)
        ona_ref[:, (2 * p + 1) * 128:(2 * p + 2) * 128] = jnp.where(even, 0.0, blk).astype(BF16)
    ona_ref[:, 1024:1536] = mm(C_NK, 512).astype(BF16)
    ona_ref[:, 1536:2048] = mm(C_NV, 512).astype(BF16)

    for j in range(6):
        og_ref[:, j * 512:(j + 1) * 512] = mm(C_GATE + j * 512, 512).astype(BF16)


def _inproj(h, mod, gains, w_in, cr, sr, ca, sa, *, is_ctx, tm=256):
    t, d = h.shape
    nt = t // tm
    per_batch = cr.shape[0] // tm
    if is_ctx:
        mod_map = lambda i: (mod.shape[0] - 1, 0, 0)
    else:
        mod_map = lambda i: (i // per_batch, 0, 0)
    pos_map = lambda i: (i % per_batch, 0)
    full = lambda i: (0, 0)
    row = lambda i: (i, 0)
    outs = (jax.ShapeDtypeStruct((t, 2048), BF16), jax.ShapeDtypeStruct((t, 1536), BF16),
            jax.ShapeDtypeStruct((t, 2048), BF16), jax.ShapeDtypeStruct((t, 3072), BF16))
    return pl.pallas_call(
        functools.partial(_inproj_kernel, rope_win=not is_ctx),
        out_shape=outs,
        grid=(nt,),
        in_specs=[pl.BlockSpec((tm, d), row),
                  pl.BlockSpec((1, 6, d), mod_map),
                  pl.BlockSpec(gains.shape, full),
                  pl.BlockSpec(w_in.shape, full),
                  pl.BlockSpec((tm, LANES), pos_map), pl.BlockSpec((tm, LANES), pos_map),
                  pl.BlockSpec((tm, LANES), pos_map), pl.BlockSpec((tm, LANES), pos_map)],
        out_specs=(pl.BlockSpec((tm, 2048), row), pl.BlockSpec((tm, 1536), row),
                   pl.BlockSpec((tm, 2048), row), pl.BlockSpec((tm, 3072), row)),
        compiler_params=_cparams(("parallel",)),
        name="inproj_ctx" if is_ctx else "inproj_lat",
    )(h, mod, gains, w_in, cr, sr, ca, sa)


def _ret_kernel(lg_ref, ql, kl, vl, gl, qc, kc, vc, gc, yl_ref, yc_ref, acc_ref):
    C = RET_CHUNK
    h = pl.program_id(1)
    lgf = lg_ref[0, h]
    lgb = lg_ref[1, h]
    ii = lax.broadcasted_iota(jnp.int32, (C, C), 0).astype(F32)
    jj = lax.broadcasted_iota(jnp.int32, (C, C), 1).astype(F32)
    diff = ii - jj
    dmat = jnp.where(diff >= 0, jnp.exp(lgf * jnp.maximum(diff, 0.0)),
                     jnp.exp(lgb * jnp.maximum(-diff, 0.0)))
    r = lax.broadcasted_iota(jnp.int32, (C, RET_DK), 0).astype(F32)
    qdf = jnp.exp(lgf * (r + 1.0))
    kdf = jnp.exp(lgf * (C - 1.0 - r))
    qdb = jnp.exp(lgb * (C - r))
    kdb = jnp.exp(lgb * r)
    ones = jnp.ones((RET_DK, RET_DK), F32)
    cdf = jnp.exp(ones * (lgf * C))
    cdb = jnp.exp(ones * (lgb * C))

    n_lat = ql.shape[0] // C
    lc = qc.shape[0]
    chunks = [(qc, kc, vc, gc, yc_ref, 0, 0)]
    chunks += [(ql, kl, vl, gl, yl_ref, j * C, lc + j * C) for j in range(n_lat)]

    state = jnp.zeros((RET_DK, RET_DK), F32)
    for (qr, kr, vr, _, _, off, aoff) in chunks:
        q = qr[off:off + C, :]
        k = kr[off:off + C, :]
        v = vr[off:off + C, :]
        s = _dot_nt(q, k) * dmat
        y = _dot(s.astype(BF16), v)
        y = y + _dot((q.astype(F32) * qdf).astype(BF16), state.astype(BF16))
        state = state * cdf + _dot_tn((k.astype(F32) * kdf).astype(BF16), v)
        acc_ref[aoff:aoff + C, :] = y

    state = jnp.zeros((RET_DK, RET_DK), F32)
    for (qr, kr, vr, gr, yr, off, aoff) in [chunks[0]] + chunks[:0:-1]:
        q = qr[off:off + C, :]
        k = kr[off:off + C, :]
        v = vr[off:off + C, :]
        y = acc_ref[aoff:aoff + C, :] + _dot((q.astype(F32) * qdb).astype(BF16), state.astype(BF16))
        state = state * cdb + _dot_tn((k.astype(F32) * kdb).astype(BF16), v)
        mu = jnp.mean(y, axis=-1, keepdims=True)
        dlt = y - mu
        var = jnp.mean(dlt * dlt, axis=-1, keepdims=True)
        yn = dlt * lax.rsqrt(var + NORM_EPS)
        g = gr[off:off + C, :].astype(F32)
        yr[off:off + C, :] = (g * jax.nn.sigmoid(g) * yn).astype(BF16)


def _retention(lg, pl_ret, pc_ret, batch):
    tl = pl_ret.shape[0]
    tc = pc_ret.shape[0]
    s = tl // batch
    lc = tc // batch
    assert lc == RET_CHUNK and s % RET_CHUNK == 0
    lat = lambda off: pl.BlockSpec((s, 128), lambda b, h, off=off: (b, off + h))
    ctx = lambda off: pl.BlockSpec((lc, 128), lambda b, h, off=off: (b, off + h))
    return pl.pallas_call(
        _ret_kernel,
        out_shape=(jax.ShapeDtypeStruct((tl, 512), BF16), jax.ShapeDtypeStruct((tc, 512), BF16)),
        grid=(batch, RET_HEADS),
        in_specs=[pl.BlockSpec(memory_space=pltpu.SMEM),
                  lat(0), lat(4), lat(8), lat(12), ctx(0), ctx(4), ctx(8), ctx(12)],
        out_specs=(pl.BlockSpec((s, 128), lambda b, h: (b, h)),
                   pl.BlockSpec((lc, 128), lambda b, h: (b, h))),
        scratch_shapes=[pltpu.VMEM((s + lc, 128), F32)],
        compiler_params=_cparams(("parallel", "parallel")),
        name="retention",
    )(lg, pl_ret, pl_ret, pl_ret, pl_ret, pc_ret, pc_ret, pc_ret, pc_ret)


def _softmax_parts(parts, sink=None):
    m = functools.reduce(jnp.maximum, [jnp.max(p, axis=-1, keepdims=True) for p in parts])
    if sink is not None:
        m = jnp.maximum(m, sink)
    es = [jnp.exp(p - m) for p in parts]
    den = functools.reduce(lambda a, b: a + b, [jnp.sum(e, axis=-1, keepdims=True) for e in es])
    if sink is not None:
        den = den + jnp.exp(sink - m)
    inv = 1.0 / den
    return [(e * inv).astype(BF16) for e in es]


def _win_kernel(sink_ref, q_ref, k0, k1, v0, v1, ck0, ck1, cv0, cv1, o_ref):
    blk = WIN_BLOCK
    nw = 3 * blk
    s_len = k0.shape[0]
    j = pl.program_id(1)
    start = pl.multiple_of(jnp.clip((j - 1) * blk, 0, s_len - nw), blk)
    qpos = j * blk + lax.broadcasted_iota(jnp.int32, (blk, nw), 0)
    kpos = start + lax.broadcasted_iota(jnp.int32, (blk, nw), 1)
    bias = jnp.where(jnp.abs(qpos - kpos) <= WINDOW, 0.0, NEG_INF)
    even = lax.broadcasted_iota(jnp.int32, (blk, LANES), 1) < 64
    g = WIN_HEADS // WIN_KV_HEADS
    for kv, (kr, vr, ckr, cvr) in enumerate(((k0, v0, ck0, cv0), (k1, v1, ck1, cv1))):
        kcat = jnp.concatenate([kr[pl.ds(start, nw), :], ckr[...]], axis=0)
        vcat = jnp.concatenate([vr[pl.ds(start, nw), :], cvr[...]], axis=0)
        qst = jnp.concatenate([q_ref[:, (g * kv + i) * 128:(g * kv + i + 1) * 128] for i in range(g)], axis=0)
        s = _dot_nt(qst, kcat)
        ps = []
        for i in range(g):
            sb = s[i * blk:(i + 1) * blk]
            pw, pc = _softmax_parts([sb[:, :nw] + bias, sb[:, nw:]], sink_ref[g * kv + i])
            ps.append(jnp.concatenate([pw, pc], axis=1))
        pv = _dot(jnp.concatenate(ps, axis=0), vcat)
        for p in range(g // 2):
            o_ref[:, (2 * kv + p) * 128:(2 * kv + p + 1) * 128] = jnp.where(
                even, pv[(2 * p) * blk:(2 * p + 1) * blk], pv[(2 * p + 1) * blk:(2 * p + 2) * blk]).astype(BF16)


def _window(sink, pl_win, pc_win, batch):
    tl = pl_win.shape[0]
    tc = pc_win.shape[0]
    s = tl // batch
    lc = tc // batch
    nb = s // WIN_BLOCK
    lat = lambda cb: pl.BlockSpec((s, 128), lambda b, j, cb=cb: (b, cb))
    ctx = lambda cb: pl.BlockSpec((lc, 128), lambda b, j, cb=cb: (b, cb))
    return pl.pallas_call(
        _win_kernel,
        out_shape=jax.ShapeDtypeStruct((tl, 512), BF16),
        grid=(batch, nb),
        in_specs=[pl.BlockSpec(memory_space=pltpu.SMEM),
                  pl.BlockSpec((WIN_BLOCK, 1024), lambda b, j: (b * nb + j, 0)),
                  lat(8), lat(9), lat(10), lat(11), ctx(8), ctx(9), ctx(10), ctx(11)],
        out_specs=pl.BlockSpec((WIN_BLOCK, 512), lambda b, j: (b * nb + j, 0)),
        compiler_params=_cparams(("parallel", "arbitrary")),
        name="window_attn",
    )(sink, pl_win, pl_win, pl_win, pl_win, pl_win, pc_win, pc_win, pc_win, pc_win)


def _na_kernel(q_ref, ka, kb, kc, va, vb, vc, ck, cv, bias_ref, o_ref):
    nq = q_ref.shape[0]
    qst = jnp.concatenate([q_ref[:, 0:128], q_ref[:, 128:256]], axis=0)
    kcat = jnp.concatenate([ka[...], kb[...], kc[...], ck[...]], axis=0)
    vcat = jnp.concatenate([va[...], vb[...], vc[...], cv[...]], axis=0)
    n_nb = ka.shape[0] * 3
    s = _dot_nt(qst, kcat)
    ps = []
    for par in range(2):
        sb = s[par * nq:(par + 1) * nq]
        pn, pc = _softmax_parts([sb[:, :n_nb] + bias_ref[0, 0, par], sb[:, n_nb:]])
        ps.append(jnp.concatenate([pn, pc], axis=1))
    pv = _dot(jnp.concatenate(ps, axis=0), vcat)
    even = lax.broadcasted_iota(jnp.int32, (nq, LANES), 1) < 64
    o_ref[...] = jnp.where(even, pv[:nq], pv[nq:]).astype(BF16)


def _na_pattern(i):
    return jnp.where(i == 0, 0, jnp.where(i == 7, 2, 1))


def _neighbourhood(pl_na, pc_na, bias, batch):
    tl = pl_na.shape[0]
    tc = pc_na.shape[0]
    s = tl // batch
    lc = tc // batch
    nq = NA_QROWS * GRID_W
    ng = s // nq
    kblk = nq
    nkb = s // kblk
    assert ng == 8 and NA_KROWS * GRID_W == 3 * kblk
    hp = NA_HEADS // 2

    def kspec(cb0, d):
        return pl.BlockSpec(
            (kblk, 128),
            lambda p, i, b, d=d, cb0=cb0: (b * nkb + jnp.clip(i - 1, 0, nkb - 3) + d, cb0 + p))

    return pl.pallas_call(
        _na_kernel,
        out_shape=jax.ShapeDtypeStruct((tl, 512), BF16),
        grid=(hp, ng, batch),
        in_specs=[pl.BlockSpec((nq, 256), lambda p, i, b: (b * ng + i, p)),
                  kspec(8, 0), kspec(8, 1), kspec(8, 2),
                  kspec(12, 0), kspec(12, 1), kspec(12, 2),
                  pl.BlockSpec((lc, 128), lambda p, i, b: (b, 8 + p)),
                  pl.BlockSpec((lc, 128), lambda p, i, b: (b, 12 + p)),
                  pl.BlockSpec((1, 1, 2, nq, 3 * kblk), lambda p, i, b: (p, _na_pattern(i), 0, 0, 0))],
        out_specs=pl.BlockSpec((nq, 128), lambda p, i, b: (b * ng + i, p)),
        compiler_params=_cparams(("parallel", "parallel", "arbitrary")),
        name="neighbourhood_attn",
    )(pl_na, pl_na, pl_na, pl_na, pl_na, pl_na, pl_na, pc_na, pc_na, bias)


def _na_bias_table(rpb, s):
    rows = s // GRID_W
    kr = min(NA_ROWS_MAX, rows)
    tabs = []
    for i in (0, 1, rows // NA_QROWS - 1):
        r = NA_QROWS * i + np.arange(NA_QROWS)
        k0 = NA_QROWS * int(np.clip(i - 1, 0, rows // NA_QROWS - 3))
        krow = k0 + np.arange(NA_KROWS)
        r_start = np.clip(r - kr // 2, 0, rows - kr)
        row_ok = (krow[None, :] >= r_start[:, None]) & (krow[None, :] < r_start[:, None] + kr)
        dr = np.clip(krow[None, :] - r[:, None] + (NA_ROWS_MAX - 1), 0, 2 * NA_ROWS_MAX - 2)
        cq = np.arange(GRID_W)
        ck = np.arange(GRID_W)
        c_start = np.clip(cq - NA_COLS // 2, 0, GRID_W - NA_COLS)
        col_ok = (ck[None, :] >= c_start[:, None]) & (ck[None, :] < c_start[:, None] + NA_COLS)
        dc = np.clip(ck[None, :] - cq[:, None] + (NA_COLS - 1), 0, 2 * NA_COLS - 2)
        ok = row_ok[:, None, :, None] & col_ok[None, :, None, :]
        sel_r = (dr[:, :, None] == np.arange(2 * NA_ROWS_MAX - 1)).astype(np.float32)
        sel_c = (dc[:, :, None] == np.arange(2 * NA_COLS - 1)).astype(np.float32)
        b = jnp.einsum("qkr,hrc,xyc->hqxky", sel_r, rpb.astype(F32), sel_c,
                       precision=lax.Precision.HIGHEST)
        b = jnp.where(ok[None], b, NEG_INF)
        tabs.append(b.reshape(NA_HEADS, NA_QROWS * GRID_W, NA_KROWS * GRID_W))
    t = jnp.stack(tabs, axis=1)
    t = t.reshape(NA_HEADS // 2, 2, 3, t.shape[2], t.shape[3])
    return jnp.swapaxes(t, 1, 2)


def _ctx_attn_kernel(sink_ref, w_ref, n_ref, ow_ref, on_ref):
    lc = w_ref.shape[0]
    even = lax.broadcasted_iota(jnp.int32, (lc, LANES), 1) < 64
    g = WIN_HEADS // WIN_KV_HEADS
    for kv in range(WIN_KV_HEADS):
        k = w_ref[:, 1024 + kv * 128:1024 + (kv + 1) * 128]
        v = w_ref[:, 1280 + kv * 128:1280 + (kv + 1) * 128]
        qst = jnp.concatenate([w_ref[:, (g * kv + i) * 128:(g * kv + i + 1) * 128] for i in range(g)], axis=0)
        s = _dot_nt(qst, k)
        ps = [_softmax_parts([s[i * lc:(i + 1) * lc]], sink_ref[g * kv + i])[0] for i in range(g)]
        pv = _dot(jnp.concatenate(ps, axis=0), v)
        for p in range(g // 2):
            ow_ref[:, (2 * kv + p) * 128:(2 * kv + p + 1) * 128] = jnp.where(
                even, pv[(2 * p) * lc:(2 * p + 1) * lc], pv[(2 * p + 1) * lc:(2 * p + 2) * lc]).astype(BF16)
    for p in range(NA_HEADS // 2):
        k = n_ref[:, 1024 + p * 128:1024 + (p + 1) * 128]
        v = n_ref[:, 1536 + p * 128:1536 + (p + 1) * 128]
        qst = jnp.concatenate([n_ref[:, (2 * p) * 128:(2 * p + 1) * 128],
                               n_ref[:, (2 * p + 1) * 128:(2 * p + 2) * 128]], axis=0)
        s = _dot_nt(qst, k)
        ps = [_softmax_parts([s[i * lc:(i + 1) * lc]])[0] for i in range(2)]
        pv = _dot(jnp.concatenate(ps, axis=0), v)
        on_ref[:, p * 128:(p + 1) * 128] = jnp.where(even, pv[:lc], pv[lc:]).astype(BF16)


def _ctx_attention(sink, pc_win, pc_na, batch):
    tc = pc_win.shape[0]
    lc = tc // batch
    row = lambda b: (b, 0)
    return pl.pallas_call(
        _ctx_attn_kernel,
        out_shape=(jax.ShapeDtypeStruct((tc, 512), BF16), jax.ShapeDtypeStruct((tc, 512), BF16)),
        grid=(batch,),
        in_specs=[pl.BlockSpec(memory_space=pltpu.SMEM),
                  pl.BlockSpec((lc, pc_win.shape[1]), row), pl.BlockSpec((lc, pc_na.shape[1]), row)],
        out_specs=(pl.BlockSpec((lc, 512), row), pl.BlockSpec((lc, 512), row)),
        compiler_params=_cparams(("parallel",)),
        name="context_attn",
    )(sink, pc_win, pc_na)


def _merge_kernel(h_ref, yr_ref, yw_ref, yn_ref, gt_ref, mod_ref, g_ref, wb_ref, wo_ref, wr_ref,
                  ho_ref, f_ref, gate_ref, *, with_router):
    z = None
    for r, y_ref in enumerate((yr_ref, yw_ref, yn_ref)):
        gate = jax.nn.sigmoid(gt_ref[:, r * D_MODEL:(r + 1) * D_MODEL].astype(F32))
        t = gate * _dot(y_ref[...], wb_ref[r])
        z = t if z is None else z + t
    o = _dot(z.astype(BF16), wo_ref[...])
    h = h_ref[...] + mod_ref[0, 2:3, :] * _rms(o, g_ref[1:2, :])
    ho_ref[...] = h
    f = _rms(h, g_ref[2:3, :]) * (1.0 + mod_ref[0, 4:5, :]) + mod_ref[0, 3:4, :]
    f_ref[...] = f.astype(BF16)
    if not with_router:
        gate_ref[...] = jnp.zeros_like(gate_ref)
        return
    logits = jnp.dot(f, wr_ref[...], preferred_element_type=F32, precision=lax.Precision.HIGHEST)
    idx = lax.broadcasted_iota(jnp.int32, logits.shape, 1)
    m1 = jnp.max(logits, axis=-1, keepdims=True)
    i1 = jnp.min(jnp.where(logits == m1, idx, N_EXPERTS), axis=-1, keepdims=True)
    sel1 = idx == i1
    rest = jnp.where(sel1, -jnp.inf, logits)
    m2 = jnp.max(rest, axis=-1, keepdims=True)
    i2 = jnp.min(jnp.where(rest == m2, idx, N_EXPERTS), axis=-1, keepdims=True)
    sel2 = idx == i2
    e2 = jnp.exp(m2 - m1)
    inv = 1.0 / (1.0 + e2)
    gate_ref[...] = jnp.where(sel1, inv, 0.0) + jnp.where(sel2, e2 * inv, 0.0)


def _merge(h, y_ret, y_win, y_na, gates, mod, gains, w_branch, w_out, w_router, *, with_router, is_ctx,
           per_batch, tm=256):
    t, d = h.shape
    if is_ctx:
        mod_map = lambda i: (mod.shape[0] - 1, 0, 0)
    else:
        mod_map = lambda i: (i // (per_batch // tm), 0, 0)
    row = lambda i: (i, 0)
    full2 = lambda i: (0, 0)
    return pl.pallas_call(
        functools.partial(_merge_kernel, with_router=with_router),
        out_shape=(jax.ShapeDtypeStruct((t, d), F32), jax.ShapeDtypeStruct((t, d), BF16),
                   jax.ShapeDtypeStruct((t, N_EXPERTS), F32)),
        grid=(t // tm,),
        in_specs=[pl.BlockSpec((tm, d), row),
                  pl.BlockSpec((tm, BRANCH_W), row), pl.BlockSpec((tm, BRANCH_W), row),
                  pl.BlockSpec((tm, BRANCH_W), row),
                  pl.BlockSpec((tm, N_BRANCH * d), row),
                  pl.BlockSpec((1, 6, d), mod_map),
                  pl.BlockSpec(gains.shape, full2),
                  pl.BlockSpec(w_branch.shape, lambda i: (0, 0, 0)),
                  pl.BlockSpec(w_out.shape, full2),
                  pl.BlockSpec(w_router.shape, full2)],
        out_specs=(pl.BlockSpec((tm, d), row), pl.BlockSpec((tm, d), row),
                   pl.BlockSpec((tm, N_EXPERTS), row)),
        compiler_params=_cparams(("parallel",)),
        name="merge_ctx" if is_ctx else "merge_lat",
    )(h, y_ret, y_win, y_na, gates, mod, gains, w_branch, w_out, w_router)


def _ffn_kernel(h_ref, f_ref, gate_ref, mod_ref, g_ref, w1_ref, w3_ref, w2_ref, o_ref, acc_ref, *, use_gates):
    e = pl.program_id(1)
    c = pl.program_id(2)

    @pl.when((e == 0) & (c == 0))
    def _():
        acc_ref[...] = jnp.zeros_like(acc_ref)

    f = f_ref[...]
    a1 = _dot(f, w1_ref[0])
    a3 = _dot(f, w3_ref[0])
    act = (a1 * jax.nn.sigmoid(a1) * a3).astype(BF16)
    y = _dot(act, w2_ref[0])
    if use_gates:
        gts = gate_ref[...]
        lane = lax.broadcasted_iota(jnp.int32, gts.shape, 1)
        y = y * jnp.sum(jnp.where(lane == e, gts, 0.0), axis=-1, keepdims=True)
    acc_ref[...] += y

    @pl.when((e == pl.num_programs(1) - 1) & (c == pl.num_programs(2) - 1))
    def _():
        o_ref[...] = h_ref[...] + mod_ref[0, 5:6, :] * _rms(acc_ref[...], g_ref[3:4, :])


def _ffn(h, f, gates, mod, gains, w1, w3, w2, *, use_gates, is_ctx, per_batch, tm, fc):
    t, d = h.shape
    ne, _, dff = w1.shape
    if is_ctx:
        mod_map = lambda i, e, c: (mod.shape[0] - 1, 0, 0)
    else:
        mod_map = lambda i, e, c: (i // (per_batch // tm), 0, 0)
    row = lambda i, e, c: (i, 0)
    return pl.pallas_call(
        functools.partial(_ffn_kernel, use_gates=use_gates),
        out_shape=jax.ShapeDtypeStruct((t, d), F32),
        grid=(t // tm, ne, dff // fc),
        in_specs=[pl.BlockSpec((tm, d), row), pl.BlockSpec((tm, d), row),
                  pl.BlockSpec((tm, N_EXPERTS), row),
                  pl.BlockSpec((1, 6, d), mod_map),
                  pl.BlockSpec(gains.shape, lambda i, e, c: (0, 0)),
                  pl.BlockSpec((1, d, fc), lambda i, e, c: (e, 0, c)),
                  pl.BlockSpec((1, d, fc), lambda i, e, c: (e, 0, c)),
                  pl.BlockSpec((1, fc, d), lambda i, e, c: (e, c, 0))],
        out_specs=pl.BlockSpec((tm, d), row),
        scratch_shapes=[pltpu.VMEM((tm, d), F32)],
        compiler_params=_cparams(("parallel", "arbitrary", "arbitrary")),
        name=("moe" if use_gates else "ffn") + ("_ctx" if is_ctx else "_lat"),
    )(h, f, gates, mod, gains, w1, w3, w2)


def _rope_tables(s, lc):
    def lin(pos):
        n_freq = RET_DK // 2
        inv = ROPE_BASE ** (-jnp.arange(n_freq, dtype=F32) / n_freq)
        ang = pos.astype(F32)[:, None] * inv
        cos, sin = jnp.cos(ang), jnp.sin(ang)
        return jnp.concatenate([cos, cos], -1), jnp.concatenate([-sin, sin], -1)

    t = jnp.arange(s)
    n_freq = WIN_HD // 4
    inv = ROPE_BASE ** (-jnp.arange(n_freq, dtype=F32) / n_freq)
    rowa = (t // GRID_W).astype(F32)[:, None] * inv
    cola = (t % GRID_W).astype(F32)[:, None] * inv
    ang = jnp.concatenate([rowa, cola], axis=-1)
    cos, sin = jnp.cos(ang), jnp.sin(ang)
    ca = jnp.tile(jnp.concatenate([cos, cos], -1), (1, 2))
    sa = jnp.tile(jnp.concatenate([-sin, sin], -1), (1, 2))
    cr_l, sr_l = lin(lc + jnp.arange(s))
    cr_c, sr_c = lin(jnp.arange(lc))
    return (cr_l, sr_l, ca, sa), (cr_c, sr_c, ca[:lc], sa[:lc])


def kernel(x, c, ctx, c_ctx, w_mod, b_mod, norm_gains, w_in, ret_decay, win_sink, na_rpb, w_branch, w_out,
           ffn_w1, ffn_w3, ffn_w2, moe_router, moe_w1, moe_w3, moe_w2):
    batch, s, d = x.shape
    lc = ctx.shape[1]
    depth = w_mod.shape[0]

    c_all = jnp.zeros((16, d), F32).at[:batch].set(c).at[batch].set(c_ctx)
    mods = _modulation(c_all, w_mod, b_mod)[:, :batch + 1].reshape(depth, batch + 1, 6, d)

    tabs_l, tabs_c = _rope_tables(s, lc)
    log_gamma = jnp.log1p(-jnp.exp(ret_decay.astype(F32)))

    h_l = x.reshape(batch * s, d)
    h_c = ctx.reshape(batch * lc, d)
    zero_router = jnp.zeros((d, N_EXPERTS), F32)

    for layer in range(depth):
        need_ctx = layer < depth - 1
        mod = mods[layer]
        gains = norm_gains[layer]
        w_in_b = w_in[layer].astype(BF16)
        wb_b = w_branch[layer].astype(BF16)
        wo_b = w_out[layer].astype(BF16)
        i = layer // 2
        is_moe = layer % 2 == 1
        if is_moe:
            w1, w3, w2 = moe_w1[i].astype(BF16), moe_w3[i].astype(BF16), moe_w2[i].astype(BF16)
            w_router = moe_router[i]
            fc, tm_ffn = 512, 1024
        else:
            w1, w3, w2 = (ffn_w1[i][None].astype(BF16), ffn_w3[i][None].astype(BF16),
                          ffn_w2[i][None].astype(BF16))
            w_router = zero_router
            fc, tm_ffn = 1408, 512

        p_ret_l, p_win_l, p_na_l, p_gt_l = _inproj(h_l, mod, gains, w_in_b, *tabs_l, is_ctx=False)
        p_ret_c, p_win_c, p_na_c, p_gt_c = _inproj(h_c, mod, gains, w_in_b, *tabs_c, is_ctx=True)

        y_ret_l, y_ret_c = _retention(log_gamma[layer], p_ret_l, p_ret_c, batch)
        y_win_l = _window(win_sink[layer], p_win_l, p_win_c, batch)
        y_na_l = _neighbourhood(p_na_l, p_na_c, _na_bias_table(na_rpb[layer], s), batch)

        h_l, f_l, g_l = _merge(h_l, y_ret_l, y_win_l, y_na_l, p_gt_l, mod, gains, wb_b, wo_b, w_router,
                               with_router=is_moe, is_ctx=False, per_batch=s)
        h_l = _ffn(h_l, f_l, g_l, mod, gains, w1, w3, w2, use_gates=is_moe, is_ctx=False,
                   per_batch=s, tm=tm_ffn, fc=fc)
        if need_ctx:
            y_win_c, y_na_c = _ctx_attention(win_sink[layer], p_win_c, p_na_c, batch)
            h_c, f_c, g_c = _merge(h_c, y_ret_c, y_win_c, y_na_c, p_gt_c, mod, gains, wb_b, wo_b, w_router,
                                   with_router=is_moe, is_ctx=True, per_batch=lc)
            h_c = _ffn(h_c, f_c, g_c, mod, gains, w1, w3, w2, use_gates=is_moe, is_ctx=True,
                       per_batch=lc, tm=tm_ffn, fc=fc)
    return h_l.reshape(batch, s, d)
```

```python
import functools
import math

import jax
import jax.numpy as jnp
import numpy as np
from jax import lax
from jax.experimental import pallas as pl
from jax.experimental.pallas import tpu as pltpu

F32 = jnp.float32
BF16 = jnp.bfloat16

D_MODEL = 1024
GRID_W = 64
RET_HEADS = 4
RET_DK = 128
WIN_HEADS = 8
WIN_KV_HEADS = 2
WIN_HD = 64
WINDOW = 128
WIN_BLOCK = 128
NA_HEADS = 8
NA_HD = 64
NA_ROWS_MAX = 8
NA_COLS = 16
BRANCH_W = 512
N_BRANCH = 3
N_EXPERTS = 8
TOP_K = 2
ROPE_BASE = 10000.0
NORM_EPS = 1e-6
NEG_INF = -1e30

LANES = 128
VMEM_LIMIT = 56 * 1024 * 1024

C_RQ, C_RK, C_RV, C_RG = 0, 512, 1024, 1536
C_WQ, C_WK, C_WV = 2048, 2560, 2688
C_NQ, C_NK, C_NV = 2816, 3328, 3840
C_GATE = 4352
D_IN = 7424

RET_CHUNK = 256
NA_QROWS = 4
NA_KROWS = 12
MOE_TILE = 512
MOE_FC = 1792


def _cparams(sem):
    return pltpu.CompilerParams(dimension_semantics=sem, vmem_limit_bytes=VMEM_LIMIT)


def _rms(x, g):
    return x * lax.rsqrt(jnp.mean(x * x, axis=-1, keepdims=True) + NORM_EPS) * g


def _dot(a, b):
    return jnp.dot(a, b, preferred_element_type=F32)


def _dot_nt(a, b):
    return lax.dot_general(a, b, (((1,), (1,)), ((), ())), preferred_element_type=F32)


def _dot_tn(a, b):
    return lax.dot_general(a, b, (((0,), (0,)), ((), ())), preferred_element_type=F32)


def _mod_kernel(c_ref, w_ref, b_ref, o_ref):
    c = c_ref[...]
    s = c * jax.nn.sigmoid(c)
    o_ref[0] = jnp.dot(s, w_ref[0], preferred_element_type=F32,
                       precision=lax.Precision.HIGHEST) + b_ref[0]


def _modulation(c_all, w_mod, b_mod):
    depth, d, n = w_mod.shape
    tn = 1536
    return pl.pallas_call(
        _mod_kernel,
        out_shape=jax.ShapeDtypeStruct((depth, c_all.shape[0], n), F32),
        grid=(depth, n // tn),
        in_specs=[pl.BlockSpec(c_all.shape, lambda l, j: (0, 0)),
                  pl.BlockSpec((1, d, tn), lambda l, j: (l, 0, j)),
                  pl.BlockSpec((1, 1, tn), lambda l, j: (l, 0, j))],
        out_specs=pl.BlockSpec((1, c_all.shape[0], tn), lambda l, j: (l, 0, j)),
        compiler_params=_cparams(("arbitrary", "arbitrary")),
        name="modulation",
    )(c_all, w_mod, b_mod.reshape(depth, 1, n))


def _inproj_kernel(h_ref, mod_ref, g_ref, w_ref, cr_ref, sr_ref, ca_ref, sa_ref,
                   oret_ref, owin_ref, ona_ref, og_ref, *, rope_win):
    x = h_ref[...]
    a = _rms(x, g_ref[0:1, :]) * (1.0 + mod_ref[0, 1:2, :]) + mod_ref[0, 0:1, :]
    a = a.astype(BF16)
    tm = x.shape[0]
    lane = lax.broadcasted_iota(jnp.int32, (tm, LANES), 1)
    even = lane < 64
    first_half = (lane % 64) < 32

    def mm(c0, cw):
        return _dot(a, w_ref[:, c0:c0 + cw])

    def rope_lin(v):
        return v * cr_ref[...] + pltpu.roll(v, 64, 1) * sr_ref[...]

    def rope_ax(v):
        rot = jnp.where(first_half, pltpu.roll(v, 96, 1), pltpu.roll(v, 32, 1))
        return v * ca_ref[...] + rot * sa_ref[...]

    r = mm(C_RQ, 512)
    for h in range(RET_HEADS):
        blk = rope_lin(r[:, h * 128:(h + 1) * 128]) * (RET_DK ** -0.5)
        oret_ref[:, h * 128:(h + 1) * 128] = blk.astype(BF16)
    r = mm(C_RK, 512)
    for h in range(RET_HEADS):
        blk = rope_lin(r[:, h * 128:(h + 1) * 128])
        oret_ref[:, 512 + h * 128:512 + (h + 1) * 128] = blk.astype(BF16)
    oret_ref[:, 1024:1536] = mm(C_RV, 512).astype(BF16)
    oret_ref[:, 1536:2048] = mm(C_RG, 512).astype(BF16)

    r = mm(C_WQ, 512)
    for p in range(WIN_HEADS // 2):
        blk = r[:, p * 128:(p + 1) * 128]
        if rope_win:
            blk = rope_ax(blk)
        blk = blk * (WIN_HD ** -0.5)
        owin_ref[:, (2 * p) * 128:(2 * p + 1) * 128] = jnp.where(even, blk, 0.0).astype(BF16)
        owin_ref[:, (2 * p + 1) * 128:(2 * p + 2) * 128] = jnp.where(even, 0.0, blk).astype(BF16)
    r = mm(C_WK, 256)
    k2 = r[:, 0:128]
    if rope_win:
        k2 = rope_ax(k2)
    v2 = r[:, 128:256]
    for i, t in enumerate((k2, v2)):
        sw = pltpu.roll(t, 64, 1)
        owin_ref[:, 1024 + 256 * i:1024 + 256 * i + 128] = jnp.where(even, t, sw).astype(BF16)
        owin_ref[:, 1024 + 256 * i + 128:1024 + 256 * i + 256] = jnp.where(even, sw, t).astype(BF16)

    r = mm(C_NQ, 512) * (NA_HD ** -0.5)
    for p in range(NA_HEADS // 2):
        blk = r[:, p * 128:(p + 1) * 128]
        ona_ref[:, (2 * p) * 128:(2 * p + 1) * 128] = jnp.where(even, blk, 0.0).astype(BF16)
        ona_ref[:, (2 * p + 1) * 128:(2 * p + 2) * 128] = jnp.where(even, 0.0, blk).astype(BF16)
    ona_ref[:, 1024:1536] = mm(C_NK, 512).astype(BF16)
    ona_ref[:, 1536:2048] = mm(C_NV, 512).astype(BF16)

    for j in range(6):
        og_ref[:, j * 512:(j + 1) * 512] = mm(C_GATE + j * 512, 512).astype(BF16)


def _inproj(h, mod, gains, w_in, cr, sr, ca, sa, *, is_ctx, tm=256):
    t, d = h.shape
    nt = t // tm
    per_batch = cr.shape[0] // tm
    if is_ctx:
        mod_map = lambda i: (mod.shape[0] - 1, 0, 0)
    else:
        mod_map = lambda i: (i // per_batch, 0, 0)
    pos_map = lambda i: (i % per_batch, 0)
    full = lambda i: (0, 0)
    row = lambda i: (i, 0)
    outs = (jax.ShapeDtypeStruct((t, 2048), BF16), jax.ShapeDtypeStruct((t, 1536), BF16),
            jax.ShapeDtypeStruct((t, 2048), BF16), jax.ShapeDtypeStruct((t, 3072), BF16))
    return pl.pallas_call(
        functools.partial(_inproj_kernel, rope_win=not is_ctx),
        out_shape=outs,
        grid=(nt,),
        in_specs=[pl.BlockSpec((tm, d), row),
                  pl.BlockSpec((1, 6, d), mod_map),
                  pl.BlockSpec(gains.shape, full),
                  pl.BlockSpec(w_in.shape, full),
                  pl.BlockSpec((tm, LANES), pos_map), pl.BlockSpec((tm, LANES), pos_map),
                  pl.BlockSpec((tm, LANES), pos_map), pl.BlockSpec((tm, LANES), pos_map)],
        out_specs=(pl.BlockSpec((tm, 2048), row), pl.BlockSpec((tm, 1536), row),
                   pl.BlockSpec((tm, 2048), row), pl.BlockSpec((tm, 3072), row)),
        compiler_params=_cparams(("parallel",)),
        name="inproj_ctx" if is_ctx else "inproj_lat",
    )(h, mod, gains, w_in, cr, sr, ca, sa)


def _ret_kernel(lg_ref, ql, kl, vl, gl, qc, kc, vc, gc, yl_ref, yc_ref, acc_ref):
    C = RET_CHUNK
    h = pl.program_id(1)
    lgf = lg_ref[0, h]
    lgb = lg_ref[1, h]
    ii = lax.broadcasted_iota(jnp.int32, (C, C), 0).astype(F32)
    jj = lax.broadcasted_iota(jnp.int32, (C, C), 1).astype(F32)
    diff = ii - jj
    dmat = jnp.where(diff >= 0, jnp.exp(lgf * jnp.maximum(diff, 0.0)),
                     jnp.exp(lgb * jnp.maximum(-diff, 0.0)))
    r = lax.broadcasted_iota(jnp.int32, (C, RET_DK), 0).astype(F32)
    qdf = jnp.exp(lgf * (r + 1.0))
    kdf = jnp.exp(lgf * (C - 1.0 - r))
    qdb = jnp.exp(lgb * (C - r))
    kdb = jnp.exp(lgb * r)
    ones = jnp.ones((RET_DK, RET_DK), F32)
    cdf = jnp.exp(ones * (lgf * C))
    cdb = jnp.exp(ones * (lgb * C))

    n_lat = ql.shape[0] // C
    lc = qc.shape[0]
    chunks = [(qc, kc, vc, gc, yc_ref, 0, 0)]
    chunks += [(ql, kl, vl, gl, yl_ref, j * C, lc + j * C) for j in range(n_lat)]

    state = jnp.zeros((RET_DK, RET_DK), F32)
    for (qr, kr, vr, _, _, off, aoff) in chunks:
        q = qr[off:off + C, :]
        k = kr[off:off + C, :]
        v = vr[off:off + C, :]
        s = _dot_nt(q, k) * dmat
        y = _dot(s.astype(BF16), v)
        y = y + _dot((q.astype(F32) * qdf).astype(BF16), state.astype(BF16))
        state = state * cdf + _dot_tn((k.astype(F32) * kdf).astype(BF16), v)
        acc_ref[aoff:aoff + C, :] = y

    state = jnp.zeros((RET_DK, RET_DK), F32)
    for (qr, kr, vr, gr, yr, off, aoff) in [chunks[0]] + chunks[:0:-1]:
        q = qr[off:off + C, :]
        k = kr[off:off + C, :]
        v = vr[off:off + C, :]
        y = acc_ref[aoff:aoff + C, :] + _dot((q.astype(F32) * qdb).astype(BF16), state.astype(BF16))
        state = state * cdb + _dot_tn((k.astype(F32) * kdb).astype(BF16), v)
        mu = jnp.mean(y, axis=-1, keepdims=True)
        dlt = y - mu
        var = jnp.mean(dlt * dlt, axis=-1, keepdims=True)
        yn = dlt * lax.rsqrt(var + NORM_EPS)
        g = gr[off:off + C, :].astype(F32)
        yr[off:off + C, :] = (g * jax.nn.sigmoid(g) * yn).astype(BF16)


def _retention(lg, pl_ret, pc_ret, batch):
    tl = pl_ret.shape[0]
    tc = pc_ret.shape[0]
    s = tl // batch
    lc = tc // batch
    assert lc == RET_CHUNK and s % RET_CHUNK == 0
    lat = lambda off: pl.BlockSpec((s, 128), lambda b, h, off=off: (b, off + h))
    ctx = lambda off: pl.BlockSpec((lc, 128), lambda b, h, off=off: (b, off + h))
    return pl.pallas_call(
        _ret_kernel,
        out_shape=(jax.ShapeDtypeStruct((tl, 512), BF16), jax.ShapeDtypeStruct((tc, 512), BF16)),
        grid=(batch, RET_HEADS),
        in_specs=[pl.BlockSpec(memory_space=pltpu.SMEM),
                  lat(0), lat(4), lat(8), lat(12), ctx(0), ctx(4), ctx(8), ctx(12)],
        out_specs=(pl.BlockSpec((s, 128), lambda b, h: (b, h)),
                   pl.BlockSpec((lc, 128), lambda b, h: (b, h))),
        scratch_shapes=[pltpu.VMEM((s + lc, 128), F32)],
        compiler_params=_cparams(("parallel", "parallel")),
        name="retention",
    )(lg, pl_ret, pl_ret, pl_ret, pl_ret, pc_ret, pc_ret, pc_ret, pc_ret)


def _softmax_parts(parts, sink=None):
    m = functools.reduce(jnp.maximum, [jnp.max(p, axis=-1, keepdims=True) for p in parts])
    if sink is not None:
        m = jnp.maximum(m, sink)
    es = [jnp.exp(p - m) for p in parts]
    den = functools.reduce(lambda a, b: a + b, [jnp.sum(e, axis=-1, keepdims=True) for e in es])
    if sink is not None:
        den = den + jnp.exp(sink - m)
    inv = 1.0 / den
    return [(e * inv).astype(BF16) for e in es]


def _win_kernel(sink_ref, q_ref, k0, k1, v0, v1, ck0, ck1, cv0, cv1, o_ref):
    blk = WIN_BLOCK
    nw = 3 * blk
    s_len = k0.shape[0]
    j = pl.program_id(1)
    start = pl.multiple_of(jnp.clip((j - 1) * blk, 0, s_len - nw), blk)
    qpos = j * blk + lax.broadcasted_iota(jnp.int32, (blk, nw), 0)
    kpos = start + lax.broadcasted_iota(jnp.int32, (blk, nw), 1)
    bias = jnp.where(jnp.abs(qpos - kpos) <= WINDOW, 0.0, NEG_INF)
    even = lax.broadcasted_iota(jnp.int32, (blk, LANES), 1) < 64
    g = WIN_HEADS // WIN_KV_HEADS
    for kv, (kr, vr, ckr, cvr) in enumerate(((k0, v0, ck0, cv0), (k1, v1, ck1, cv1))):
        kcat = jnp.concatenate([kr[pl.ds(start, nw), :], ckr[...]], axis=0)
        vcat = jnp.concatenate([vr[pl.ds(start, nw), :], cvr[...]], axis=0)
        qst = jnp.concatenate([q_ref[:, (g * kv + i) * 128:(g * kv + i + 1) * 128] for i in range(g)], axis=0)
        s = _dot_nt(qst, kcat)
        ps = []
        for i in range(g):
            sb = s[i * blk:(i + 1) * blk]
            pw, pc = _softmax_parts([sb[:, :nw] + bias, sb[:, nw:]], sink_ref[g * kv + i])
            ps.append(jnp.concatenate([pw, pc], axis=1))
        pv = _dot(jnp.concatenate(ps, axis=0), vcat)
        for p in range(g // 2):
            o_ref[:, (2 * kv + p) * 128:(2 * kv + p + 1) * 128] = jnp.where(
                even, pv[(2 * p) * blk:(2 * p + 1) * blk], pv[(2 * p + 1) * blk:(2 * p + 2) * blk]).astype(BF16)


def _window(sink, pl_win, pc_win, batch):
    tl = pl_win.shape[0]
    tc = pc_win.shape[0]
    s = tl // batch
    lc = tc // batch
    nb = s // WIN_BLOCK
    lat = lambda cb: pl.BlockSpec((s, 128), lambda b, j, cb=cb: (b, cb))
    ctx = lambda cb: pl.BlockSpec((lc, 128), lambda b, j, cb=cb: (b, cb))
    return pl.pallas_call(
        _win_kernel,
        out_shape=jax.ShapeDtypeStruct((tl, 512), BF16),
        grid=(batch, nb),
        in_specs=[pl.BlockSpec(memory_space=pltpu.SMEM),
                  pl.BlockSpec((WIN_BLOCK, 1024), lambda b, j: (b * nb + j, 0)),
                  lat(8), lat(9), lat(10), lat(11), ctx(8), ctx(9), ctx(10), ctx(11)],
        out_specs=pl.BlockSpec((WIN_BLOCK, 512), lambda b, j: (b * nb + j, 0)),
        compiler_params=_cparams(("parallel", "arbitrary")),
        name="window_attn",
    )(sink, pl_win, pl_win, pl_win, pl_win, pl_win, pc_win, pc_win, pc_win, pc_win)


def _na_kernel(q_ref, ka, kb, kc, va, vb, vc, ck, cv, bias_ref, o_ref):
    nq = q_ref.shape[0]
    qst = jnp.concatenate([q_ref[:, 0:128], q_ref[:, 128:256]], axis=0)
    kcat = jnp.concatenate([ka[...], kb[...], kc[...], ck[...]], axis=0)
    vcat = jnp.concatenate([va[...], vb[...], vc[...], cv[...]], axis=0)
    n_nb = ka.shape[0] * 3
    s = _dot_nt(qst, kcat)
    ps = []
    for par in range(2):
        sb = s[par * nq:(par + 1) * nq]
        pn, pc = _softmax_parts([sb[:, :n_nb] + bias_ref[0, 0, par], sb[:, n_nb:]])
        ps.append(jnp.concatenate([pn, pc], axis=1))
    pv = _dot(jnp.concatenate(ps, axis=0), vcat)
    even = lax.broadcasted_iota(jnp.int32, (nq, LANES), 1) < 64
    o_ref[...] = jnp.where(even, pv[:nq], pv[nq:]).astype(BF16)


def _na_pattern(i):
    return jnp.where(i == 0, 0, jnp.where(i == 7, 2, 1))


def _neighbourhood(pl_na, pc_na, bias, batch):
    tl = pl_na.shape[0]
    tc = pc_na.shape[0]
    s = tl // batch
    lc = tc // batch
    nq = NA_QROWS * GRID_W
    ng = s // nq
    kblk = nq
    nkb = s // kblk
    assert ng == 8 and NA_KROWS * GRID_W == 3 * kblk
    hp = NA_HEADS // 2

    def kspec(cb0, d):
        return pl.BlockSpec(
            (kblk, 128),
            lambda p, i, b, d=d, cb0=cb0: (b * nkb + jnp.clip(i - 1, 0, nkb - 3) + d, cb0 + p))

    return pl.pallas_call(
        _na_kernel,
        out_shape=jax.ShapeDtypeStruct((tl, 512), BF16),
        grid=(hp, ng, batch),
        in_specs=[pl.BlockSpec((nq, 256), lambda p, i, b: (b * ng + i, p)),
                  kspec(8, 0), kspec(8, 1), kspec(8, 2),
                  kspec(12, 0), kspec(12, 1), kspec(12, 2),
                  pl.BlockSpec((lc, 128), lambda p, i, b: (b, 8 + p)),
                  pl.BlockSpec((lc, 128), lambda p, i, b: (b, 12 + p)),
                  pl.BlockSpec((1, 1, 2, nq, 3 * kblk), lambda p, i, b: (p, _na_pattern(i), 0, 0, 0))],
        out_specs=pl.BlockSpec((nq, 128), lambda p, i, b: (b * ng + i, p)),
        compiler_params=_cparams(("parallel", "parallel", "arbitrary")),
        name="neighbourhood_attn",
    )(pl_na, pl_na, pl_na, pl_na, pl_na, pl_na, pl_na, pc_na, pc_na, bias)


def _na_bias_table(rpb, s):
    rows = s // GRID_W
    kr = min(NA_ROWS_MAX, rows)
    tabs = []
    for i in (0, 1, rows // NA_QROWS - 1):
        r = NA_QROWS * i + np.arange(NA_QROWS)
        k0 = NA_QROWS * int(np.clip(i - 1, 0, rows // NA_QROWS - 3))
        krow = k0 + np.arange(NA_KROWS)
        r_start = np.clip(r - kr // 2, 0, rows - kr)
        row_ok = (krow[None, :] >= r_start[:, None]) & (krow[None, :] < r_start[:, None] + kr)
        dr = np.clip(krow[None, :] - r[:, None] + (NA_ROWS_MAX - 1), 0, 2 * NA_ROWS_MAX - 2)
        cq = np.arange(GRID_W)
        ck = np.arange(GRID_W)
        c_start = np.clip(cq - NA_COLS // 2, 0, GRID_W - NA_COLS)
        col_ok = (ck[None, :] >= c_start[:, None]) & (ck[None, :] < c_start[:, None] + NA_COLS)
        dc = np.clip(ck[None, :] - cq[:, None] + (NA_COLS - 1), 0, 2 * NA_COLS - 2)
        ok = row_ok[:, None, :, None] & col_ok[None, :, None, :]
        sel_r = (dr[:, :, None] == np.arange(2 * NA_ROWS_MAX - 1)).astype(np.float32)
        sel_c = (dc[:, :, None] == np.arange(2 * NA_COLS - 1)).astype(np.float32)
        b = jnp.einsum("qkr,hrc,xyc->hqxky", sel_r, rpb.astype(F32), sel_c,
                       precision=lax.Precision.HIGHEST)
        b = jnp.where(ok[None], b, NEG_INF)
        tabs.append(b.reshape(NA_HEADS, NA_QROWS * GRID_W, NA_KROWS * GRID_W))
    t = jnp.stack(tabs, axis=1)
    t = t.reshape(NA_HEADS // 2, 2, 3, t.shape[2], t.shape[3])
    return jnp.swapaxes(t, 1, 2)


def _ctx_attn_kernel(sink_ref, w_ref, n_ref, ow_ref, on_ref):
    lc = w_ref.shape[0]
    even = lax.broadcasted_iota(jnp.int32, (lc, LANES), 1) < 64
    g = WIN_HEADS // WIN_KV_HEADS
    for kv in range(WIN_KV_HEADS):
        k = w_ref[:, 1024 + kv * 128:1024 + (kv + 1) * 128]
        v = w_ref[:, 1280 + kv * 128:1280 + (kv + 1) * 128]
        qst = jnp.concatenate([w_ref[:, (g * kv + i) * 128:(g * kv + i + 1) * 128] for i in range(g)], axis=0)
        s = _dot_nt(qst, k)
        ps = [_softmax_parts([s[i * lc:(i + 1) * lc]], sink_ref[g * kv + i])[0] for i in range(g)]
        pv = _dot(jnp.concatenate(ps, axis=0), v)
        for p in range(g // 2):
            ow_ref[:, (2 * kv + p) * 128:(2 * kv + p + 1) * 128] = jnp.where(
                even, pv[(2 * p) * lc:(2 * p + 1) * lc], pv[(2 * p + 1) * lc:(2 * p + 2) * lc]).astype(BF16)
    for p in range(NA_HEADS // 2):
        k = n_ref[:, 1024 + p * 128:1024 + (p + 1) * 128]
        v = n_ref[:, 1536 + p * 128:1536 + (p + 1) * 128]
        qst = jnp.concatenate([n_ref[:, (2 * p) * 128:(2 * p + 1) * 128],
                               n_ref[:, (2 * p + 1) * 128:(2 * p + 2) * 128]], axis=0)
        s = _dot_nt(qst, k)
        ps = [_softmax_parts([s[i * lc:(i + 1) * lc]])[0] for i in range(2)]
        pv = _dot(jnp.concatenate(ps, axis=0), v)
        on_ref[:, p * 128:(p + 1) * 128] = jnp.where(even, pv[:lc], pv[lc:]).astype(BF16)


def _ctx_attention(sink, pc_win, pc_na, batch):
    tc = pc_win.shape[0]
    lc = tc // batch
    row = lambda b: (b, 0)
    return pl.pallas_call(
        _ctx_attn_kernel,
        out_shape=(jax.ShapeDtypeStruct((tc, 512), BF16), jax.ShapeDtypeStruct((tc, 512), BF16)),
        grid=(batch,),
        in_specs=[pl.BlockSpec(memory_space=pltpu.SMEM),
                  pl.BlockSpec((lc, pc_win.shape[1]), row), pl.BlockSpec((lc, pc_na.shape[1]), row)],
        out_specs=(pl.BlockSpec((lc, 512), row), pl.BlockSpec((lc, 512), row)),
        compiler_params=_cparams(("parallel",)),
        name="context_attn",
    )(sink, pc_win, pc_na)


def _merge_kernel(h_ref, yr_ref, yw_ref, yn_ref, gt_ref, mod_ref, g_ref, wb_ref, wo_ref, wr_ref,
                  ho_ref, f_ref, idx_ref, gw_ref, *, with_router):
    z = None
    for r, y_ref in enumerate((yr_ref, yw_ref, yn_ref)):
        gate = jax.nn.sigmoid(gt_ref[:, r * D_MODEL:(r + 1) * D_MODEL].astype(F32))
        t = gate * _dot(y_ref[...], wb_ref[r])
        z = t if z is None else z + t
    o = _dot(z.astype(BF16), wo_ref[...])
    h = h_ref[...] + mod_ref[0, 2:3, :] * _rms(o, g_ref[1:2, :])
    ho_ref[...] = h
    f = _rms(h, g_ref[2:3, :]) * (1.0 + mod_ref[0, 4:5, :]) + mod_ref[0, 3:4, :]
    f_ref[...] = f.astype(f_ref.dtype)
    if not with_router:
        idx_ref[...] = jnp.zeros_like(idx_ref)
        gw_ref[...] = jnp.zeros_like(gw_ref)
        return
    f_hi = f.astype(BF16)
    f_lo = (f - f_hi.astype(F32)).astype(BF16)
    wr = wr_ref[...]
    w_hi = wr.astype(BF16)
    w_lo = (wr - w_hi.astype(F32)).astype(BF16)
    logits = _dot(f_hi, w_hi) + (_dot(f_lo, w_hi) + _dot(f_hi, w_lo))
    idx = lax.broadcasted_iota(jnp.int32, logits.shape, 1)
    m1 = jnp.max(logits, axis=-1, keepdims=True)
    i1 = jnp.min(jnp.where(logits == m1, idx, N_EXPERTS), axis=-1, keepdims=True)
    rest = jnp.where(idx == i1, -jnp.inf, logits)
    m2 = jnp.max(rest, axis=-1, keepdims=True)
    i2 = jnp.min(jnp.where(rest == m2, idx, N_EXPERTS), axis=-1, keepdims=True)
    e2 = jnp.exp(m2 - m1)
    inv = 1.0 / (1.0 + e2)
    first = lax.broadcasted_iota(jnp.int32, idx_ref.shape, 1) == 0
    idx_ref[...] = jnp.where(first, i1, i2)
    gw_ref[...] = jnp.where(first, inv, e2 * inv)


def _merge(h, y_ret, y_win, y_na, gates, mod, gains, w_branch, w_out, w_router, *, with_router, is_ctx,
           per_batch, tm=256):
    t, d = h.shape
    f_dtype = F32 if with_router else BF16
    if is_ctx:
        mod_map = lambda i: (mod.shape[0] - 1, 0, 0)
    else:
        mod_map = lambda i: (i // (per_batch // tm), 0, 0)
    row = lambda i: (i, 0)
    full2 = lambda i: (0, 0)
    return pl.pallas_call(
        functools.partial(_merge_kernel, with_router=with_router),
        out_shape=(jax.ShapeDtypeStruct((t, d), F32), jax.ShapeDtypeStruct((t, d), f_dtype),
                   jax.ShapeDtypeStruct((t, TOP_K), jnp.int32), jax.ShapeDtypeStruct((t, TOP_K), F32)),
        grid=(t // tm,),
        in_specs=[pl.BlockSpec((tm, d), row),
                  pl.BlockSpec((tm, BRANCH_W), row), pl.BlockSpec((tm, BRANCH_W), row),
                  pl.BlockSpec((tm, BRANCH_W), row),
                  pl.BlockSpec((tm, N_BRANCH * d), row),
                  pl.BlockSpec((1, 6, d), mod_map),
                  pl.BlockSpec(gains.shape, full2),
                  pl.BlockSpec(w_branch.shape, lambda i: (0, 0, 0)),
                  pl.BlockSpec(w_out.shape, full2),
                  pl.BlockSpec(w_router.shape, full2)],
        out_specs=(pl.BlockSpec((tm, d), row), pl.BlockSpec((tm, d), row),
                   pl.BlockSpec((tm, TOP_K), row), pl.BlockSpec((tm, TOP_K), row)),
        compiler_params=_cparams(("parallel",)),
        name="merge_ctx" if is_ctx else "merge_lat",
    )(h, y_ret, y_win, y_na, gates, mod, gains, w_branch, w_out, w_router)


def _swiglu(x, w1, w3, w2):
    a1 = _dot(x, w1)
    a3 = _dot(x, w3)
    return _dot((a1 * jax.nn.sigmoid(a1) * a3).astype(BF16), w2)


def _ffn_kernel(h_ref, f_ref, mod_ref, g_ref, w1_ref, w3_ref, w2_ref, o_ref, acc_ref):
    c = pl.program_id(1)

    @pl.when(c == 0)
    def _():
        acc_ref[...] = jnp.zeros_like(acc_ref)

    acc_ref[...] += _swiglu(f_ref[...], w1_ref[...], w3_ref[...], w2_ref[...])

    @pl.when(c == pl.num_programs(1) - 1)
    def _():
        o_ref[...] = h_ref[...] + mod_ref[0, 5:6, :] * _rms(acc_ref[...], g_ref[3:4, :])


def _ffn(h, f, mod, gains, w1, w3, w2, *, is_ctx, per_batch, tm, fc):
    t, d = h.shape
    dff = w1.shape[1]
    if is_ctx:
        mod_map = lambda i, c: (mod.shape[0] - 1, 0, 0)
    else:
        mod_map = lambda i, c: (i // (per_batch // tm), 0, 0)
    row = lambda i, c: (i, 0)
    return pl.pallas_call(
        _ffn_kernel,
        out_shape=jax.ShapeDtypeStruct((t, d), F32),
        grid=(t // tm, dff // fc),
        in_specs=[pl.BlockSpec((tm, d), row), pl.BlockSpec((tm, d), row),
                  pl.BlockSpec((1, 6, d), mod_map),
                  pl.BlockSpec(gains.shape, lambda i, c: (0, 0)),
                  pl.BlockSpec((d, fc), lambda i, c: (0, c)),
                  pl.BlockSpec((d, fc), lambda i, c: (0, c)),
                  pl.BlockSpec((fc, d), lambda i, c: (c, 0))],
        out_specs=pl.BlockSpec((tm, d), row),
        scratch_shapes=[pltpu.VMEM((tm, d), F32)],
        compiler_params=_cparams(("parallel", "arbitrary")),
        name="ffn_ctx" if is_ctx else "ffn_lat",
    )(h, f, mod, gains, w1, w3, w2)


def _route_slots(idx, tile):
    n = idx.shape[0] * TOP_K
    e = idx.reshape(n)
    onehot = (e[:, None] == jnp.arange(N_EXPERTS, dtype=jnp.int32)).astype(jnp.int32)
    csum = jnp.cumsum(onehot, axis=0)
    counts = csum[-1]
    rank = jnp.sum(csum * onehot, axis=1) - 1
    padded = (counts + tile - 1) // tile * tile
    ends = jnp.cumsum(padded)
    starts = ends - padded
    pos = jnp.sum(onehot * starts[None, :], axis=1) + rank
    n_tiles = n // tile + N_EXPERTS
    tile_start = jnp.arange(n_tiles, dtype=jnp.int32) * tile
    tile_expert = jnp.minimum(jnp.sum((tile_start[:, None] >= ends[None, :]).astype(jnp.int32), axis=1),
                              N_EXPERTS - 1)
    n_active = (ends[-1] // tile).reshape(1)
    return pos.astype(jnp.int32), tile_expert.astype(jnp.int32), n_active.astype(jnp.int32)


def _dispatch_kernel(pos_ref, f_ref, xs_in_ref, xs_ref, sem):
    del xs_in_ref
    tm = f_ref.shape[0]

    def row_copy(r, k):
        p = pos_ref[0, 0, TOP_K * r + k]
        return pltpu.make_async_copy(f_ref.at[pl.ds(r, 1), :], xs_ref.at[pl.ds(p, 1), :], sem)

    def start(r, carry):
        for k in range(TOP_K):
            row_copy(r, k).start()
        return carry

    def wait(r, carry):
        for k in range(TOP_K):
            row_copy(r, k).wait()
        return carry

    lax.fori_loop(0, tm, start, 0)
    lax.fori_loop(0, tm, wait, 0)


def _dispatch(f, pos, xs, tm=256):
    t, d = f.shape
    pos3 = pos.reshape(t // tm, 1, TOP_K * tm)
    return pl.pallas_call(
        _dispatch_kernel,
        out_shape=jax.ShapeDtypeStruct(xs.shape, xs.dtype),
        grid=(t // tm,),
        in_specs=[pl.BlockSpec((1, 1, TOP_K * tm), lambda i: (i, 0, 0), memory_space=pltpu.SMEM),
                  pl.BlockSpec((tm, d), lambda i: (i, 0)),
                  pl.BlockSpec(memory_space=pl.ANY)],
        out_specs=pl.BlockSpec(memory_space=pl.ANY),
        scratch_shapes=[pltpu.SemaphoreType.DMA(())],
        input_output_aliases={2: 0},
        compiler_params=_cparams(("arbitrary",)),
        name="moe_dispatch",
    )(pos3, f, xs)


def _experts_kernel(te_ref, na_ref, x_ref, *refs, has_prev):
    del te_ref
    if has_prev:
        yprev_ref, w1_ref, w3_ref, w2_ref, y_ref = refs
    else:
        w1_ref, w3_ref, w2_ref, y_ref = refs
    active = pl.program_id(0) < na_ref[0]

    @pl.when(active)
    def _():
        y = _swiglu(x_ref[...].astype(BF16), w1_ref[0], w3_ref[0], w2_ref[0])
        y_ref[...] = yprev_ref[...] + y if has_prev else y

    @pl.when(jnp.logical_not(active))
    def _():
        y_ref[...] = jnp.zeros_like(y_ref)


def _experts_pass(xs, y_prev, tile_expert, n_active, w1, w3, w2, *, tile, fc, c):
    p, d = xs.shape
    slot = pl.BlockSpec((tile, d), lambda j, te, na: (j, 0))
    in_specs = [slot] + ([slot] if y_prev is not None else []) + [
        pl.BlockSpec((1, d, fc), lambda j, te, na: (te[j], 0, c)),
        pl.BlockSpec((1, d, fc), lambda j, te, na: (te[j], 0, c)),
        pl.BlockSpec((1, fc, d), lambda j, te, na: (te[j], c, 0))]
    grid_spec = pltpu.PrefetchScalarGridSpec(
        num_scalar_prefetch=2, grid=(p // tile,), in_specs=in_specs, out_specs=slot)
    args = (xs,) + ((y_prev,) if y_prev is not None else ()) + (w1, w3, w2)
    return pl.pallas_call(
        functools.partial(_experts_kernel, has_prev=y_prev is not None),
        out_shape=jax.ShapeDtypeStruct((p, d), F32),
        grid_spec=grid_spec,
        compiler_params=_cparams(("arbitrary",)),
        name="moe_experts",
    )(tile_expert, n_active, *args)


def _experts(xs, tile_expert, n_active, w1, w3, w2, *, tile, fc):
    y = None
    for c in range(w1.shape[2] // fc):
        y = _experts_pass(xs, y, tile_expert, n_active, w1, w3, w2, tile=tile, fc=fc, c=c)
    return y


def _combine_kernel(pos_ref, gw_ref, h_ref, mod_ref, g_ref, y_any, o_ref, buf_ref, sem):
    tm = h_ref.shape[0]

    def row_copy(r, k):
        p = pos_ref[0, 0, TOP_K * r + k]
        return pltpu.make_async_copy(y_any.at[pl.ds(p, 1), :], buf_ref.at[k, pl.ds(r, 1), :], sem)

    def start(r, carry):
        for k in range(TOP_K):
            row_copy(r, k).start()
        return carry

    def wait(r, carry):
        for k in range(TOP_K):
            row_copy(r, k).wait()
        return carry

    lax.fori_loop(0, tm, start, 0)
    lax.fori_loop(0, tm, wait, 0)
    gw = gw_ref[...]
    y = gw[:, 0:1] * buf_ref[0] + gw[:, 1:2] * buf_ref[1]
    o_ref[...] = h_ref[...] + mod_ref[0, 5:6, :] * _rms(y, g_ref[3:4, :])


def _combine(h, pos, gw, y_sorted, mod, gains, *, is_ctx, per_batch, tm=256):
    t, d = h.shape
    pos3 = pos.reshape(t // tm, 1, TOP_K * tm)
    if is_ctx:
        mod_map = lambda i: (mod.shape[0] - 1, 0, 0)
    else:
        mod_map = lambda i: (i // (per_batch // tm), 0, 0)
    row = lambda i: (i, 0)
    return pl.pallas_call(
        _combine_kernel,
        out_shape=jax.ShapeDtypeStruct((t, d), F32),
        grid=(t // tm,),
        in_specs=[pl.BlockSpec((1, 1, TOP_K * tm), lambda i: (i, 0, 0), memory_space=pltpu.SMEM),
                  pl.BlockSpec((tm, TOP_K), row),
                  pl.BlockSpec((tm, d), row),
                  pl.BlockSpec((1, 6, d), mod_map),
                  pl.BlockSpec(gains.shape, lambda i: (0, 0)),
                  pl.BlockSpec(memory_space=pl.ANY)],
        out_specs=pl.BlockSpec((tm, d), row),
        scratch_shapes=[pltpu.VMEM((TOP_K, tm, d), F32), pltpu.SemaphoreType.DMA(())],
        compiler_params=_cparams(("arbitrary",)),
        name="moe_combine_ctx" if is_ctx else "moe_combine_lat",
    )(pos3, gw, h, mod, gains, y_sorted)


def _rope_tables(s, lc):
    def lin(pos):
        n_freq = RET_DK // 2
        inv = ROPE_BASE ** (-jnp.arange(n_freq, dtype=F32) / n_freq)
        ang = pos.astype(F32)[:, None] * inv
        cos, sin = jnp.cos(ang), jnp.sin(ang)
        return jnp.concatenate([cos, cos], -1), jnp.concatenate([-sin, sin], -1)

    t = jnp.arange(s)
    n_freq = WIN_HD // 4
    inv = ROPE_BASE ** (-jnp.arange(n_freq, dtype=F32) / n_freq)
    rowa = (t // GRID_W).astype(F32)[:, None] * inv
    cola = (t % GRID_W).astype(F32)[:, None] * inv
    ang = jnp.concatenate([rowa, cola], axis=-1)
    cos, sin = jnp.cos(ang), jnp.sin(ang)
    ca = jnp.tile(jnp.concatenate([cos, cos], -1), (1, 2))
    sa = jnp.tile(jnp.concatenate([-sin, sin], -1), (1, 2))
    cr_l, sr_l = lin(lc + jnp.arange(s))
    cr_c, sr_c = lin(jnp.arange(lc))
    return (cr_l, sr_l, ca, sa), (cr_c, sr_c, ca[:lc], sa[:lc])


def kernel(x, c, ctx, c_ctx, w_mod, b_mod, norm_gains, w_in, ret_decay, win_sink, na_rpb, w_branch, w_out,
           ffn_w1, ffn_w3, ffn_w2, moe_router, moe_w1, moe_w3, moe_w2):
    batch, s, d = x.shape
    lc = ctx.shape[1]
    depth = w_mod.shape[0]

    c_all = jnp.zeros((16, d), F32).at[:batch].set(c).at[batch].set(c_ctx)
    mods = _modulation(c_all, w_mod, b_mod)[:, :batch + 1].reshape(depth, batch + 1, 6, d)

    tabs_l, tabs_c = _rope_tables(s, lc)
    log_gamma = jnp.log1p(-jnp.exp(ret_decay.astype(F32)))

    h_l = x.reshape(batch * s, d)
    h_c = ctx.reshape(batch * lc, d)
    zero_router = jnp.zeros((d, N_EXPERTS), F32)

    for layer in range(depth):
        need_ctx = layer < depth - 1
        mod = mods[layer]
        gains = norm_gains[layer]
        w_in_b = w_in[layer].astype(BF16)
        wb_b = w_branch[layer].astype(BF16)
        wo_b = w_out[layer].astype(BF16)
        i = layer // 2
        is_moe = layer % 2 == 1
        if is_moe:
            w1, w3, w2 = moe_w1[i].astype(BF16), moe_w3[i].astype(BF16), moe_w2[i].astype(BF16)
            w_router = moe_router[i]
        else:
            w1, w3, w2 = ffn_w1[i].astype(BF16), ffn_w3[i].astype(BF16), ffn_w2[i].astype(BF16)
            w_router = zero_router

        p_ret_l, p_win_l, p_na_l, p_gt_l = _inproj(h_l, mod, gains, w_in_b, *tabs_l, is_ctx=False)
        p_ret_c, p_win_c, p_na_c, p_gt_c = _inproj(h_c, mod, gains, w_in_b, *tabs_c, is_ctx=True)

        y_ret_l, y_ret_c = _retention(log_gamma[layer], p_ret_l, p_ret_c, batch)
        y_win_l = _window(win_sink[layer], p_win_l, p_win_c, batch)
        y_na_l = _neighbourhood(p_na_l, p_na_c, _na_bias_table(na_rpb[layer], s), batch)

        h_l, f_l, idx_l, gw_l = _merge(h_l, y_ret_l, y_win_l, y_na_l, p_gt_l, mod, gains, wb_b, wo_b,
                                       w_router, with_router=is_moe, is_ctx=False, per_batch=s)
        if need_ctx:
            y_win_c, y_na_c = _ctx_attention(win_sink[layer], p_win_c, p_na_c, batch)
            h_c, f_c, idx_c, gw_c = _merge(h_c, y_ret_c, y_win_c, y_na_c, p_gt_c, mod, gains, wb_b, wo_b,
                                           w_router, with_router=is_moe, is_ctx=True, per_batch=lc)

        if not is_moe:
            h_l = _ffn(h_l, f_l, mod, gains, w1, w3, w2, is_ctx=False, per_batch=s, tm=512, fc=1408)
            if need_ctx:
                h_c = _ffn(h_c, f_c, mod, gains, w1, w3, w2, is_ctx=True, per_batch=lc, tm=512, fc=1408)
            continue

        idx_all = jnp.concatenate([idx_l, idx_c], axis=0) if need_ctx else idx_l
        pos, tile_expert, n_active = _route_slots(idx_all, MOE_TILE)
        n_slots = tile_expert.shape[0] * MOE_TILE
        xs = jnp.zeros((n_slots, d), F32)
        n_l = TOP_K * h_l.shape[0]
        xs = _dispatch(f_l, pos[:n_l], xs)
        if need_ctx:
            xs = _dispatch(f_c, pos[n_l:], xs)
        y_sorted = _experts(xs, tile_expert, n_active, w1, w3, w2, tile=MOE_TILE, fc=MOE_FC)
        h_l = _combine(h_l, pos[:n_l], gw_l, y_sorted, mod, gains, is_ctx=False, per_batch=s)
        if need_ctx:
            h_c = _combine(h_c, pos[n_l:], gw_c, y_sorted, mod, gains, is_ctx=True, per_batch=lc)
    return h_l.reshape(batch, s, d)
```

```python
import functools
import math

import jax
import jax.numpy as jnp
import numpy as np
from jax import lax
from jax.experimental import pallas as pl
from jax.experimental.pallas import tpu as pltpu

F32 = jnp.float32
BF16 = jnp.bfloat16

D_MODEL = 1024
GRID_W = 64
RET_HEADS = 4
RET_DK = 128
WIN_HEADS = 8
WIN_KV_HEADS = 2
WIN_HD = 64
WINDOW = 128
WIN_BLOCK = 128
NA_HEADS = 8
NA_HD = 64
NA_ROWS_MAX = 8
NA_COLS = 16
BRANCH_W = 512
N_BRANCH = 3
N_EXPERTS = 8
TOP_K = 2
ROPE_BASE = 10000.0
NORM_EPS = 1e-6
NEG_INF = -1e30
LOG2E = math.log2(math.e)

LANES = 128
VMEM_LIMIT = 56 * 1024 * 1024

C_RQ, C_RK, C_RV, C_RG = 0, 512, 1024, 1536
C_WQ, C_WK, C_WV = 2048, 2560, 2688
C_NQ, C_NK, C_NV = 2816, 3328, 3840
C_GATE = 4352
D_IN = 7424

RET_CHUNK = 256
NA_QROWS = 4
NA_KROWS = 12
MOE_TILE = 512
MOE_FC = 1792


def _cparams(sem):
    return pltpu.CompilerParams(dimension_semantics=sem, vmem_limit_bytes=VMEM_LIMIT)


def _rms(x, g):
    return x * lax.rsqrt(jnp.mean(x * x, axis=-1, keepdims=True) + NORM_EPS) * g


def _dot(a, b):
    return jnp.dot(a, b, preferred_element_type=F32)


def _dot_nt(a, b):
    return lax.dot_general(a, b, (((1,), (1,)), ((), ())), preferred_element_type=F32)


def _dot_tn(a, b):
    return lax.dot_general(a, b, (((0,), (0,)), ((), ())), preferred_element_type=F32)


def _mod_kernel(c_ref, w_ref, b_ref, o_ref):
    c = c_ref[...]
    s = c * jax.nn.sigmoid(c)
    o_ref[0] = jnp.dot(s, w_ref[0], preferred_element_type=F32,
                       precision=lax.Precision.HIGHEST) + b_ref[0]


def _modulation(c_all, w_mod, b_mod):
    depth, d, n = w_mod.shape
    tn = 1536
    return pl.pallas_call(
        _mod_kernel,
        out_shape=jax.ShapeDtypeStruct((depth, c_all.shape[0], n), F32),
        grid=(depth, n // tn),
        in_specs=[pl.BlockSpec(c_all.shape, lambda l, j: (0, 0)),
                  pl.BlockSpec((1, d, tn), lambda l, j: (l, 0, j)),
                  pl.BlockSpec((1, 1, tn), lambda l, j: (l, 0, j))],
        out_specs=pl.BlockSpec((1, c_all.shape[0], tn), lambda l, j: (l, 0, j)),
        compiler_params=_cparams(("arbitrary", "arbitrary")),
        name="modulation",
    )(c_all, w_mod, b_mod.reshape(depth, 1, n))


def _inproj_kernel(h_ref, mod_ref, g_ref, w_ref, cr_ref, sr_ref, ca_ref, sa_ref,
                   oret_ref, owin_ref, ona_ref, og_ref, *, rope_win):
    x = h_ref[...]
    a = _rms(x, g_ref[0:1, :]) * (1.0 + mod_ref[0, 1:2, :]) + mod_ref[0, 0:1, :]
    a = a.astype(BF16)
    tm = x.shape[0]
    lane = lax.broadcasted_iota(jnp.int32, (tm, LANES), 1)
    even = lane < 64
    first_half = (lane % 64) < 32

    def mm(c0, cw):
        return _dot(a, w_ref[:, c0:c0 + cw])

    def rope_lin(v):
        return v * cr_ref[...] + pltpu.roll(v, 64, 1) * sr_ref[...]

    def rope_ax(v):
        rot = jnp.where(first_half, pltpu.roll(v, 96, 1), pltpu.roll(v, 32, 1))
        return v * ca_ref[...] + rot * sa_ref[...]

    r = mm(C_RQ, 512)
    for h in range(RET_HEADS):
        blk = rope_lin(r[:, h * 128:(h + 1) * 128]) * (RET_DK ** -0.5)
        oret_ref[:, h * 128:(h + 1) * 128] = blk.astype(BF16)
    r = mm(C_RK, 512)
    for h in range(RET_HEADS):
        blk = rope_lin(r[:, h * 128:(h + 1) * 128])
        oret_ref[:, 512 + h * 128:512 + (h + 1) * 128] = blk.astype(BF16)
    oret_ref[:, 1024:1536] = mm(C_RV, 512).astype(BF16)
    oret_ref[:, 1536:2048] = mm(C_RG, 512).astype(BF16)

    r = mm(C_WQ, 512)
    for p in range(WIN_HEADS // 2):
        blk = r[:, p * 128:(p + 1) * 128]
        if rope_win:
            blk = rope_ax(blk)
        blk = blk * (WIN_HD ** -0.5 * LOG2E)
        owin_ref[:, (2 * p) * 128:(2 * p + 1) * 128] = jnp.where(even, blk, 0.0).astype(BF16)
        owin_ref[:, (2 * p + 1) * 128:(2 * p + 2) * 128] = jnp.where(even, 0.0, blk).astype(BF16)
    r = mm(C_WK, 256)
    k2 = r[:, 0:128]
    if rope_win:
        k2 = rope_ax(k2)
    v2 = r[:, 128:256]
    for i, t in enumerate((k2, v2)):
        sw = pltpu.roll(t, 64, 1)
        owin_ref[:, 1024 + 256 * i:1024 + 256 * i + 128] = jnp.where(even, t, sw).astype(BF16)
        owin_ref[:, 1024 + 256 * i + 128:1024 + 256 * i + 256] = jnp.where(even, sw, t).astype(BF16)

    r = mm(C_NQ, 512) * (NA_HD ** -0.5 * LOG2E)
    for p in range(NA_HEADS // 2):
        blk = r[:, p * 128:(p + 1) * 128]
        ona_ref[:, (2 * p) * 128:(2 * p + 1) * 128] = jnp.where(even, blk, 0.0).astype(BF16)
        ona_ref[:, (2 * p + 1) * 128:(2 * p + 2) * 128] = jnp.where(even, 0.0, blk).astype(BF16)
    ona_ref[:, 1024:1536] = mm(C_NK, 512).astype(BF16)
    ona_ref[:, 1536:2048] = mm(C_NV, 512).astype(BF16)

    for j in range(6):
        og_ref[:, j * 512:(j + 1) * 512] = mm(C_GATE + j * 512, 512).astype(BF16)


def _inproj(h, mod, gains, w_in, cr, sr, ca, sa, *, is_ctx, tm=256):
    t, d = h.shape
    nt = t // tm
    per_batch = cr.shape[0] // tm
    if is_ctx:
        mod_map = lambda i: (mod.shape[0] - 1, 0, 0)
    else:
        mod_map = lambda i: (i // per_batch, 0, 0)
    pos_map = lambda i: (i % per_batch, 0)
    full = lambda i: (0, 0)
    row = lambda i: (i, 0)
    outs = (jax.ShapeDtypeStruct((t, 2048), BF16), jax.ShapeDtypeStruct((t, 1536), BF16),
            jax.ShapeDtypeStruct((t, 2048), BF16), jax.ShapeDtypeStruct((t, 3072), BF16))
    return pl.pallas_call(
        functools.partial(_inproj_kernel, rope_win=not is_ctx),
        out_shape=outs,
        grid=(nt,),
        in_specs=[pl.BlockSpec((tm, d), row),
                  pl.BlockSpec((1, 6, d), mod_map),
                  pl.BlockSpec(gains.shape, full),
                  pl.BlockSpec(w_in.shape, full),
                  pl.BlockSpec((tm, LANES), pos_map), pl.BlockSpec((tm, LANES), pos_map),
                  pl.BlockSpec((tm, LANES), pos_map), pl.BlockSpec((tm, LANES), pos_map)],
        out_specs=(pl.BlockSpec((tm, 2048), row), pl.BlockSpec((tm, 1536), row),
                   pl.BlockSpec((tm, 2048), row), pl.BlockSpec((tm, 3072), row)),
        compiler_params=_cparams(("parallel",)),
        name="inproj_ctx" if is_ctx else "inproj_lat",
    )(h, mod, gains, w_in, cr, sr, ca, sa)


def _ret_kernel(lg_ref, ql, kl, vl, gl, qc, kc, vc, gc, yl_ref, yc_ref, acc_ref):
    C = RET_CHUNK
    h = pl.program_id(1)
    lgf = lg_ref[0, h]
    lgb = lg_ref[1, h]
    ii = lax.broadcasted_iota(jnp.int32, (C, C), 0).astype(F32)
    jj = lax.broadcasted_iota(jnp.int32, (C, C), 1).astype(F32)
    diff = ii - jj
    dmat = jnp.where(diff >= 0, jnp.exp(lgf * jnp.maximum(diff, 0.0)),
                     jnp.exp(lgb * jnp.maximum(-diff, 0.0)))
    r = lax.broadcasted_iota(jnp.int32, (C, RET_DK), 0).astype(F32)
    qdf = jnp.exp(lgf * (r + 1.0))
    kdf = jnp.exp(lgf * (C - 1.0 - r))
    qdb = jnp.exp(lgb * (C - r))
    kdb = jnp.exp(lgb * r)
    ones = jnp.ones((RET_DK, RET_DK), F32)
    cdf = jnp.exp(ones * (lgf * C))
    cdb = jnp.exp(ones * (lgb * C))

    n_lat = ql.shape[0] // C
    lc = qc.shape[0]
    chunks = [(qc, kc, vc, gc, yc_ref, 0, 0)]
    chunks += [(ql, kl, vl, gl, yl_ref, j * C, lc + j * C) for j in range(n_lat)]

    state = jnp.zeros((RET_DK, RET_DK), F32)
    for (qr, kr, vr, _, _, off, aoff) in chunks:
        q = qr[off:off + C, :]
        k = kr[off:off + C, :]
        v = vr[off:off + C, :]
        s = _dot_nt(q, k) * dmat
        y = _dot(s.astype(BF16), v)
        y = y + _dot((q.astype(F32) * qdf).astype(BF16), state.astype(BF16))
        state = state * cdf + _dot_tn((k.astype(F32) * kdf).astype(BF16), v)
        acc_ref[aoff:aoff + C, :] = y

    state = jnp.zeros((RET_DK, RET_DK), F32)
    for (qr, kr, vr, gr, yr, off, aoff) in [chunks[0]] + chunks[:0:-1]:
        q = qr[off:off + C, :]
        k = kr[off:off + C, :]
        v = vr[off:off + C, :]
        y = acc_ref[aoff:aoff + C, :] + _dot((q.astype(F32) * qdb).astype(BF16), state.astype(BF16))
        state = state * cdb + _dot_tn((k.astype(F32) * kdb).astype(BF16), v)
        mu = jnp.mean(y, axis=-1, keepdims=True)
        dlt = y - mu
        var = jnp.mean(dlt * dlt, axis=-1, keepdims=True)
        yn = dlt * lax.rsqrt(var + NORM_EPS)
        g = gr[off:off + C, :].astype(F32)
        yr[off:off + C, :] = (g * jax.nn.sigmoid(g) * yn).astype(BF16)


def _retention(lg, pl_ret, pc_ret, batch):
    tl = pl_ret.shape[0]
    tc = pc_ret.shape[0]
    s = tl // batch
    lc = tc // batch
    assert lc == RET_CHUNK and s % RET_CHUNK == 0
    lat = lambda off: pl.BlockSpec((s, 128), lambda b, h, off=off: (b, off + h))
    ctx = lambda off: pl.BlockSpec((lc, 128), lambda b, h, off=off: (b, off + h))
    return pl.pallas_call(
        _ret_kernel,
        out_shape=(jax.ShapeDtypeStruct((tl, 512), BF16), jax.ShapeDtypeStruct((tc, 512), BF16)),
        grid=(batch, RET_HEADS),
        in_specs=[pl.BlockSpec(memory_space=pltpu.SMEM),
                  lat(0), lat(4), lat(8), lat(12), ctx(0), ctx(4), ctx(8), ctx(12)],
        out_specs=(pl.BlockSpec((s, 128), lambda b, h: (b, h)),
                   pl.BlockSpec((lc, 128), lambda b, h: (b, h))),
        scratch_shapes=[pltpu.VMEM((s + lc, 128), F32)],
        compiler_params=_cparams(("parallel", "parallel")),
        name="retention",
    )(lg, pl_ret, pl_ret, pl_ret, pl_ret, pc_ret, pc_ret, pc_ret, pc_ret)


def _softmax_parts(parts, sink=None):
    m = functools.reduce(jnp.maximum, [jnp.max(p, axis=-1, keepdims=True) for p in parts])
    if sink is not None:
        m = jnp.maximum(m, sink)
    es = [jnp.exp2(p - m) for p in parts]
    den = functools.reduce(lambda a, b: a + b, [jnp.sum(e, axis=-1, keepdims=True) for e in es])
    if sink is not None:
        den = den + jnp.exp2(sink - m)
    return [e.astype(BF16) for e in es], 1.0 / den


def _win_kernel(sink_ref, q_ref, k0, k1, v0, v1, ck0, ck1, cv0, cv1, o_ref):
    blk = WIN_BLOCK
    nw = 3 * blk
    s_len = k0.shape[0]
    even = lax.broadcasted_iota(jnp.int32, (blk, LANES), 1) < 64
    g = WIN_HEADS // WIN_KV_HEADS
    for jj in range(q_ref.shape[0] // blk):
        rows = slice(jj * blk, (jj + 1) * blk)
        j = pl.program_id(1) * (q_ref.shape[0] // blk) + jj
        start = pl.multiple_of(jnp.clip((j - 1) * blk, 0, s_len - nw), blk)
        qpos = j * blk + lax.broadcasted_iota(jnp.int32, (blk, nw), 0)
        kpos = start + lax.broadcasted_iota(jnp.int32, (blk, nw), 1)
        bias = jnp.where(jnp.abs(qpos - kpos) <= WINDOW, 0.0, NEG_INF)
        for kv, (kr, vr, ckr, cvr) in enumerate(((k0, v0, ck0, cv0), (k1, v1, ck1, cv1))):
            kcat = jnp.concatenate([kr[pl.ds(start, nw), :], ckr[...]], axis=0)
            vcat = jnp.concatenate([vr[pl.ds(start, nw), :], cvr[...]], axis=0)
            qst = jnp.concatenate(
                [q_ref[rows, (g * kv + i) * 128:(g * kv + i + 1) * 128] for i in range(g)], axis=0)
            s = _dot_nt(qst, kcat)
            ps, invs = [], []
            for i in range(g):
                sb = s[i * blk:(i + 1) * blk]
                (pw, pc), inv = _softmax_parts([sb[:, :nw] + bias, sb[:, nw:]],
                                               sink_ref[g * kv + i] * LOG2E)
                ps.append(jnp.concatenate([pw, pc], axis=1))
                invs.append(inv)
            pv = _dot(jnp.concatenate(ps, axis=0), vcat)
            for p in range(g // 2):
                o_ref[rows, (2 * kv + p) * 128:(2 * kv + p + 1) * 128] = jnp.where(
                    even, pv[(2 * p) * blk:(2 * p + 1) * blk] * invs[2 * p],
                    pv[(2 * p + 1) * blk:(2 * p + 2) * blk] * invs[2 * p + 1]).astype(BF16)


def _window(sink, pl_win, pc_win, batch):
    tl = pl_win.shape[0]
    tc = pc_win.shape[0]
    s = tl // batch
    lc = tc // batch
    tq = 2 * WIN_BLOCK
    nb = s // tq
    lat = lambda cb: pl.BlockSpec((s, 128), lambda b, j, cb=cb: (b, cb))
    ctx = lambda cb: pl.BlockSpec((lc, 128), lambda b, j, cb=cb: (b, cb))
    return pl.pallas_call(
        _win_kernel,
        out_shape=jax.ShapeDtypeStruct((tl, 512), BF16),
        grid=(batch, nb),
        in_specs=[pl.BlockSpec(memory_space=pltpu.SMEM),
                  pl.BlockSpec((tq, 1024), lambda b, j: (b * nb + j, 0)),
                  lat(8), lat(9), lat(10), lat(11), ctx(8), ctx(9), ctx(10), ctx(11)],
        out_specs=pl.BlockSpec((tq, 512), lambda b, j: (b * nb + j, 0)),
        compiler_params=_cparams(("parallel", "arbitrary")),
        name="window_attn",
    )(sink, pl_win, pl_win, pl_win, pl_win, pl_win, pc_win, pc_win, pc_win, pc_win)


def _na_kernel(q_ref, ka, kb, kc, va, vb, vc, ck, cv, bias_ref, o_ref):
    nq = q_ref.shape[0]
    n_nb = ka.shape[0] * 3
    even = lax.broadcasted_iota(jnp.int32, (nq, LANES), 1) < 64
    for p in range(NA_HEADS // 2):
        lanes = slice(p * 128, (p + 1) * 128)
        qst = jnp.concatenate([q_ref[:, (2 * p) * 128:(2 * p + 1) * 128],
                               q_ref[:, (2 * p + 1) * 128:(2 * p + 2) * 128]], axis=0)
        kcat = jnp.concatenate([ka[:, lanes], kb[:, lanes], kc[:, lanes], ck[:, lanes]], axis=0)
        vcat = jnp.concatenate([va[:, lanes], vb[:, lanes], vc[:, lanes], cv[:, lanes]], axis=0)
        s = _dot_nt(qst, kcat)
        ps, invs = [], []
        for par in range(2):
            sb = s[par * nq:(par + 1) * nq]
            (pn, pc), inv = _softmax_parts([sb[:, :n_nb] + bias_ref[p, 0, par], sb[:, n_nb:]])
            ps.append(jnp.concatenate([pn, pc], axis=1))
            invs.append(inv)
        pv = _dot(jnp.concatenate(ps, axis=0), vcat)
        o_ref[:, lanes] = jnp.where(even, pv[:nq] * invs[0], pv[nq:] * invs[1]).astype(BF16)


def _na_pattern(i):
    return jnp.where(i == 0, 0, jnp.where(i == 7, 2, 1))


def _neighbourhood(pl_na, pc_na, bias, batch):
    tl = pl_na.shape[0]
    tc = pc_na.shape[0]
    s = tl // batch
    lc = tc // batch
    nq = NA_QROWS * GRID_W
    ng = s // nq
    kblk = nq
    nkb = s // kblk
    assert ng == 8 and NA_KROWS * GRID_W == 3 * kblk
    hp = NA_HEADS // 2

    def kspec(cb, d):
        return pl.BlockSpec(
            (kblk, 512), lambda i, b, d=d, cb=cb: (b * nkb + jnp.clip(i - 1, 0, nkb - 3) + d, cb))

    return pl.pallas_call(
        _na_kernel,
        out_shape=jax.ShapeDtypeStruct((tl, 512), BF16),
        grid=(ng, batch),
        in_specs=[pl.BlockSpec((nq, 1024), lambda i, b: (b * ng + i, 0)),
                  kspec(2, 0), kspec(2, 1), kspec(2, 2),
                  kspec(3, 0), kspec(3, 1), kspec(3, 2),
                  pl.BlockSpec((lc, 512), lambda i, b: (b, 2)),
                  pl.BlockSpec((lc, 512), lambda i, b: (b, 3)),
                  pl.BlockSpec((hp, 1, 2, nq, 3 * kblk), lambda i, b: (0, _na_pattern(i), 0, 0, 0))],
        out_specs=pl.BlockSpec((nq, 512), lambda i, b: (b * ng + i, 0)),
        compiler_params=_cparams(("parallel", "arbitrary")),
        name="neighbourhood_attn",
    )(pl_na, pl_na, pl_na, pl_na, pl_na, pl_na, pl_na, pc_na, pc_na, bias)


def _na_bias_table(rpb, s):
    rows = s // GRID_W
    kr = min(NA_ROWS_MAX, rows)
    tabs = []
    for i in (0, 1, rows // NA_QROWS - 1):
        r = NA_QROWS * i + np.arange(NA_QROWS)
        k0 = NA_QROWS * int(np.clip(i - 1, 0, rows // NA_QROWS - 3))
        krow = k0 + np.arange(NA_KROWS)
        r_start = np.clip(r - kr // 2, 0, rows - kr)
        row_ok = (krow[None, :] >= r_start[:, None]) & (krow[None, :] < r_start[:, None] + kr)
        dr = np.clip(krow[None, :] - r[:, None] + (NA_ROWS_MAX - 1), 0, 2 * NA_ROWS_MAX - 2)
        cq = np.arange(GRID_W)
        ck = np.arange(GRID_W)
        c_start = np.clip(cq - NA_COLS // 2, 0, GRID_W - NA_COLS)
        col_ok = (ck[None, :] >= c_start[:, None]) & (ck[None, :] < c_start[:, None] + NA_COLS)
        dc = np.clip(ck[None, :] - cq[:, None] + (NA_COLS - 1), 0, 2 * NA_COLS - 2)
        ok = row_ok[:, None, :, None] & col_ok[None, :, None, :]
        sel_r = (dr[:, :, None] == np.arange(2 * NA_ROWS_MAX - 1)).astype(np.float32)
        sel_c = (dc[:, :, None] == np.arange(2 * NA_COLS - 1)).astype(np.float32)
        b = jnp.einsum("qkr,hrc,xyc->hqxky", sel_r, rpb.astype(F32), sel_c,
                       precision=lax.Precision.HIGHEST)
        b = jnp.where(ok[None], b * LOG2E, NEG_INF)
        tabs.append(b.reshape(NA_HEADS, NA_QROWS * GRID_W, NA_KROWS * GRID_W))
    t = jnp.stack(tabs, axis=1)
    t = t.reshape(NA_HEADS // 2, 2, 3, t.shape[2], t.shape[3])
    return jnp.swapaxes(t, 1, 2)


def _ctx_attn_kernel(sink_ref, w_ref, n_ref, ow_ref, on_ref):
    lc = w_ref.shape[0]
    even = lax.broadcasted_iota(jnp.int32, (lc, LANES), 1) < 64
    g = WIN_HEADS // WIN_KV_HEADS
    for kv in range(WIN_KV_HEADS):
        k = w_ref[:, 1024 + kv * 128:1024 + (kv + 1) * 128]
        v = w_ref[:, 1280 + kv * 128:1280 + (kv + 1) * 128]
        qst = jnp.concatenate([w_ref[:, (g * kv + i) * 128:(g * kv + i + 1) * 128] for i in range(g)], axis=0)
        s = _dot_nt(qst, k)
        sm = [_softmax_parts([s[i * lc:(i + 1) * lc]], sink_ref[g * kv + i] * LOG2E) for i in range(g)]
        pv = _dot(jnp.concatenate([e[0][0] for e in sm], axis=0), v)
        for p in range(g // 2):
            ow_ref[:, (2 * kv + p) * 128:(2 * kv + p + 1) * 128] = jnp.where(
                even, pv[(2 * p) * lc:(2 * p + 1) * lc] * sm[2 * p][1],
                pv[(2 * p + 1) * lc:(2 * p + 2) * lc] * sm[2 * p + 1][1]).astype(BF16)
    for p in range(NA_HEADS // 2):
        k = n_ref[:, 1024 + p * 128:1024 + (p + 1) * 128]
        v = n_ref[:, 1536 + p * 128:1536 + (p + 1) * 128]
        qst = jnp.concatenate([n_ref[:, (2 * p) * 128:(2 * p + 1) * 128],
                               n_ref[:, (2 * p + 1) * 128:(2 * p + 2) * 128]], axis=0)
        s = _dot_nt(qst, k)
        sm = [_softmax_parts([s[i * lc:(i + 1) * lc]]) for i in range(2)]
        pv = _dot(jnp.concatenate([e[0][0] for e in sm], axis=0), v)
        on_ref[:, p * 128:(p + 1) * 128] = jnp.where(even, pv[:lc] * sm[0][1], pv[lc:] * sm[1][1]).astype(BF16)


def _ctx_attention(sink, pc_win, pc_na, batch):
    tc = pc_win.shape[0]
    lc = tc // batch
    row = lambda b: (b, 0)
    return pl.pallas_call(
        _ctx_attn_kernel,
        out_shape=(jax.ShapeDtypeStruct((tc, 512), BF16), jax.ShapeDtypeStruct((tc, 512), BF16)),
        grid=(batch,),
        in_specs=[pl.BlockSpec(memory_space=pltpu.SMEM),
                  pl.BlockSpec((lc, pc_win.shape[1]), row), pl.BlockSpec((lc, pc_na.shape[1]), row)],
        out_specs=(pl.BlockSpec((lc, 512), row), pl.BlockSpec((lc, 512), row)),
        compiler_params=_cparams(("parallel",)),
        name="context_attn",
    )(sink, pc_win, pc_na)


def _merge_kernel(h_ref, yr_ref, yw_ref, yn_ref, gt_ref, mod_ref, g_ref, wb_ref, wo_ref, wr_ref,
                  ho_ref, f_ref, idx_ref, gw_ref, *, with_router):
    z = None
    for r, y_ref in enumerate((yr_ref, yw_ref, yn_ref)):
        gate = jax.nn.sigmoid(gt_ref[:, r * D_MODEL:(r + 1) * D_MODEL].astype(F32))
        t = gate * _dot(y_ref[...], wb_ref[r])
        z = t if z is None else z + t
    o = _dot(z.astype(BF16), wo_ref[...])
    h = h_ref[...] + mod_ref[0, 2:3, :] * _rms(o, g_ref[1:2, :])
    ho_ref[...] = h
    f = _rms(h, g_ref[2:3, :]) * (1.0 + mod_ref[0, 4:5, :]) + mod_ref[0, 3:4, :]
    f_ref[...] = f.astype(f_ref.dtype)
    if not with_router:
        idx_ref[...] = jnp.zeros_like(idx_ref)
        gw_ref[...] = jnp.zeros_like(gw_ref)
        return
    f_hi = f.astype(BF16)
    f_lo = (f - f_hi.astype(F32)).astype(BF16)
    wr = wr_ref[...]
    w_hi = wr.astype(BF16)
    w_lo = (wr - w_hi.astype(F32)).astype(BF16)
    logits = _dot(f_hi, w_hi) + (_dot(f_lo, w_hi) + _dot(f_hi, w_lo))
    idx = lax.broadcasted_iota(jnp.int32, logits.shape, 1)
    m1 = jnp.max(logits, axis=-1, keepdims=True)
    i1 = jnp.min(jnp.where(logits == m1, idx, N_EXPERTS), axis=-1, keepdims=True)
    rest = jnp.where(idx == i1, -jnp.inf, logits)
    m2 = jnp.max(rest, axis=-1, keepdims=True)
    i2 = jnp.min(jnp.where(rest == m2, idx, N_EXPERTS), axis=-1, keepdims=True)
    e2 = jnp.exp(m2 - m1)
    inv = 1.0 / (1.0 + e2)
    first = lax.broadcasted_iota(jnp.int32, idx_ref.shape, 1) == 0
    idx_ref[...] = jnp.where(first, i1, i2)
    gw_ref[...] = jnp.where(first, inv, e2 * inv)


def _merge(h, y_ret, y_win, y_na, gates, mod, gains, w_branch, w_out, w_router, *, with_router, is_ctx,
           per_batch, tm=256):
    t, d = h.shape
    f_dtype = F32 if with_router else BF16
    if is_ctx:
        mod_map = lambda i: (mod.shape[0] - 1, 0, 0)
    else:
        mod_map = lambda i: (i // (per_batch // tm), 0, 0)
    row = lambda i: (i, 0)
    full2 = lambda i: (0, 0)
    return pl.pallas_call(
        functools.partial(_merge_kernel, with_router=with_router),
        out_shape=(jax.ShapeDtypeStruct((t, d), F32), jax.ShapeDtypeStruct((t, d), f_dtype),
                   jax.ShapeDtypeStruct((t, TOP_K), jnp.int32), jax.ShapeDtypeStruct((t, TOP_K), F32)),
        grid=(t // tm,),
        in_specs=[pl.BlockSpec((tm, d), row),
                  pl.BlockSpec((tm, BRANCH_W), row), pl.BlockSpec((tm, BRANCH_W), row),
                  pl.BlockSpec((tm, BRANCH_W), row),
                  pl.BlockSpec((tm, N_BRANCH * d), row),
                  pl.BlockSpec((1, 6, d), mod_map),
                  pl.BlockSpec(gains.shape, full2),
                  pl.BlockSpec(w_branch.shape, lambda i: (0, 0, 0)),
                  pl.BlockSpec(w_out.shape, full2),
                  pl.BlockSpec(w_router.shape, full2)],
        out_specs=(pl.BlockSpec((tm, d), row), pl.BlockSpec((tm, d), row),
                   pl.BlockSpec((tm, TOP_K), row), pl.BlockSpec((tm, TOP_K), row)),
        compiler_params=_cparams(("parallel",)),
        name="merge_ctx" if is_ctx else "merge_lat",
    )(h, y_ret, y_win, y_na, gates, mod, gains, w_branch, w_out, w_router)


def _swiglu(x, w1, w3, w2):
    a1 = _dot(x, w1)
    a3 = _dot(x, w3)
    return _dot((a1 * jax.nn.sigmoid(a1) * a3).astype(BF16), w2)


def _ffn_kernel(h_ref, f_ref, mod_ref, g_ref, w1_ref, w3_ref, w2_ref, o_ref, acc_ref):
    c = pl.program_id(1)

    @pl.when(c == 0)
    def _():
        acc_ref[...] = jnp.zeros_like(acc_ref)

    acc_ref[...] += _swiglu(f_ref[...], w1_ref[...], w3_ref[...], w2_ref[...])

    @pl.when(c == pl.num_programs(1) - 1)
    def _():
        o_ref[...] = h_ref[...] + mod_ref[0, 5:6, :] * _rms(acc_ref[...], g_ref[3:4, :])


def _ffn(h, f, mod, gains, w1, w3, w2, *, is_ctx, per_batch, tm, fc):
    t, d = h.shape
    dff = w1.shape[1]
    if is_ctx:
        mod_map = lambda i, c: (mod.shape[0] - 1, 0, 0)
    else:
        mod_map = lambda i, c: (i // (per_batch // tm), 0, 0)
    row = lambda i, c: (i, 0)
    return pl.pallas_call(
        _ffn_kernel,
        out_shape=jax.ShapeDtypeStruct((t, d), F32),
        grid=(t // tm, dff // fc),
        in_specs=[pl.BlockSpec((tm, d), row), pl.BlockSpec((tm, d), row),
                  pl.BlockSpec((1, 6, d), mod_map),
                  pl.BlockSpec(gains.shape, lambda i, c: (0, 0)),
                  pl.BlockSpec((d, fc), lambda i, c: (0, c)),
                  pl.BlockSpec((d, fc), lambda i, c: (0, c)),
                  pl.BlockSpec((fc, d), lambda i, c: (c, 0))],
        out_specs=pl.BlockSpec((tm, d), row),
        scratch_shapes=[pltpu.VMEM((tm, d), F32)],
        compiler_params=_cparams(("parallel", "arbitrary")),
        name="ffn_ctx" if is_ctx else "ffn_lat",
    )(h, f, mod, gains, w1, w3, w2)


def _route_slots(idx, tile):
    n = idx.shape[0] * TOP_K
    e = idx.reshape(n)
    onehot = (e[:, None] == jnp.arange(N_EXPERTS, dtype=jnp.int32)).astype(jnp.int32)
    csum = jnp.cumsum(onehot, axis=0)
    counts = csum[-1]
    rank = jnp.sum(csum * onehot, axis=1) - 1
    padded = (counts + tile - 1) // tile * tile
    ends = jnp.cumsum(padded)
    starts = ends - padded
    pos = jnp.sum(onehot * starts[None, :], axis=1) + rank
    n_tiles = n // tile + N_EXPERTS
    tile_start = jnp.arange(n_tiles, dtype=jnp.int32) * tile
    tile_expert = jnp.minimum(jnp.sum((tile_start[:, None] >= ends[None, :]).astype(jnp.int32), axis=1),
                              N_EXPERTS - 1)
    n_active = (ends[-1] // tile).reshape(1)
    return pos.astype(jnp.int32), tile_expert.astype(jnp.int32), n_active.astype(jnp.int32)


def _dispatch_kernel(pos_ref, f_ref, xs_in_ref, xs_ref, sem):
    del xs_in_ref
    tm = f_ref.shape[0]

    def row_copy(r, k):
        p = pos_ref[0, 0, TOP_K * r + k]
        return pltpu.make_async_copy(f_ref.at[pl.ds(r, 1), :], xs_ref.at[pl.ds(p, 1), :], sem)

    def start(r, carry):
        for k in range(TOP_K):
            row_copy(r, k).start()
        return carry

    def wait(r, carry):
        for k in range(TOP_K):
            row_copy(r, k).wait()
        return carry

    lax.fori_loop(0, tm, start, 0, unroll=8)
    lax.fori_loop(0, tm, wait, 0, unroll=8)


def _dispatch(f, pos, xs, tm=256):
    t, d = f.shape
    pos3 = pos.reshape(t // tm, 1, TOP_K * tm)
    return pl.pallas_call(
        _dispatch_kernel,
        out_shape=jax.ShapeDtypeStruct(xs.shape, xs.dtype),
        grid=(t // tm,),
        in_specs=[pl.BlockSpec((1, 1, TOP_K * tm), lambda i: (i, 0, 0), memory_space=pltpu.SMEM),
                  pl.BlockSpec((tm, d), lambda i: (i, 0)),
                  pl.BlockSpec(memory_space=pl.ANY)],
        out_specs=pl.BlockSpec(memory_space=pl.ANY),
        scratch_shapes=[pltpu.SemaphoreType.DMA(())],
        input_output_aliases={2: 0},
        compiler_params=_cparams(("arbitrary",)),
        name="moe_dispatch",
    )(pos3, f, xs)


def _experts_kernel(te_ref, na_ref, x_ref, *refs, has_prev):
    del te_ref
    if has_prev:
        yprev_ref, w1_ref, w3_ref, w2_ref, y_ref = refs
    else:
        w1_ref, w3_ref, w2_ref, y_ref = refs
    active = pl.program_id(0) < na_ref[0]

    @pl.when(active)
    def _():
        y = _swiglu(x_ref[...].astype(BF16), w1_ref[0], w3_ref[0], w2_ref[0])
        y_ref[...] = yprev_ref[...] + y if has_prev else y

    @pl.when(jnp.logical_not(active))
    def _():
        y_ref[...] = jnp.zeros_like(y_ref)


def _experts_pass(xs, y_prev, tile_expert, n_active, w1, w3, w2, *, tile, fc, c):
    p, d = xs.shape
    slot = pl.BlockSpec((tile, d), lambda j, te, na: (j, 0))
    in_specs = [slot] + ([slot] if y_prev is not None else []) + [
        pl.BlockSpec((1, d, fc), lambda j, te, na: (te[j], 0, c)),
        pl.BlockSpec((1, d, fc), lambda j, te, na: (te[j], 0, c)),
        pl.BlockSpec((1, fc, d), lambda j, te, na: (te[j], c, 0))]
    grid_spec = pltpu.PrefetchScalarGridSpec(
        num_scalar_prefetch=2, grid=(p // tile,), in_specs=in_specs, out_specs=slot)
    args = (xs,) + ((y_prev,) if y_prev is not None else ()) + (w1, w3, w2)
    return pl.pallas_call(
        functools.partial(_experts_kernel, has_prev=y_prev is not None),
        out_shape=jax.ShapeDtypeStruct((p, d), F32),
        grid_spec=grid_spec,
        compiler_params=_cparams(("arbitrary",)),
        name="moe_experts",
    )(tile_expert, n_active, *args)


def _experts(xs, tile_expert, n_active, w1, w3, w2, *, tile, fc):
    y = None
    for c in range(w1.shape[2] // fc):
        y = _experts_pass(xs, y, tile_expert, n_active, w1, w3, w2, tile=tile, fc=fc, c=c)
    return y


def _combine_kernel(pos_ref, gw_ref, h_ref, mod_ref, g_ref, y_any, o_ref, buf_ref, sem):
    tm = h_ref.shape[0]

    def row_copy(r, k):
        p = pos_ref[0, 0, TOP_K * r + k]
        return pltpu.make_async_copy(y_any.at[pl.ds(p, 1), :], buf_ref.at[k, pl.ds(r, 1), :], sem)

    def start(r, carry):
        for k in range(TOP_K):
            row_copy(r, k).start()
        return carry

    def wait(r, carry):
        for k in range(TOP_K):
            row_copy(r, k).wait()
        return carry

    lax.fori_loop(0, tm, start, 0, unroll=8)
    lax.fori_loop(0, tm, wait, 0, unroll=8)
    gw = gw_ref[...]
    y = gw[:, 0:1] * buf_ref[0] + gw[:, 1:2] * buf_ref[1]
    o_ref[...] = h_ref[...] + mod_ref[0, 5:6, :] * _rms(y, g_ref[3:4, :])


def _combine(h, pos, gw, y_sorted, mod, gains, *, is_ctx, per_batch, tm=256):
    t, d = h.shape
    pos3 = pos.reshape(t // tm, 1, TOP_K * tm)
    if is_ctx:
        mod_map = lambda i: (mod.shape[0] - 1, 0, 0)
    else:
        mod_map = lambda i: (i // (per_batch // tm), 0, 0)
    row = lambda i: (i, 0)
    return pl.pallas_call(
        _combine_kernel,
        out_shape=jax.ShapeDtypeStruct((t, d), F32),
        grid=(t // tm,),
        in_specs=[pl.BlockSpec((1, 1, TOP_K * tm), lambda i: (i, 0, 0), memory_space=pltpu.SMEM),
                  pl.BlockSpec((tm, TOP_K), row),
                  pl.BlockSpec((tm, d), row),
                  pl.BlockSpec((1, 6, d), mod_map),
                  pl.BlockSpec(gains.shape, lambda i: (0, 0)),
                  pl.BlockSpec(memory_space=pl.ANY)],
        out_specs=pl.BlockSpec((tm, d), row),
        scratch_shapes=[pltpu.VMEM((TOP_K, tm, d), F32), pltpu.SemaphoreType.DMA(())],
        compiler_params=_cparams(("arbitrary",)),
        name="moe_combine_ctx" if is_ctx else "moe_combine_lat",
    )(pos3, gw, h, mod, gains, y_sorted)


def _rope_tables(s, lc):
    def lin(pos):
        n_freq = RET_DK // 2
        inv = ROPE_BASE ** (-jnp.arange(n_freq, dtype=F32) / n_freq)
        ang = pos.astype(F32)[:, None] * inv
        cos, sin = jnp.cos(ang), jnp.sin(ang)
        return jnp.concatenate([cos, cos], -1), jnp.concatenate([-sin, sin], -1)

    t = jnp.arange(s)
    n_freq = WIN_HD // 4
    inv = ROPE_BASE ** (-jnp.arange(n_freq, dtype=F32) / n_freq)
    rowa = (t // GRID_W).astype(F32)[:, None] * inv
    cola = (t % GRID_W).astype(F32)[:, None] * inv
    ang = jnp.concatenate([rowa, cola], axis=-1)
    cos, sin = jnp.cos(ang), jnp.sin(ang)
    ca = jnp.tile(jnp.concatenate([cos, cos], -1), (1, 2))
    sa = jnp.tile(jnp.concatenate([-sin, sin], -1), (1, 2))
    cr_l, sr_l = lin(lc + jnp.arange(s))
    cr_c, sr_c = lin(jnp.arange(lc))
    return (cr_l, sr_l, ca, sa), (cr_c, sr_c, ca[:lc], sa[:lc])


def kernel(x, c, ctx, c_ctx, w_mod, b_mod, norm_gains, w_in, ret_decay, win_sink, na_rpb, w_branch, w_out,
           ffn_w1, ffn_w3, ffn_w2, moe_router, moe_w1, moe_w3, moe_w2):
    batch, s, d = x.shape
    lc = ctx.shape[1]
    depth = w_mod.shape[0]

    c_all = jnp.zeros((16, d), F32).at[:batch].set(c).at[batch].set(c_ctx)
    mods = _modulation(c_all, w_mod, b_mod)[:, :batch + 1].reshape(depth, batch + 1, 6, d)

    tabs_l, tabs_c = _rope_tables(s, lc)
    log_gamma = jnp.log1p(-jnp.exp(ret_decay.astype(F32)))

    h_l = x.reshape(batch * s, d)
    h_c = ctx.reshape(batch * lc, d)
    zero_router = jnp.zeros((d, N_EXPERTS), F32)

    for layer in range(depth):
        need_ctx = layer < depth - 1
        mod = mods[layer]
        gains = norm_gains[layer]
        w_in_b = w_in[layer].astype(BF16)
        wb_b = w_branch[layer].astype(BF16)
        wo_b = w_out[layer].astype(BF16)
        i = layer // 2
        is_moe = layer % 2 == 1
        if is_moe:
            w1, w3, w2 = moe_w1[i].astype(BF16), moe_w3[i].astype(BF16), moe_w2[i].astype(BF16)
            w_router = moe_router[i]
        else:
            w1, w3, w2 = ffn_w1[i].astype(BF16), ffn_w3[i].astype(BF16), ffn_w2[i].astype(BF16)
            w_router = zero_router

        p_ret_l, p_win_l, p_na_l, p_gt_l = _inproj(h_l, mod, gains, w_in_b, *tabs_l, is_ctx=False)
        p_ret_c, p_win_c, p_na_c, p_gt_c = _inproj(h_c, mod, gains, w_in_b, *tabs_c, is_ctx=True)

        y_ret_l, y_ret_c = _retention(log_gamma[layer], p_ret_l, p_ret_c, batch)
        y_win_l = _window(win_sink[layer], p_win_l, p_win_c, batch)
        y_na_l = _neighbourhood(p_na_l, p_na_c, _na_bias_table(na_rpb[layer], s), batch)

        h_l, f_l, idx_l, gw_l = _merge(h_l, y_ret_l, y_win_l, y_na_l, p_gt_l, mod, gains, wb_b, wo_b,
                                       w_router, with_router=is_moe, is_ctx=False, per_batch=s)
        if need_ctx:
            y_win_c, y_na_c = _ctx_attention(win_sink[layer], p_win_c, p_na_c, batch)
            h_c, f_c, idx_c, gw_c = _merge(h_c, y_ret_c, y_win_c, y_na_c, p_gt_c, mod, gains, wb_b, wo_b,
                                           w_router, with_router=is_moe, is_ctx=True, per_batch=lc)

        if not is_moe:
            h_l = _ffn(h_l, f_l, mod, gains, w1, w3, w2, is_ctx=False, per_batch=s, tm=512, fc=1408)
            if need_ctx:
                h_c = _ffn(h_c, f_c, mod, gains, w1, w3, w2, is_ctx=True, per_batch=lc, tm=512, fc=1408)
            continue

        idx_all = jnp.concatenate([idx_l, idx_c], axis=0) if need_ctx else idx_l
        pos, tile_expert, n_active = _route_slots(idx_all, MOE_TILE)
        n_slots = tile_expert.shape[0] * MOE_TILE
        xs = jnp.zeros((n_slots, d), F32)
        n_l = TOP_K * h_l.shape[0]
        xs = _dispatch(f_l, pos[:n_l], xs)
        if need_ctx:
            xs = _dispatch(f_c, pos[n_l:], xs)
        y_sorted = _experts(xs, tile_expert, n_active, w1, w3, w2, tile=MOE_TILE, fc=MOE_FC)
        h_l = _combine(h_l, pos[:n_l], gw_l, y_sorted, mod, gains, is_ctx=False, per_batch=s)
        if need_ctx:
            h_c = _combine(h_c, pos[n_l:], gw_c, y_sorted, mod, gains, is_ctx=True, per_batch=lc)
    return h_l.reshape(batch, s, d)
```

```python
import functools
import math

import jax
import jax.numpy as jnp
import numpy as np
from jax import lax
from jax.experimental import pallas as pl
from jax.experimental.pallas import tpu as pltpu

F32 = jnp.float32
BF16 = jnp.bfloat16

D_MODEL = 1024
GRID_W = 64
RET_HEADS = 4
RET_DK = 128
WIN_HEADS = 8
WIN_KV_HEADS = 2
WIN_HD = 64
WINDOW = 128
WIN_BLOCK = 128
NA_HEADS = 8
NA_HD = 64
NA_ROWS_MAX = 8
NA_COLS = 16
BRANCH_W = 512
N_BRANCH = 3
N_EXPERTS = 8
TOP_K = 2
ROPE_BASE = 10000.0
NORM_EPS = 1e-6
NEG_INF = -1e30
LOG2E = math.log2(math.e)

LANES = 128
VMEM_LIMIT = 56 * 1024 * 1024

C_RQ, C_RK, C_RV, C_RG = 0, 512, 1024, 1536
C_WQ, C_WK, C_WV = 2048, 2560, 2688
C_NQ, C_NK, C_NV = 2816, 3328, 3840
C_GATE = 4352
D_IN = 7424

RET_CHUNK = 256
NA_QROWS = 4
NA_KROWS = 12
FF_CHUNK = 256
MOE_TILE = 512
MOE_FC = 1792


def _cparams(sem):
    return pltpu.CompilerParams(dimension_semantics=sem, vmem_limit_bytes=VMEM_LIMIT)


def _rms(x, g):
    return x * lax.rsqrt(jnp.mean(x * x, axis=-1, keepdims=True) + NORM_EPS) * g


def _sigmoid(x):
    return 0.5 * jnp.tanh(0.5 * x) + 0.5


def _dot(a, b):
    return jnp.dot(a, b, preferred_element_type=F32)


def _dot_nt(a, b):
    return lax.dot_general(a, b, (((1,), (1,)), ((), ())), preferred_element_type=F32)


def _dot_tn(a, b):
    return lax.dot_general(a, b, (((0,), (0,)), ((), ())), preferred_element_type=F32)


def _mod_kernel(c_ref, w_ref, b_ref, o_ref):
    c = c_ref[...]
    s = c * jax.nn.sigmoid(c)
    o_ref[0] = jnp.dot(s, w_ref[0], preferred_element_type=F32,
                       precision=lax.Precision.HIGHEST) + b_ref[0]


def _modulation(c_all, w_mod, b_mod):
    depth, d, n = w_mod.shape
    tn = 1536
    return pl.pallas_call(
        _mod_kernel,
        out_shape=jax.ShapeDtypeStruct((depth, c_all.shape[0], n), F32),
        grid=(depth, n // tn),
        in_specs=[pl.BlockSpec(c_all.shape, lambda l, j: (0, 0)),
                  pl.BlockSpec((1, d, tn), lambda l, j: (l, 0, j)),
                  pl.BlockSpec((1, 1, tn), lambda l, j: (l, 0, j))],
        out_specs=pl.BlockSpec((1, c_all.shape[0], tn), lambda l, j: (l, 0, j)),
        compiler_params=_cparams(("arbitrary", "arbitrary")),
        name="modulation",
    )(c_all, w_mod, b_mod.reshape(depth, 1, n))


def _inproj_kernel(h_ref, mod_ref, g_ref, w_ref, cr_ref, sr_ref, ca_ref, sa_ref,
                   oret_ref, owin_ref, ona_ref, og_ref, *, rope_win):
    x = h_ref[...]
    a = _rms(x, g_ref[0:1, :]) * (1.0 + mod_ref[0, 1:2, :]) + mod_ref[0, 0:1, :]
    a = a.astype(BF16)
    tm = x.shape[0]
    lane = lax.broadcasted_iota(jnp.int32, (tm, LANES), 1)
    even = lane < 64
    first_half = (lane % 64) < 32

    def mm(c0, cw):
        return _dot(a, w_ref[:, c0:c0 + cw])

    def rope_lin(v):
        return v * cr_ref[...] + pltpu.roll(v, 64, 1) * sr_ref[...]

    def rope_ax(v):
        rot = jnp.where(first_half, pltpu.roll(v, 96, 1), pltpu.roll(v, 32, 1))
        return v * ca_ref[...] + rot * sa_ref[...]

    r = mm(C_RQ, 512)
    for h in range(RET_HEADS):
        blk = rope_lin(r[:, h * 128:(h + 1) * 128]) * (RET_DK ** -0.5)
        oret_ref[:, h * 128:(h + 1) * 128] = blk.astype(BF16)
    r = mm(C_RK, 512)
    for h in range(RET_HEADS):
        blk = rope_lin(r[:, h * 128:(h + 1) * 128])
        oret_ref[:, 512 + h * 128:512 + (h + 1) * 128] = blk.astype(BF16)
    oret_ref[:, 1024:1536] = mm(C_RV, 512).astype(BF16)
    oret_ref[:, 1536:2048] = mm(C_RG, 512).astype(BF16)

    r = mm(C_WQ, 512)
    for p in range(WIN_HEADS // 2):
        blk = r[:, p * 128:(p + 1) * 128]
        if rope_win:
            blk = rope_ax(blk)
        blk = blk * (WIN_HD ** -0.5 * LOG2E)
        owin_ref[:, (2 * p) * 128:(2 * p + 1) * 128] = jnp.where(even, blk, 0.0).astype(BF16)
        owin_ref[:, (2 * p + 1) * 128:(2 * p + 2) * 128] = jnp.where(even, 0.0, blk).astype(BF16)
    r = mm(C_WK, 256)
    k2 = r[:, 0:128]
    if rope_win:
        k2 = rope_ax(k2)
    v2 = r[:, 128:256]
    for i, t in enumerate((k2, v2)):
        sw = pltpu.roll(t, 64, 1)
        owin_ref[:, 1024 + 256 * i:1024 + 256 * i + 128] = jnp.where(even, t, sw).astype(BF16)
        owin_ref[:, 1024 + 256 * i + 128:1024 + 256 * i + 256] = jnp.where(even, sw, t).astype(BF16)

    r = mm(C_NQ, 512) * (NA_HD ** -0.5 * LOG2E)
    for p in range(NA_HEADS // 2):
        blk = r[:, p * 128:(p + 1) * 128]
        ona_ref[:, (2 * p) * 128:(2 * p + 1) * 128] = jnp.where(even, blk, 0.0).astype(BF16)
        ona_ref[:, (2 * p + 1) * 128:(2 * p + 2) * 128] = jnp.where(even, 0.0, blk).astype(BF16)
    ona_ref[:, 1024:1536] = mm(C_NK, 512).astype(BF16)
    ona_ref[:, 1536:2048] = mm(C_NV, 512).astype(BF16)

    for j in range(6):
        og_ref[:, j * 512:(j + 1) * 512] = mm(C_GATE + j * 512, 512).astype(BF16)


def _inproj(h, mod, gains, w_in, cr, sr, ca, sa, *, is_ctx, tm=256):
    t, d = h.shape
    nt = t // tm
    per_batch = cr.shape[0] // tm
    if is_ctx:
        mod_map = lambda i: (mod.shape[0] - 1, 0, 0)
    else:
        mod_map = lambda i: (i // per_batch, 0, 0)
    pos_map = lambda i: (i % per_batch, 0)
    full = lambda i: (0, 0)
    row = lambda i: (i, 0)
    outs = (jax.ShapeDtypeStruct((t, 2048), BF16), jax.ShapeDtypeStruct((t, 1536), BF16),
            jax.ShapeDtypeStruct((t, 2048), BF16), jax.ShapeDtypeStruct((t, 3072), BF16))
    return pl.pallas_call(
        functools.partial(_inproj_kernel, rope_win=not is_ctx),
        out_shape=outs,
        grid=(nt,),
        in_specs=[pl.BlockSpec((tm, d), row),
                  pl.BlockSpec((1, 6, d), mod_map),
                  pl.BlockSpec(gains.shape, full),
                  pl.BlockSpec(w_in.shape, full, pipeline_mode=pl.Buffered(1)),
                  pl.BlockSpec((tm, LANES), pos_map), pl.BlockSpec((tm, LANES), pos_map),
                  pl.BlockSpec((tm, LANES), pos_map), pl.BlockSpec((tm, LANES), pos_map)],
        out_specs=(pl.BlockSpec((tm, 2048), row), pl.BlockSpec((tm, 1536), row),
                   pl.BlockSpec((tm, 2048), row), pl.BlockSpec((tm, 3072), row)),
        compiler_params=_cparams(("parallel",)),
        name="inproj_ctx" if is_ctx else "inproj_lat",
    )(h, mod, gains, w_in, cr, sr, ca, sa)


def _ret_kernel(lg_ref, ql, kl, vl, gl, qc, kc, vc, gc, yl_ref, yc_ref, acc_ref):
    C = RET_CHUNK
    h = pl.program_id(1)
    lgf = lg_ref[0, h]
    lgb = lg_ref[1, h]
    ii = lax.broadcasted_iota(jnp.int32, (C, C), 0).astype(F32)
    jj = lax.broadcasted_iota(jnp.int32, (C, C), 1).astype(F32)
    diff = ii - jj
    dmat = jnp.where(diff >= 0, jnp.exp(lgf * jnp.maximum(diff, 0.0)),
                     jnp.exp(lgb * jnp.maximum(-diff, 0.0)))
    r = lax.broadcasted_iota(jnp.int32, (C, RET_DK), 0).astype(F32)
    qdf = jnp.exp(lgf * (r + 1.0))
    kdf = jnp.exp(lgf * (C - 1.0 - r))
    qdb = jnp.exp(lgb * (C - r))
    kdb = jnp.exp(lgb * r)
    ones = jnp.ones((RET_DK, RET_DK), F32)
    cdf = jnp.exp(ones * (lgf * C))
    cdb = jnp.exp(ones * (lgb * C))

    n_lat = ql.shape[0] // C
    lc = qc.shape[0]
    chunks = [(qc, kc, vc, gc, yc_ref, 0, 0)]
    chunks += [(ql, kl, vl, gl, yl_ref, j * C, lc + j * C) for j in range(n_lat)]

    state = jnp.zeros((RET_DK, RET_DK), F32)
    for (qr, kr, vr, _, _, off, aoff) in chunks:
        q = qr[off:off + C, :]
        k = kr[off:off + C, :]
        v = vr[off:off + C, :]
        s = _dot_nt(q, k) * dmat
        y = _dot(s.astype(BF16), v)
        y = y + _dot((q.astype(F32) * qdf).astype(BF16), state.astype(BF16))
        state = state * cdf + _dot_tn((k.astype(F32) * kdf).astype(BF16), v)
        acc_ref[aoff:aoff + C, :] = y

    state = jnp.zeros((RET_DK, RET_DK), F32)
    for (qr, kr, vr, gr, yr, off, aoff) in [chunks[0]] + chunks[:0:-1]:
        q = qr[off:off + C, :]
        k = kr[off:off + C, :]
        v = vr[off:off + C, :]
        y = acc_ref[aoff:aoff + C, :] + _dot((q.astype(F32) * qdb).astype(BF16), state.astype(BF16))
        state = state * cdb + _dot_tn((k.astype(F32) * kdb).astype(BF16), v)
        mu = jnp.mean(y, axis=-1, keepdims=True)
        dlt = y - mu
        var = jnp.mean(dlt * dlt, axis=-1, keepdims=True)
        yn = dlt * lax.rsqrt(var + NORM_EPS)
        g = gr[off:off + C, :].astype(F32)
        yr[off:off + C, :] = (g * _sigmoid(g) * yn).astype(BF16)


def _retention(lg, pl_ret, pc_ret, batch):
    tl = pl_ret.shape[0]
    tc = pc_ret.shape[0]
    s = tl // batch
    lc = tc // batch
    assert lc == RET_CHUNK and s % RET_CHUNK == 0
    lat = lambda off: pl.BlockSpec((s, 128), lambda b, h, off=off: (b, off + h))
    ctx = lambda off: pl.BlockSpec((lc, 128), lambda b, h, off=off: (b, off + h))
    return pl.pallas_call(
        _ret_kernel,
        out_shape=(jax.ShapeDtypeStruct((tl, 512), BF16), jax.ShapeDtypeStruct((tc, 512), BF16)),
        grid=(batch, RET_HEADS),
        in_specs=[pl.BlockSpec(memory_space=pltpu.SMEM),
                  lat(0), lat(4), lat(8), lat(12), ctx(0), ctx(4), ctx(8), ctx(12)],
        out_specs=(pl.BlockSpec((s, 128), lambda b, h: (b, h)),
                   pl.BlockSpec((lc, 128), lambda b, h: (b, h))),
        scratch_shapes=[pltpu.VMEM((s + lc, 128), F32)],
        compiler_params=_cparams(("parallel", "parallel")),
        name="retention",
    )(lg, pl_ret, pl_ret, pl_ret, pl_ret, pc_ret, pc_ret, pc_ret, pc_ret)


def _softmax_parts(parts, sink=None):
    m = functools.reduce(jnp.maximum, [jnp.max(p, axis=-1, keepdims=True) for p in parts])
    if sink is not None:
        m = jnp.maximum(m, sink)
    es = [jnp.exp2(p - m) for p in parts]
    den = functools.reduce(lambda a, b: a + b, [jnp.sum(e, axis=-1, keepdims=True) for e in es])
    if sink is not None:
        den = den + jnp.exp2(sink - m)
    return [e.astype(BF16) for e in es], 1.0 / den


def _win_kernel(sink_ref, q_ref, k0, k1, v0, v1, ck0, ck1, cv0, cv1, o_ref):
    blk = WIN_BLOCK
    nw = 3 * blk
    s_len = k0.shape[0]
    even = lax.broadcasted_iota(jnp.int32, (blk, LANES), 1) < 64
    g = WIN_HEADS // WIN_KV_HEADS
    for jj in range(q_ref.shape[0] // blk):
        rows = slice(jj * blk, (jj + 1) * blk)
        j = pl.program_id(1) * (q_ref.shape[0] // blk) + jj
        start = pl.multiple_of(jnp.clip((j - 1) * blk, 0, s_len - nw), blk)
        qpos = j * blk + lax.broadcasted_iota(jnp.int32, (blk, nw), 0)
        kpos = start + lax.broadcasted_iota(jnp.int32, (blk, nw), 1)
        bias = jnp.where(jnp.abs(qpos - kpos) <= WINDOW, 0.0, NEG_INF)
        for kv, (kr, vr, ckr, cvr) in enumerate(((k0, v0, ck0, cv0), (k1, v1, ck1, cv1))):
            kcat = jnp.concatenate([kr[pl.ds(start, nw), :], ckr[...]], axis=0)
            vcat = jnp.concatenate([vr[pl.ds(start, nw), :], cvr[...]], axis=0)
            qst = jnp.concatenate(
                [q_ref[rows, (g * kv + i) * 128:(g * kv + i + 1) * 128] for i in range(g)], axis=0)
            s = _dot_nt(qst, kcat)
            ps, invs = [], []
            for i in range(g):
                sb = s[i * blk:(i + 1) * blk]
                (pw, pc), inv = _softmax_parts([sb[:, :nw] + bias, sb[:, nw:]],
                                               sink_ref[g * kv + i] * LOG2E)
                ps.append(jnp.concatenate([pw, pc], axis=1))
                invs.append(inv)
            pv = _dot(jnp.concatenate(ps, axis=0), vcat)
            for p in range(g // 2):
                o_ref[rows, (2 * kv + p) * 128:(2 * kv + p + 1) * 128] = jnp.where(
                    even, pv[(2 * p) * blk:(2 * p + 1) * blk] * invs[2 * p],
                    pv[(2 * p + 1) * blk:(2 * p + 2) * blk] * invs[2 * p + 1]).astype(BF16)


def _window(sink, pl_win, pc_win, batch):
    tl = pl_win.shape[0]
    tc = pc_win.shape[0]
    s = tl // batch
    lc = tc // batch
    tq = 2 * WIN_BLOCK
    nb = s // tq
    lat = lambda cb: pl.BlockSpec((s, 128), lambda b, j, cb=cb: (b, cb))
    ctx = lambda cb: pl.BlockSpec((lc, 128), lambda b, j, cb=cb: (b, cb))
    return pl.pallas_call(
        _win_kernel,
        out_shape=jax.ShapeDtypeStruct((tl, 512), BF16),
        grid=(batch, nb),
        in_specs=[pl.BlockSpec(memory_space=pltpu.SMEM),
                  pl.BlockSpec((tq, 1024), lambda b, j: (b * nb + j, 0)),
                  lat(8), lat(9), lat(10), lat(11), ctx(8), ctx(9), ctx(10), ctx(11)],
        out_specs=pl.BlockSpec((tq, 512), lambda b, j: (b * nb + j, 0)),
        compiler_params=_cparams(("parallel", "arbitrary")),
        name="window_attn",
    )(sink, pl_win, pl_win, pl_win, pl_win, pl_win, pc_win, pc_win, pc_win, pc_win)


def _na_kernel(q_ref, ka, kb, kc, va, vb, vc, ck, cv, bias_ref, o_ref):
    nq = q_ref.shape[0]
    n_nb = ka.shape[0] * 3
    even = lax.broadcasted_iota(jnp.int32, (nq, LANES), 1) < 64
    for p in range(NA_HEADS // 2):
        lanes = slice(p * 128, (p + 1) * 128)
        qst = jnp.concatenate([q_ref[:, (2 * p) * 128:(2 * p + 1) * 128],
                               q_ref[:, (2 * p + 1) * 128:(2 * p + 2) * 128]], axis=0)
        kcat = jnp.concatenate([ka[:, lanes], kb[:, lanes], kc[:, lanes], ck[:, lanes]], axis=0)
        vcat = jnp.concatenate([va[:, lanes], vb[:, lanes], vc[:, lanes], cv[:, lanes]], axis=0)
        s = _dot_nt(qst, kcat)
        ps, invs = [], []
        for par in range(2):
            sb = s[par * nq:(par + 1) * nq]
            (pn, pc), inv = _softmax_parts([sb[:, :n_nb] + bias_ref[0, p, par, 0], sb[:, n_nb:]])
            ps.append(jnp.concatenate([pn, pc], axis=1))
            invs.append(inv)
        pv = _dot(jnp.concatenate(ps, axis=0), vcat)
        o_ref[:, lanes] = jnp.where(even, pv[:nq] * invs[0], pv[nq:] * invs[1]).astype(BF16)


def _na_pattern(i):
    return jnp.where(i == 0, 0, jnp.where(i == 7, 2, 1))


def _neighbourhood(pl_na, pc_na, bias, layer, batch):
    tl = pl_na.shape[0]
    tc = pc_na.shape[0]
    s = tl // batch
    lc = tc // batch
    nq = NA_QROWS * GRID_W
    ng = s // nq
    kblk = nq
    nkb = s // kblk
    assert ng == 8 and NA_KROWS * GRID_W == 3 * kblk
    hp = NA_HEADS // 2

    def kspec(cb, d):
        return pl.BlockSpec(
            (kblk, 512), lambda i, b, d=d, cb=cb: (b * nkb + jnp.clip(i - 1, 0, nkb - 3) + d, cb))

    return pl.pallas_call(
        _na_kernel,
        out_shape=jax.ShapeDtypeStruct((tl, 512), BF16),
        grid=(ng, batch),
        in_specs=[pl.BlockSpec((nq, 1024), lambda i, b: (b * ng + i, 0)),
                  kspec(2, 0), kspec(2, 1), kspec(2, 2),
                  kspec(3, 0), kspec(3, 1), kspec(3, 2),
                  pl.BlockSpec((lc, 512), lambda i, b: (b, 2)),
                  pl.BlockSpec((lc, 512), lambda i, b: (b, 3)),
                  pl.BlockSpec((1, hp, 2, 1, nq, 3 * kblk),
                               lambda i, b: (layer, 0, 0, _na_pattern(i), 0, 0))],
        out_specs=pl.BlockSpec((nq, 512), lambda i, b: (b * ng + i, 0)),
        compiler_params=_cparams(("parallel", "arbitrary")),
        name="neighbourhood_attn",
    )(pl_na, pl_na, pl_na, pl_na, pl_na, pl_na, pl_na, pc_na, pc_na, bias)


def _na_bias_tables(rpb, s):
    depth = rpb.shape[0]
    rows = s // GRID_W
    kr = min(NA_ROWS_MAX, rows)
    tabs = []
    for i in (0, 1, rows // NA_QROWS - 1):
        r = NA_QROWS * i + np.arange(NA_QROWS)
        k0 = NA_QROWS * int(np.clip(i - 1, 0, rows // NA_QROWS - 3))
        krow = k0 + np.arange(NA_KROWS)
        r_start = np.clip(r - kr // 2, 0, rows - kr)
        row_ok = (krow[None, :] >= r_start[:, None]) & (krow[None, :] < r_start[:, None] + kr)
        dr = np.clip(krow[None, :] - r[:, None] + (NA_ROWS_MAX - 1), 0, 2 * NA_ROWS_MAX - 2)
        cq = np.arange(GRID_W)
        ck = np.arange(GRID_W)
        c_start = np.clip(cq - NA_COLS // 2, 0, GRID_W - NA_COLS)
        col_ok = (ck[None, :] >= c_start[:, None]) & (ck[None, :] < c_start[:, None] + NA_COLS)
        dc = np.clip(ck[None, :] - cq[:, None] + (NA_COLS - 1), 0, 2 * NA_COLS - 2)
        ok = row_ok[:, None, :, None] & col_ok[None, :, None, :]
        sel_r = (dr[:, :, None] == np.arange(2 * NA_ROWS_MAX - 1)).astype(np.float32)
        sel_c = (dc[:, :, None] == np.arange(2 * NA_COLS - 1)).astype(np.float32)
        b = jnp.einsum("qkr,lhrc,xyc->lhqxky", sel_r, rpb.astype(F32), sel_c,
                       precision=lax.Precision.HIGHEST)
        b = jnp.where(ok[None, None], b * LOG2E, NEG_INF)
        tabs.append(b.reshape(depth, NA_HEADS, NA_QROWS * GRID_W, NA_KROWS * GRID_W))
    t = jnp.stack(tabs, axis=2)
    return t.reshape(depth, NA_HEADS // 2, 2, 3, t.shape[3], t.shape[4])


def _ctx_attn_kernel(sink_ref, w_ref, n_ref, ow_ref, on_ref):
    lc = w_ref.shape[0]
    even = lax.broadcasted_iota(jnp.int32, (lc, LANES), 1) < 64
    g = WIN_HEADS // WIN_KV_HEADS
    for kv in range(WIN_KV_HEADS):
        k = w_ref[:, 1024 + kv * 128:1024 + (kv + 1) * 128]
        v = w_ref[:, 1280 + kv * 128:1280 + (kv + 1) * 128]
        qst = jnp.concatenate([w_ref[:, (g * kv + i) * 128:(g * kv + i + 1) * 128] for i in range(g)], axis=0)
        s = _dot_nt(qst, k)
        sm = [_softmax_parts([s[i * lc:(i + 1) * lc]], sink_ref[g * kv + i] * LOG2E) for i in range(g)]
        pv = _dot(jnp.concatenate([e[0][0] for e in sm], axis=0), v)
        for p in range(g // 2):
            ow_ref[:, (2 * kv + p) * 128:(2 * kv + p + 1) * 128] = jnp.where(
                even, pv[(2 * p) * lc:(2 * p + 1) * lc] * sm[2 * p][1],
                pv[(2 * p + 1) * lc:(2 * p + 2) * lc] * sm[2 * p + 1][1]).astype(BF16)
    for p in range(NA_HEADS // 2):
        k = n_ref[:, 1024 + p * 128:1024 + (p + 1) * 128]
        v = n_ref[:, 1536 + p * 128:1536 + (p + 1) * 128]
        qst = jnp.concatenate([n_ref[:, (2 * p) * 128:(2 * p + 1) * 128],
                               n_ref[:, (2 * p + 1) * 128:(2 * p + 2) * 128]], axis=0)
        s = _dot_nt(qst, k)
        sm = [_softmax_parts([s[i * lc:(i + 1) * lc]]) for i in range(2)]
        pv = _dot(jnp.concatenate([e[0][0] for e in sm], axis=0), v)
        on_ref[:, p * 128:(p + 1) * 128] = jnp.where(even, pv[:lc] * sm[0][1], pv[lc:] * sm[1][1]).astype(BF16)


def _ctx_attention(sink, pc_win, pc_na, batch):
    tc = pc_win.shape[0]
    lc = tc // batch
    row = lambda b: (b, 0)
    return pl.pallas_call(
        _ctx_attn_kernel,
        out_shape=(jax.ShapeDtypeStruct((tc, 512), BF16), jax.ShapeDtypeStruct((tc, 512), BF16)),
        grid=(batch,),
        in_specs=[pl.BlockSpec(memory_space=pltpu.SMEM),
                  pl.BlockSpec((lc, pc_win.shape[1]), row), pl.BlockSpec((lc, pc_na.shape[1]), row)],
        out_specs=(pl.BlockSpec((lc, 512), row), pl.BlockSpec((lc, 512), row)),
        compiler_params=_cparams(("parallel",)),
        name="context_attn",
    )(sink, pc_win, pc_na)


def _merge_kernel(h_ref, yr_ref, yw_ref, yn_ref, gt_ref, mod_ref, g_ref, wb_ref, wo_ref, wr_ref,
                  ho_ref, f_ref, idx_ref, gw_ref, *, with_router):
    z = None
    for r, y_ref in enumerate((yr_ref, yw_ref, yn_ref)):
        gate = _sigmoid(gt_ref[:, r * D_MODEL:(r + 1) * D_MODEL].astype(F32))
        t = gate * _dot(y_ref[...], wb_ref[r])
        z = t if z is None else z + t
    o = _dot(z.astype(BF16), wo_ref[...])
    h = h_ref[...] + mod_ref[0, 2:3, :] * _rms(o, g_ref[1:2, :])
    ho_ref[...] = h
    f = _rms(h, g_ref[2:3, :]) * (1.0 + mod_ref[0, 4:5, :]) + mod_ref[0, 3:4, :]
    f_ref[...] = f.astype(f_ref.dtype)
    if not with_router:
        idx_ref[...] = jnp.zeros_like(idx_ref)
        gw_ref[...] = jnp.zeros_like(gw_ref)
        return
    f_hi = f.astype(BF16)
    f_lo = (f - f_hi.astype(F32)).astype(BF16)
    wr = wr_ref[...]
    w_hi = wr.astype(BF16)
    w_lo = (wr - w_hi.astype(F32)).astype(BF16)
    logits = _dot(f_hi, w_hi) + (_dot(f_lo, w_hi) + _dot(f_hi, w_lo))
    idx = lax.broadcasted_iota(jnp.int32, logits.shape, 1)
    m1 = jnp.max(logits, axis=-1, keepdims=True)
    i1 = jnp.min(jnp.where(logits == m1, idx, N_EXPERTS), axis=-1, keepdims=True)
    rest = jnp.where(idx == i1, -jnp.inf, logits)
    m2 = jnp.max(rest, axis=-1, keepdims=True)
    i2 = jnp.min(jnp.where(rest == m2, idx, N_EXPERTS), axis=-1, keepdims=True)
    e2 = jnp.exp(m2 - m1)
    inv = 1.0 / (1.0 + e2)
    first = lax.broadcasted_iota(jnp.int32, idx_ref.shape, 1) == 0
    idx_ref[...] = jnp.where(first, i1, i2)
    gw_ref[...] = jnp.where(first, inv, e2 * inv)


def _merge(h, y_ret, y_win, y_na, gates, mod, gains, w_branch, w_out, w_router, *, with_router, is_ctx,
           per_batch, tm=256):
    t, d = h.shape
    f_dtype = F32 if with_router else BF16
    if is_ctx:
        mod_map = lambda i: (mod.shape[0] - 1, 0, 0)
    else:
        mod_map = lambda i: (i // (per_batch // tm), 0, 0)
    row = lambda i: (i, 0)
    full2 = lambda i: (0, 0)
    return pl.pallas_call(
        functools.partial(_merge_kernel, with_router=with_router),
        out_shape=(jax.ShapeDtypeStruct((t, d), F32), jax.ShapeDtypeStruct((t, d), f_dtype),
                   jax.ShapeDtypeStruct((t, TOP_K), jnp.int32), jax.ShapeDtypeStruct((t, TOP_K), F32)),
        grid=(t // tm,),
        in_specs=[pl.BlockSpec((tm, d), row),
                  pl.BlockSpec((tm, BRANCH_W), row), pl.BlockSpec((tm, BRANCH_W), row),
                  pl.BlockSpec((tm, BRANCH_W), row),
                  pl.BlockSpec((tm, N_BRANCH * d), row),
                  pl.BlockSpec((1, 6, d), mod_map),
                  pl.BlockSpec(gains.shape, full2),
                  pl.BlockSpec(w_branch.shape, lambda i: (0, 0, 0), pipeline_mode=pl.Buffered(1)),
                  pl.BlockSpec(w_out.shape, full2, pipeline_mode=pl.Buffered(1)),
                  pl.BlockSpec(w_router.shape, full2)],
        out_specs=(pl.BlockSpec((tm, d), row), pl.BlockSpec((tm, d), row),
                   pl.BlockSpec((tm, TOP_K), row), pl.BlockSpec((tm, TOP_K), row)),
        compiler_params=_cparams(("parallel",)),
        name="merge_ctx" if is_ctx else "merge_lat",
    )(h, y_ret, y_win, y_na, gates, mod, gains, w_branch, w_out, w_router)


def _swiglu(x, w1, w3, w2, act_ref):
    dff = act_ref.shape[1]
    for c0 in range(0, dff, FF_CHUNK):
        cw = min(FF_CHUNK, dff - c0)
        a1 = _dot(x, w1[:, c0:c0 + cw])
        a3 = _dot(x, w3[:, c0:c0 + cw])
        act_ref[:, c0:c0 + cw] = (a1 * _sigmoid(a1) * a3).astype(BF16)
    return _dot(act_ref[...], w2[...])


def _ffn_kernel(h_ref, f_ref, mod_ref, g_ref, w1_ref, w3_ref, w2_ref, o_ref, act_ref):
    y = _swiglu(f_ref[...], w1_ref, w3_ref, w2_ref, act_ref)
    o_ref[...] = h_ref[...] + mod_ref[0, 5:6, :] * _rms(y, g_ref[3:4, :])


def _ffn(h, f, mod, gains, w1, w3, w2, *, is_ctx, per_batch, tm=512):
    t, d = h.shape
    dff = w1.shape[1]
    if is_ctx:
        mod_map = lambda i: (mod.shape[0] - 1, 0, 0)
    else:
        mod_map = lambda i: (i // (per_batch // tm), 0, 0)
    row = lambda i: (i, 0)
    const = lambda shape: pl.BlockSpec(shape, lambda i: (0, 0), pipeline_mode=pl.Buffered(1))
    return pl.pallas_call(
        _ffn_kernel,
        out_shape=jax.ShapeDtypeStruct((t, d), F32),
        grid=(t // tm,),
        in_specs=[pl.BlockSpec((tm, d), row), pl.BlockSpec((tm, d), row),
                  pl.BlockSpec((1, 6, d), mod_map),
                  pl.BlockSpec(gains.shape, lambda i: (0, 0)),
                  const((d, dff)), const((d, dff)), const((dff, d))],
        out_specs=pl.BlockSpec((tm, d), row),
        scratch_shapes=[pltpu.VMEM((tm, dff), BF16)],
        compiler_params=_cparams(("parallel",)),
        name="ffn_ctx" if is_ctx else "ffn_lat",
    )(h, f, mod, gains, w1, w3, w2)


def _route_slots(idx, tile):
    n = idx.shape[0] * TOP_K
    e = idx.reshape(n)
    onehot = (e[:, None] == jnp.arange(N_EXPERTS, dtype=jnp.int32)).astype(jnp.int32)
    csum = jnp.cumsum(onehot, axis=0)
    counts = csum[-1]
    rank = jnp.sum(csum * onehot, axis=1) - 1
    padded = (counts + tile - 1) // tile * tile
    ends = jnp.cumsum(padded)
    starts = ends - padded
    pos = jnp.sum(onehot * starts[None, :], axis=1) + rank
    n_tiles = n // tile + N_EXPERTS
    tile_start = jnp.arange(n_tiles, dtype=jnp.int32) * tile
    tile_expert = jnp.minimum(jnp.sum((tile_start[:, None] >= ends[None, :]).astype(jnp.int32), axis=1),
                              N_EXPERTS - 1)
    n_active = (ends[-1] // tile).reshape(1)
    return pos.astype(jnp.int32), tile_expert.astype(jnp.int32), n_active.astype(jnp.int32)


def _dispatch_kernel(pos_ref, f_ref, xs_in_ref, xs_ref, sem):
    del xs_in_ref
    tm = f_ref.shape[0]

    def row_copy(r, k):
        p = pos_ref[0, 0, TOP_K * r + k]
        return pltpu.make_async_copy(f_ref.at[pl.ds(r, 1), :], xs_ref.at[pl.ds(p, 1), :], sem)

    def start(r, carry):
        for k in range(TOP_K):
            row_copy(r, k).start()
        return carry

    def wait(r, carry):
        for k in range(TOP_K):
            row_copy(r, k).wait()
        return carry

    lax.fori_loop(0, tm, start, 0, unroll=8)
    lax.fori_loop(0, tm, wait, 0, unroll=8)


def _dispatch(f, pos, xs, tm=256):
    t, d = f.shape
    pos3 = pos.reshape(t // tm, 1, TOP_K * tm)
    return pl.pallas_call(
        _dispatch_kernel,
        out_shape=jax.ShapeDtypeStruct(xs.shape, xs.dtype),
        grid=(t // tm,),
        in_specs=[pl.BlockSpec((1, 1, TOP_K * tm), lambda i: (i, 0, 0), memory_space=pltpu.SMEM),
                  pl.BlockSpec((tm, d), lambda i: (i, 0)),
                  pl.BlockSpec(memory_space=pl.ANY)],
        out_specs=pl.BlockSpec(memory_space=pl.ANY),
        scratch_shapes=[pltpu.SemaphoreType.DMA(())],
        input_output_aliases={2: 0},
        compiler_params=_cparams(("arbitrary",)),
        name="moe_dispatch",
    )(pos3, f, xs)


def _experts_kernel(te_ref, na_ref, x_ref, *refs, has_prev):
    del te_ref
    if has_prev:
        yprev_ref, w1_ref, w3_ref, w2_ref, y_ref, act_ref = refs
    else:
        w1_ref, w3_ref, w2_ref, y_ref, act_ref = refs
    active = pl.program_id(0) < na_ref[0]

    @pl.when(active)
    def _():
        y = _swiglu(x_ref[...].astype(BF16), w1_ref.at[0], w3_ref.at[0], w2_ref.at[0], act_ref)
        y_ref[...] = yprev_ref[...] + y if has_prev else y

    @pl.when(jnp.logical_not(active))
    def _():
        y_ref[...] = jnp.zeros_like(y_ref)


def _experts_pass(xs, y_prev, tile_expert, n_active, w1, w3, w2, *, tile, fc, c):
    p, d = xs.shape
    slot = pl.BlockSpec((tile, d), lambda j, te, na: (j, 0))
    in_specs = [slot] + ([slot] if y_prev is not None else []) + [
        pl.BlockSpec((1, d, fc), lambda j, te, na: (te[j], 0, c)),
        pl.BlockSpec((1, d, fc), lambda j, te, na: (te[j], 0, c)),
        pl.BlockSpec((1, fc, d), lambda j, te, na: (te[j], c, 0))]
    grid_spec = pltpu.PrefetchScalarGridSpec(
        num_scalar_prefetch=2, grid=(p // tile,), in_specs=in_specs, out_specs=slot,
        scratch_shapes=[pltpu.VMEM((tile, fc), BF16)])
    args = (xs,) + ((y_prev,) if y_prev is not None else ()) + (w1, w3, w2)
    return pl.pallas_call(
        functools.partial(_experts_kernel, has_prev=y_prev is not None),
        out_shape=jax.ShapeDtypeStruct((p, d), F32),
        grid_spec=grid_spec,
        compiler_params=_cparams(("arbitrary",)),
        name="moe_experts",
    )(tile_expert, n_active, *args)


def _experts(xs, tile_expert, n_active, w1, w3, w2, *, tile, fc):
    y = None
    for c in range(w1.shape[2] // fc):
        y = _experts_pass(xs, y, tile_expert, n_active, w1, w3, w2, tile=tile, fc=fc, c=c)
    return y


def _combine_kernel(pos_ref, gw_ref, h_ref, mod_ref, g_ref, y_any, o_ref, buf_ref, sem):
    tm = h_ref.shape[0]

    def row_copy(r, k):
        p = pos_ref[0, 0, TOP_K * r + k]
        return pltpu.make_async_copy(y_any.at[pl.ds(p, 1), :], buf_ref.at[k, pl.ds(r, 1), :], sem)

    def start(r, carry):
        for k in range(TOP_K):
            row_copy(r, k).start()
        return carry

    def wait(r, carry):
        for k in range(TOP_K):
            row_copy(r, k).wait()
        return carry

    lax.fori_loop(0, tm, start, 0, unroll=8)
    lax.fori_loop(0, tm, wait, 0, unroll=8)
    gw = gw_ref[...]
    y = gw[:, 0:1] * buf_ref[0] + gw[:, 1:2] * buf_ref[1]
    o_ref[...] = h_ref[...] + mod_ref[0, 5:6, :] * _rms(y, g_ref[3:4, :])


def _combine(h, pos, gw, y_sorted, mod, gains, *, is_ctx, per_batch, tm=256):
    t, d = h.shape
    pos3 = pos.reshape(t // tm, 1, TOP_K * tm)
    if is_ctx:
        mod_map = lambda i: (mod.shape[0] - 1, 0, 0)
    else:
        mod_map = lambda i: (i // (per_batch // tm), 0, 0)
    row = lambda i: (i, 0)
    return pl.pallas_call(
        _combine_kernel,
        out_shape=jax.ShapeDtypeStruct((t, d), F32),
        grid=(t // tm,),
        in_specs=[pl.BlockSpec((1, 1, TOP_K * tm), lambda i: (i, 0, 0), memory_space=pltpu.SMEM),
                  pl.BlockSpec((tm, TOP_K), row),
                  pl.BlockSpec((tm, d), row),
                  pl.BlockSpec((1, 6, d), mod_map),
                  pl.BlockSpec(gains.shape, lambda i: (0, 0)),
                  pl.BlockSpec(memory_space=pl.ANY)],
        out_specs=pl.BlockSpec((tm, d), row),
        scratch_shapes=[pltpu.VMEM((TOP_K, tm, d), F32), pltpu.SemaphoreType.DMA(())],
        compiler_params=_cparams(("arbitrary",)),
        name="moe_combine_ctx" if is_ctx else "moe_combine_lat",
    )(pos3, gw, h, mod, gains, y_sorted)


def _rope_tables(s, lc):
    def lin(pos):
        n_freq = RET_DK // 2
        inv = ROPE_BASE ** (-jnp.arange(n_freq, dtype=F32) / n_freq)
        ang = pos.astype(F32)[:, None] * inv
        cos, sin = jnp.cos(ang), jnp.sin(ang)
        return jnp.concatenate([cos, cos], -1), jnp.concatenate([-sin, sin], -1)

    t = jnp.arange(s)
    n_freq = WIN_HD // 4
    inv = ROPE_BASE ** (-jnp.arange(n_freq, dtype=F32) / n_freq)
    rowa = (t // GRID_W).astype(F32)[:, None] * inv
    cola = (t % GRID_W).astype(F32)[:, None] * inv
    ang = jnp.concatenate([rowa, cola], axis=-1)
    cos, sin = jnp.cos(ang), jnp.sin(ang)
    ca = jnp.tile(jnp.concatenate([cos, cos], -1), (1, 2))
    sa = jnp.tile(jnp.concatenate([-sin, sin], -1), (1, 2))
    cr_l, sr_l = lin(lc + jnp.arange(s))
    cr_c, sr_c = lin(jnp.arange(lc))
    return (cr_l, sr_l, ca, sa), (cr_c, sr_c, ca[:lc], sa[:lc])


def kernel(x, c, ctx, c_ctx, w_mod, b_mod, norm_gains, w_in, ret_decay, win_sink, na_rpb, w_branch, w_out,
           ffn_w1, ffn_w3, ffn_w2, moe_router, moe_w1, moe_w3, moe_w2):
    batch, s, d = x.shape
    lc = ctx.shape[1]
    depth = w_mod.shape[0]

    c_all = jnp.zeros((16, d), F32).at[:batch].set(c).at[batch].set(c_ctx)
    mods = _modulation(c_all, w_mod, b_mod)[:, :batch + 1].reshape(depth, batch + 1, 6, d)

    tabs_l, tabs_c = _rope_tables(s, lc)
    log_gamma = jnp.log1p(-jnp.exp(ret_decay.astype(F32)))
    na_bias = _na_bias_tables(na_rpb, s)

    h_l = x.reshape(batch * s, d)
    h_c = ctx.reshape(batch * lc, d)
    zero_router = jnp.zeros((d, N_EXPERTS), F32)

    for layer in range(depth):
        need_ctx = layer < depth - 1
        mod = mods[layer]
        gains = norm_gains[layer]
        w_in_b = w_in[layer].astype(BF16)
        wb_b = w_branch[layer].astype(BF16)
        wo_b = w_out[layer].astype(BF16)
        i = layer // 2
        is_moe = layer % 2 == 1
        if is_moe:
            w1, w3, w2 = moe_w1[i].astype(BF16), moe_w3[i].astype(BF16), moe_w2[i].astype(BF16)
            w_router = moe_router[i]
        else:
            w1, w3, w2 = ffn_w1[i].astype(BF16), ffn_w3[i].astype(BF16), ffn_w2[i].astype(BF16)
            w_router = zero_router

        p_ret_l, p_win_l, p_na_l, p_gt_l = _inproj(h_l, mod, gains, w_in_b, *tabs_l, is_ctx=False, tm=512)
        p_ret_c, p_win_c, p_na_c, p_gt_c = _inproj(h_c, mod, gains, w_in_b, *tabs_c, is_ctx=True, tm=256)

        y_ret_l, y_ret_c = _retention(log_gamma[layer], p_ret_l, p_ret_c, batch)
        y_win_l = _window(win_sink[layer], p_win_l, p_win_c, batch)
        y_na_l = _neighbourhood(p_na_l, p_na_c, na_bias, layer, batch)

        h_l, f_l, idx_l, gw_l = _merge(h_l, y_ret_l, y_win_l, y_na_l, p_gt_l, mod, gains, wb_b, wo_b,
                                       w_router, with_router=is_moe, is_ctx=False, per_batch=s, tm=512)
        if need_ctx:
            y_win_c, y_na_c = _ctx_attention(win_sink[layer], p_win_c, p_na_c, batch)
            h_c, f_c, idx_c, gw_c = _merge(h_c, y_ret_c, y_win_c, y_na_c, p_gt_c, mod, gains, wb_b, wo_b,
                                           w_router, with_router=is_moe, is_ctx=True, per_batch=lc, tm=512)

        if not is_moe:
            h_l = _ffn(h_l, f_l, mod, gains, w1, w3, w2, is_ctx=False, per_batch=s)
            if need_ctx:
                h_c = _ffn(h_c, f_c, mod, gains, w1, w3, w2, is_ctx=True, per_batch=lc)
            continue

        idx_all = jnp.concatenate([idx_l, idx_c], axis=0) if need_ctx else idx_l
        pos, tile_expert, n_active = _route_slots(idx_all, MOE_TILE)
        n_slots = tile_expert.shape[0] * MOE_TILE
        xs = jnp.zeros((n_slots, d), F32)
        n_l = TOP_K * h_l.shape[0]
        xs = _dispatch(f_l, pos[:n_l], xs)
        if need_ctx:
            xs = _dispatch(f_c, pos[n_l:], xs)
        y_sorted = _experts(xs, tile_expert, n_active, w1, w3, w2, tile=MOE_TILE, fc=MOE_FC)
        h_l = _combine(h_l, pos[:n_l], gw_l, y_sorted, mod, gains, is_ctx=False, per_batch=s)
        if need_ctx:
            h_c = _combine(h_c, pos[n_l:], gw_c, y_sorted, mod, gains, is_ctx=True, per_batch=lc)
    return h_l.reshape(batch, s, d)
```

```python
import functools
import math

import jax
import jax.numpy as jnp
import numpy as np
from jax import lax
from jax.experimental import pallas as pl
from jax.experimental.pallas import tpu as pltpu

F32 = jnp.float32
BF16 = jnp.bfloat16

D_MODEL = 1024
GRID_W = 64
RET_HEADS = 4
RET_DK = 128
WIN_HEADS = 8
WIN_KV_HEADS = 2
WIN_HD = 64
WINDOW = 128
WIN_BLOCK = 128
NA_HEADS = 8
NA_HD = 64
NA_ROWS_MAX = 8
NA_COLS = 16
BRANCH_W = 512
N_BRANCH = 3
N_EXPERTS = 8
TOP_K = 2
ROPE_BASE = 10000.0
NORM_EPS = 1e-6
NEG_INF = -1e30
LOG2E = math.log2(math.e)

LANES = 128
VMEM_LIMIT = 56 * 1024 * 1024

C_RQ, C_RK, C_RV, C_RG = 0, 512, 1024, 1536
C_WQ, C_WK, C_WV = 2048, 2560, 2688
C_NQ, C_NK, C_NV = 2816, 3328, 3840
C_GATE = 4352
D_IN = 7424

RET_CHUNK = 256
NA_QROWS = 4
NA_KROWS = 12
FF_CHUNK = 256
MOE_TILE = 512
MOE_FC = 1792


def _cparams(sem):
    return pltpu.CompilerParams(dimension_semantics=sem, vmem_limit_bytes=VMEM_LIMIT)


def _rms(x, g):
    return x * lax.rsqrt(jnp.mean(x * x, axis=-1, keepdims=True) + NORM_EPS) * g


def _sigmoid(x):
    return 0.5 * jnp.tanh(0.5 * x) + 0.5


def _dot(a, b):
    return jnp.dot(a, b, preferred_element_type=F32)


def _dot_nt(a, b):
    return lax.dot_general(a, b, (((1,), (1,)), ((), ())), preferred_element_type=F32)


def _dot_tn(a, b):
    return lax.dot_general(a, b, (((0,), (0,)), ((), ())), preferred_element_type=F32)


def _mod_kernel(c_ref, w_ref, b_ref, o_ref):
    c = c_ref[...]
    s = c * jax.nn.sigmoid(c)
    o_ref[0] = jnp.dot(s, w_ref[0], preferred_element_type=F32,
                       precision=lax.Precision.HIGHEST) + b_ref[0]


def _modulation(c_all, w_mod, b_mod):
    depth, d, n = w_mod.shape
    tn = 1536
    return pl.pallas_call(
        _mod_kernel,
        out_shape=jax.ShapeDtypeStruct((depth, c_all.shape[0], n), F32),
        grid=(depth, n // tn),
        in_specs=[pl.BlockSpec(c_all.shape, lambda l, j: (0, 0)),
                  pl.BlockSpec((1, d, tn), lambda l, j: (l, 0, j)),
                  pl.BlockSpec((1, 1, tn), lambda l, j: (l, 0, j))],
        out_specs=pl.BlockSpec((1, c_all.shape[0], tn), lambda l, j: (l, 0, j)),
        compiler_params=_cparams(("arbitrary", "arbitrary")),
        name="modulation",
    )(c_all, w_mod, b_mod.reshape(depth, 1, n))


def _inproj_kernel(h_ref, mod_ref, g_ref, w_ref, cr_ref, sr_ref, ca_ref, sa_ref,
                   oret_ref, owin_ref, ona_ref, og_ref, *, rope_win):
    x = h_ref[...]
    a = _rms(x, g_ref[0:1, :]) * (1.0 + mod_ref[0, 1:2, :]) + mod_ref[0, 0:1, :]
    a = a.astype(BF16)
    tm = x.shape[0]
    lane = lax.broadcasted_iota(jnp.int32, (tm, LANES), 1)
    even = lane < 64
    first_half = (lane % 64) < 32

    def mm(c0, cw):
        return _dot(a, w_ref[:, c0:c0 + cw])

    def rope_lin(v):
        return v * cr_ref[...] + pltpu.roll(v, 64, 1) * sr_ref[...]

    def rope_ax(v):
        rot = jnp.where(first_half, pltpu.roll(v, 96, 1), pltpu.roll(v, 32, 1))
        return v * ca_ref[...] + rot * sa_ref[...]

    r = mm(C_RQ, 512)
    for h in range(RET_HEADS):
        blk = rope_lin(r[:, h * 128:(h + 1) * 128]) * (RET_DK ** -0.5)
        oret_ref[:, h * 128:(h + 1) * 128] = blk.astype(BF16)
    r = mm(C_RK, 512)
    for h in range(RET_HEADS):
        blk = rope_lin(r[:, h * 128:(h + 1) * 128])
        oret_ref[:, 512 + h * 128:512 + (h + 1) * 128] = blk.astype(BF16)
    oret_ref[:, 1024:1536] = mm(C_RV, 512).astype(BF16)
    oret_ref[:, 1536:2048] = mm(C_RG, 512).astype(BF16)

    r = mm(C_WQ, 512)
    for p in range(WIN_HEADS // 2):
        blk = r[:, p * 128:(p + 1) * 128]
        if rope_win:
            blk = rope_ax(blk)
        blk = blk * (WIN_HD ** -0.5 * LOG2E)
        owin_ref[:, (2 * p) * 128:(2 * p + 1) * 128] = jnp.where(even, blk, 0.0).astype(BF16)
        owin_ref[:, (2 * p + 1) * 128:(2 * p + 2) * 128] = jnp.where(even, 0.0, blk).astype(BF16)
    r = mm(C_WK, 256)
    k2 = r[:, 0:128]
    if rope_win:
        k2 = rope_ax(k2)
    v2 = r[:, 128:256]
    for i, t in enumerate((k2, v2)):
        sw = pltpu.roll(t, 64, 1)
        owin_ref[:, 1024 + 256 * i:1024 + 256 * i + 128] = jnp.where(even, t, sw).astype(BF16)
        owin_ref[:, 1024 + 256 * i + 128:1024 + 256 * i + 256] = jnp.where(even, sw, t).astype(BF16)

    r = mm(C_NQ, 512) * (NA_HD ** -0.5 * LOG2E)
    for p in range(NA_HEADS // 2):
        blk = r[:, p * 128:(p + 1) * 128]
        ona_ref[:, (2 * p) * 128:(2 * p + 1) * 128] = jnp.where(even, blk, 0.0).astype(BF16)
        ona_ref[:, (2 * p + 1) * 128:(2 * p + 2) * 128] = jnp.where(even, 0.0, blk).astype(BF16)
    ona_ref[:, 1024:1536] = mm(C_NK, 512).astype(BF16)
    ona_ref[:, 1536:2048] = mm(C_NV, 512).astype(BF16)

    for j in range(6):
        og_ref[:, j * 512:(j + 1) * 512] = mm(C_GATE + j * 512, 512).astype(BF16)


def _inproj(h, mod, gains, w_in, cr, sr, ca, sa, *, is_ctx, tm=256):
    t, d = h.shape
    nt = t // tm
    per_batch = cr.shape[0] // tm
    if is_ctx:
        mod_map = lambda i: (mod.shape[0] - 1, 0, 0)
    else:
        mod_map = lambda i: (i // per_batch, 0, 0)
    pos_map = lambda i: (i % per_batch, 0)
    full = lambda i: (0, 0)
    row = lambda i: (i, 0)
    outs = (jax.ShapeDtypeStruct((t, 2048), BF16), jax.ShapeDtypeStruct((t, 1536), BF16),
            jax.ShapeDtypeStruct((t, 2048), BF16), jax.ShapeDtypeStruct((t, 3072), BF16))
    return pl.pallas_call(
        functools.partial(_inproj_kernel, rope_win=not is_ctx),
        out_shape=outs,
        grid=(nt,),
        in_specs=[pl.BlockSpec((tm, d), row),
                  pl.BlockSpec((1, 6, d), mod_map),
                  pl.BlockSpec(gains.shape, full),
                  pl.BlockSpec(w_in.shape, full, pipeline_mode=pl.Buffered(1)),
                  pl.BlockSpec((tm, LANES), pos_map), pl.BlockSpec((tm, LANES), pos_map),
                  pl.BlockSpec((tm, LANES), pos_map), pl.BlockSpec((tm, LANES), pos_map)],
        out_specs=(pl.BlockSpec((tm, 2048), row), pl.BlockSpec((tm, 1536), row),
                   pl.BlockSpec((tm, 2048), row), pl.BlockSpec((tm, 3072), row)),
        compiler_params=_cparams(("parallel",)),
        name="inproj_ctx" if is_ctx else "inproj_lat",
    )(h, mod, gains, w_in, cr, sr, ca, sa)


def _ret_kernel(lg_ref, ql, kl, vl, gl, qc, kc, vc, gc, yl_ref, yc_ref, acc_ref):
    C = RET_CHUNK
    h = pl.program_id(1)
    lgf = lg_ref[0, h]
    lgb = lg_ref[1, h]
    ii = lax.broadcasted_iota(jnp.int32, (C, C), 0).astype(F32)
    jj = lax.broadcasted_iota(jnp.int32, (C, C), 1).astype(F32)
    diff = ii - jj
    dmat = jnp.where(diff >= 0, jnp.exp(lgf * jnp.maximum(diff, 0.0)),
                     jnp.exp(lgb * jnp.maximum(-diff, 0.0)))
    r = lax.broadcasted_iota(jnp.int32, (C, RET_DK), 0).astype(F32)
    qdf = jnp.exp(lgf * (r + 1.0))
    kdf = jnp.exp(lgf * (C - 1.0 - r))
    qdb = jnp.exp(lgb * (C - r))
    kdb = jnp.exp(lgb * r)
    ones = jnp.ones((RET_DK, RET_DK), F32)
    cdf = jnp.exp(ones * (lgf * C))
    cdb = jnp.exp(ones * (lgb * C))

    n_lat = ql.shape[0] // C
    lc = qc.shape[0]
    chunks = [(qc, kc, vc, gc, yc_ref, 0, 0)]
    chunks += [(ql, kl, vl, gl, yl_ref, j * C, lc + j * C) for j in range(n_lat)]

    state = jnp.zeros((RET_DK, RET_DK), F32)
    for (qr, kr, vr, _, _, off, aoff) in chunks:
        q = qr[off:off + C, :]
        k = kr[off:off + C, :]
        v = vr[off:off + C, :]
        s = _dot_nt(q, k) * dmat
        y = _dot(s.astype(BF16), v)
        y = y + _dot((q.astype(F32) * qdf).astype(BF16), state.astype(BF16))
        state = state * cdf + _dot_tn((k.astype(F32) * kdf).astype(BF16), v)
        acc_ref[aoff:aoff + C, :] = y

    state = jnp.zeros((RET_DK, RET_DK), F32)
    for (qr, kr, vr, gr, yr, off, aoff) in [chunks[0]] + chunks[:0:-1]:
        q = qr[off:off + C, :]
        k = kr[off:off + C, :]
        v = vr[off:off + C, :]
        y = acc_ref[aoff:aoff + C, :] + _dot((q.astype(F32) * qdb).astype(BF16), state.astype(BF16))
        state = state * cdb + _dot_tn((k.astype(F32) * kdb).astype(BF16), v)
        mu = jnp.mean(y, axis=-1, keepdims=True)
        dlt = y - mu
        var = jnp.mean(dlt * dlt, axis=-1, keepdims=True)
        yn = dlt * lax.rsqrt(var + NORM_EPS)
        g = gr[off:off + C, :].astype(F32)
        yr[off:off + C, :] = (g * _sigmoid(g) * yn).astype(BF16)


def _retention(lg, pl_ret, pc_ret, batch):
    tl = pl_ret.shape[0]
    tc = pc_ret.shape[0]
    s = tl // batch
    lc = tc // batch
    assert lc == RET_CHUNK and s % RET_CHUNK == 0
    lat = lambda off: pl.BlockSpec((s, 128), lambda b, h, off=off: (b, off + h))
    ctx = lambda off: pl.BlockSpec((lc, 128), lambda b, h, off=off: (b, off + h))
    return pl.pallas_call(
        _ret_kernel,
        out_shape=(jax.ShapeDtypeStruct((tl, 512), BF16), jax.ShapeDtypeStruct((tc, 512), BF16)),
        grid=(batch, RET_HEADS),
        in_specs=[pl.BlockSpec(memory_space=pltpu.SMEM),
                  lat(0), lat(4), lat(8), lat(12), ctx(0), ctx(4), ctx(8), ctx(12)],
        out_specs=(pl.BlockSpec((s, 128), lambda b, h: (b, h)),
                   pl.BlockSpec((lc, 128), lambda b, h: (b, h))),
        scratch_shapes=[pltpu.VMEM((s + lc, 128), F32)],
        compiler_params=_cparams(("parallel", "parallel")),
        name="retention",
    )(lg, pl_ret, pl_ret, pl_ret, pl_ret, pc_ret, pc_ret, pc_ret, pc_ret)


def _softmax_parts(parts, sink=None):
    m = functools.reduce(jnp.maximum, [jnp.max(p, axis=-1, keepdims=True) for p in parts])
    if sink is not None:
        m = jnp.maximum(m, sink)
    es = [jnp.exp2(p - m) for p in parts]
    den = functools.reduce(lambda a, b: a + b, [jnp.sum(e, axis=-1, keepdims=True) for e in es])
    if sink is not None:
        den = den + jnp.exp2(sink - m)
    return [e.astype(BF16) for e in es], 1.0 / den


def _win_kernel(sink_ref, q_ref, k0, k1, v0, v1, ck0, ck1, cv0, cv1, o_ref):
    blk = WIN_BLOCK
    nw = 3 * blk
    s_len = k0.shape[0]
    even = lax.broadcasted_iota(jnp.int32, (blk, LANES), 1) < 64
    g = WIN_HEADS // WIN_KV_HEADS
    for jj in range(q_ref.shape[0] // blk):
        rows = slice(jj * blk, (jj + 1) * blk)
        j = pl.program_id(1) * (q_ref.shape[0] // blk) + jj
        start = pl.multiple_of(jnp.clip((j - 1) * blk, 0, s_len - nw), blk)
        qpos = j * blk + lax.broadcasted_iota(jnp.int32, (blk, nw), 0)
        kpos = start + lax.broadcasted_iota(jnp.int32, (blk, nw), 1)
        bias = jnp.where(jnp.abs(qpos - kpos) <= WINDOW, 0.0, NEG_INF)
        for kv, (kr, vr, ckr, cvr) in enumerate(((k0, v0, ck0, cv0), (k1, v1, ck1, cv1))):
            kcat = jnp.concatenate([kr[pl.ds(start, nw), :], ckr[...]], axis=0)
            vcat = jnp.concatenate([vr[pl.ds(start, nw), :], cvr[...]], axis=0)
            qst = jnp.concatenate(
                [q_ref[rows, (g * kv + i) * 128:(g * kv + i + 1) * 128] for i in range(g)], axis=0)
            s = _dot_nt(qst, kcat)
            ps, invs = [], []
            for i in range(g):
                sb = s[i * blk:(i + 1) * blk]
                (pw, pc), inv = _softmax_parts([sb[:, :nw] + bias, sb[:, nw:]],
                                               sink_ref[g * kv + i] * LOG2E)
                ps.append(jnp.concatenate([pw, pc], axis=1))
                invs.append(inv)
            pv = _dot(jnp.concatenate(ps, axis=0), vcat)
            for p in range(g // 2):
                o_ref[rows, (2 * kv + p) * 128:(2 * kv + p + 1) * 128] = jnp.where(
                    even, pv[(2 * p) * blk:(2 * p + 1) * blk] * invs[2 * p],
                    pv[(2 * p + 1) * blk:(2 * p + 2) * blk] * invs[2 * p + 1]).astype(BF16)


def _window(sink, pl_win, pc_win, batch):
    tl = pl_win.shape[0]
    tc = pc_win.shape[0]
    s = tl // batch
    lc = tc // batch
    tq = 2 * WIN_BLOCK
    nb = s // tq
    lat = lambda cb: pl.BlockSpec((s, 128), lambda b, j, cb=cb: (b, cb))
    ctx = lambda cb: pl.BlockSpec((lc, 128), lambda b, j, cb=cb: (b, cb))
    return pl.pallas_call(
        _win_kernel,
        out_shape=jax.ShapeDtypeStruct((tl, 512), BF16),
        grid=(batch, nb),
        in_specs=[pl.BlockSpec(memory_space=pltpu.SMEM),
                  pl.BlockSpec((tq, 1024), lambda b, j: (b * nb + j, 0)),
                  lat(8), lat(9), lat(10), lat(11), ctx(8), ctx(9), ctx(10), ctx(11)],
        out_specs=pl.BlockSpec((tq, 512), lambda b, j: (b * nb + j, 0)),
        compiler_params=_cparams(("parallel", "arbitrary")),
        name="window_attn",
    )(sink, pl_win, pl_win, pl_win, pl_win, pl_win, pc_win, pc_win, pc_win, pc_win)


def _na_kernel(ids_ref, q_ref, ka, kb, kc, va, vb, vc, ck, cv, pb_ref, o_ref):
    nq = q_ref.shape[0]
    n_nb = ka.shape[0] * 3
    even = lax.broadcasted_iota(jnp.int32, (nq, LANES), 1) < 64
    i = pl.program_id(0)
    pat = jnp.where(i == 0, 0, jnp.where(i == pl.num_programs(0) - 1, 2, 1))

    def bias(h):
        return jnp.concatenate(
            [jnp.concatenate([pb_ref[0, h, ids_ref[pat, qr, kp]] for kp in range(NA_KROWS // 2)], axis=1)
             for qr in range(NA_QROWS)], axis=0)

    for p in range(NA_HEADS // 2):
        lanes = slice(p * 128, (p + 1) * 128)
        qst = jnp.concatenate([q_ref[:, (2 * p) * 128:(2 * p + 1) * 128],
                               q_ref[:, (2 * p + 1) * 128:(2 * p + 2) * 128]], axis=0)
        kcat = jnp.concatenate([ka[:, lanes], kb[:, lanes], kc[:, lanes], ck[:, lanes]], axis=0)
        vcat = jnp.concatenate([va[:, lanes], vb[:, lanes], vc[:, lanes], cv[:, lanes]], axis=0)
        s = _dot_nt(qst, kcat)
        ps, invs = [], []
        for par in range(2):
            sb = s[par * nq:(par + 1) * nq]
            (pn, pc), inv = _softmax_parts([sb[:, :n_nb] + bias(2 * p + par), sb[:, n_nb:]])
            ps.append(jnp.concatenate([pn, pc], axis=1))
            invs.append(inv)
        pv = _dot(jnp.concatenate(ps, axis=0), vcat)
        o_ref[:, lanes] = jnp.where(even, pv[:nq] * invs[0], pv[nq:] * invs[1]).astype(BF16)


def _neighbourhood(pl_na, pc_na, bias_blocks, bias_ids, layer, batch):
    tl = pl_na.shape[0]
    tc = pc_na.shape[0]
    s = tl // batch
    lc = tc // batch
    nq = NA_QROWS * GRID_W
    ng = s // nq
    kblk = nq
    nkb = s // kblk
    assert NA_KROWS * GRID_W == 3 * kblk

    def kspec(cb, d):
        return pl.BlockSpec(
            (kblk, 512), lambda i, b, d=d, cb=cb: (b * nkb + jnp.clip(i - 1, 0, nkb - 3) + d, cb))

    return pl.pallas_call(
        _na_kernel,
        out_shape=jax.ShapeDtypeStruct((tl, 512), BF16),
        grid=(ng, batch),
        in_specs=[pl.BlockSpec(memory_space=pltpu.SMEM),
                  pl.BlockSpec((nq, 1024), lambda i, b: (b * ng + i, 0)),
                  kspec(2, 0), kspec(2, 1), kspec(2, 2),
                  kspec(3, 0), kspec(3, 1), kspec(3, 2),
                  pl.BlockSpec((lc, 512), lambda i, b: (b, 2)),
                  pl.BlockSpec((lc, 512), lambda i, b: (b, 3)),
                  pl.BlockSpec((1,) + bias_blocks.shape[1:], lambda i, b: (layer, 0, 0, 0, 0),
                               pipeline_mode=pl.Buffered(1))],
        out_specs=pl.BlockSpec((nq, 512), lambda i, b: (b * ng + i, 0)),
        compiler_params=_cparams(("parallel", "arbitrary")),
        name="neighbourhood_attn",
    )(bias_ids, pl_na, pl_na, pl_na, pl_na, pl_na, pl_na, pl_na, pc_na, pc_na, bias_blocks)


def _na_pair_plan(s):
    rows = s // GRID_W
    kr = min(NA_ROWS_MAX, rows)
    ng = rows // NA_QROWS
    pairs = {}
    ids = np.zeros((3, NA_QROWS, NA_KROWS // 2), np.int32)
    for pi, i in enumerate((0, 1, ng - 1)):
        k0 = NA_QROWS * int(np.clip(i - 1, 0, ng - 3))
        for qr in range(NA_QROWS):
            r = NA_QROWS * i + qr
            r_start = int(np.clip(r - kr // 2, 0, rows - kr))
            for kp in range(NA_KROWS // 2):
                offs = tuple(
                    krow - r + NA_ROWS_MAX - 1 if r_start <= krow < r_start + kr else -1
                    for krow in (k0 + 2 * kp, k0 + 2 * kp + 1))
                ids[pi, qr, kp] = pairs.setdefault(offs, len(pairs))
    return list(pairs), ids


def _na_bias_pairs(rpb, s):
    pair_list, ids = _na_pair_plan(s)
    cq = np.arange(GRID_W)
    ck = np.arange(GRID_W)
    c_start = np.clip(cq - NA_COLS // 2, 0, GRID_W - NA_COLS)
    col_ok = (ck[None, :] >= c_start[:, None]) & (ck[None, :] < c_start[:, None] + NA_COLS)
    dc = np.clip(ck[None, :] - cq[:, None] + (NA_COLS - 1), 0, 2 * NA_COLS - 2)
    sel_c = (dc[:, :, None] == np.arange(2 * NA_COLS - 1)).astype(np.float32)
    t = jnp.einsum("lhrc,xyc->lhrxy", rpb.astype(F32), sel_c, precision=lax.Precision.HIGHEST)
    t = jnp.where(col_ok[None, None, None], t * LOG2E, NEG_INF)
    outside = jnp.full(t.shape[:2] + t.shape[3:], NEG_INF, F32)
    half = lambda off: outside if off < 0 else t[:, :, off]
    blocks = jnp.stack([jnp.concatenate([half(a), half(b)], axis=-1) for a, b in pair_list], axis=2)
    return blocks, jnp.asarray(ids)


def _ctx_attn_kernel(sink_ref, w_ref, n_ref, ow_ref, on_ref):
    lc = w_ref.shape[0]
    even = lax.broadcasted_iota(jnp.int32, (lc, LANES), 1) < 64
    g = WIN_HEADS // WIN_KV_HEADS
    for kv in range(WIN_KV_HEADS):
        k = w_ref[:, 1024 + kv * 128:1024 + (kv + 1) * 128]
        v = w_ref[:, 1280 + kv * 128:1280 + (kv + 1) * 128]
        qst = jnp.concatenate([w_ref[:, (g * kv + i) * 128:(g * kv + i + 1) * 128] for i in range(g)], axis=0)
        s = _dot_nt(qst, k)
        sm = [_softmax_parts([s[i * lc:(i + 1) * lc]], sink_ref[g * kv + i] * LOG2E) for i in range(g)]
        pv = _dot(jnp.concatenate([e[0][0] for e in sm], axis=0), v)
        for p in range(g // 2):
            ow_ref[:, (2 * kv + p) * 128:(2 * kv + p + 1) * 128] = jnp.where(
                even, pv[(2 * p) * lc:(2 * p + 1) * lc] * sm[2 * p][1],
                pv[(2 * p + 1) * lc:(2 * p + 2) * lc] * sm[2 * p + 1][1]).astype(BF16)
    for p in range(NA_HEADS // 2):
        k = n_ref[:, 1024 + p * 128:1024 + (p + 1) * 128]
        v = n_ref[:, 1536 + p * 128:1536 + (p + 1) * 128]
        qst = jnp.concatenate([n_ref[:, (2 * p) * 128:(2 * p + 1) * 128],
                               n_ref[:, (2 * p + 1) * 128:(2 * p + 2) * 128]], axis=0)
        s = _dot_nt(qst, k)
        sm = [_softmax_parts([s[i * lc:(i + 1) * lc]]) for i in range(2)]
        pv = _dot(jnp.concatenate([e[0][0] for e in sm], axis=0), v)
        on_ref[:, p * 128:(p + 1) * 128] = jnp.where(even, pv[:lc] * sm[0][1], pv[lc:] * sm[1][1]).astype(BF16)


def _ctx_attention(sink, pc_win, pc_na, batch):
    tc = pc_win.shape[0]
    lc = tc // batch
    row = lambda b: (b, 0)
    return pl.pallas_call(
        _ctx_attn_kernel,
        out_shape=(jax.ShapeDtypeStruct((tc, 512), BF16), jax.ShapeDtypeStruct((tc, 512), BF16)),
        grid=(batch,),
        in_specs=[pl.BlockSpec(memory_space=pltpu.SMEM),
                  pl.BlockSpec((lc, pc_win.shape[1]), row), pl.BlockSpec((lc, pc_na.shape[1]), row)],
        out_specs=(pl.BlockSpec((lc, 512), row), pl.BlockSpec((lc, 512), row)),
        compiler_params=_cparams(("parallel",)),
        name="context_attn",
    )(sink, pc_win, pc_na)


def _merge_kernel(h_ref, yr_ref, yw_ref, yn_ref, gt_ref, mod_ref, g_ref, wb_ref, wo_ref, wr_ref,
                  ho_ref, f_ref, idx_ref, gw_ref, *, with_router):
    z = None
    for r, y_ref in enumerate((yr_ref, yw_ref, yn_ref)):
        gate = _sigmoid(gt_ref[:, r * D_MODEL:(r + 1) * D_MODEL].astype(F32))
        t = gate * _dot(y_ref[...], wb_ref[r])
        z = t if z is None else z + t
    o = _dot(z.astype(BF16), wo_ref[...])
    h = h_ref[...] + mod_ref[0, 2:3, :] * _rms(o, g_ref[1:2, :])
    ho_ref[...] = h
    f = _rms(h, g_ref[2:3, :]) * (1.0 + mod_ref[0, 4:5, :]) + mod_ref[0, 3:4, :]
    f_ref[...] = f.astype(f_ref.dtype)
    if not with_router:
        idx_ref[...] = jnp.zeros_like(idx_ref)
        gw_ref[...] = jnp.zeros_like(gw_ref)
        return
    f_hi = f.astype(BF16)
    f_lo = (f - f_hi.astype(F32)).astype(BF16)
    wr = wr_ref[...]
    w_hi = wr.astype(BF16)
    w_lo = (wr - w_hi.astype(F32)).astype(BF16)
    logits = _dot(f_hi, w_hi) + (_dot(f_lo, w_hi) + _dot(f_hi, w_lo))
    idx = lax.broadcasted_iota(jnp.int32, logits.shape, 1)
    m1 = jnp.max(logits, axis=-1, keepdims=True)
    i1 = jnp.min(jnp.where(logits == m1, idx, N_EXPERTS), axis=-1, keepdims=True)
    rest = jnp.where(idx == i1, -jnp.inf, logits)
    m2 = jnp.max(rest, axis=-1, keepdims=True)
    i2 = jnp.min(jnp.where(rest == m2, idx, N_EXPERTS), axis=-1, keepdims=True)
    e2 = jnp.exp(m2 - m1)
    inv = 1.0 / (1.0 + e2)
    first = lax.broadcasted_iota(jnp.int32, idx_ref.shape, 1) == 0
    idx_ref[...] = jnp.where(first, i1, i2)
    gw_ref[...] = jnp.where(first, inv, e2 * inv)


def _merge(h, y_ret, y_win, y_na, gates, mod, gains, w_branch, w_out, w_router, *, with_router, is_ctx,
           per_batch, tm=256):
    t, d = h.shape
    f_dtype = F32 if with_router else BF16
    if is_ctx:
        mod_map = lambda i: (mod.shape[0] - 1, 0, 0)
    else:
        mod_map = lambda i: (i // (per_batch // tm), 0, 0)
    row = lambda i: (i, 0)
    full2 = lambda i: (0, 0)
    return pl.pallas_call(
        functools.partial(_merge_kernel, with_router=with_router),
        out_shape=(jax.ShapeDtypeStruct((t, d), F32), jax.ShapeDtypeStruct((t, d), f_dtype),
                   jax.ShapeDtypeStruct((t, TOP_K), jnp.int32), jax.ShapeDtypeStruct((t, TOP_K), F32)),
        grid=(t // tm,),
        in_specs=[pl.BlockSpec((tm, d), row),
                  pl.BlockSpec((tm, BRANCH_W), row), pl.BlockSpec((tm, BRANCH_W), row),
                  pl.BlockSpec((tm, BRANCH_W), row),
                  pl.BlockSpec((tm, N_BRANCH * d), row),
                  pl.BlockSpec((1, 6, d), mod_map),
                  pl.BlockSpec(gains.shape, full2),
                  pl.BlockSpec(w_branch.shape, lambda i: (0, 0, 0), pipeline_mode=pl.Buffered(1)),
                  pl.BlockSpec(w_out.shape, full2, pipeline_mode=pl.Buffered(1)),
                  pl.BlockSpec(w_router.shape, full2)],
        out_specs=(pl.BlockSpec((tm, d), row), pl.BlockSpec((tm, d), row),
                   pl.BlockSpec((tm, TOP_K), row), pl.BlockSpec((tm, TOP_K), row)),
        compiler_params=_cparams(("parallel",)),
        name="merge_ctx" if is_ctx else "merge_lat",
    )(h, y_ret, y_win, y_na, gates, mod, gains, w_branch, w_out, w_router)


def _swiglu(x, w1, w3, w2, act_ref):
    dff = act_ref.shape[1]
    for c0 in range(0, dff, FF_CHUNK):
        cw = min(FF_CHUNK, dff - c0)
        a1 = _dot(x, w1[:, c0:c0 + cw])
        a3 = _dot(x, w3[:, c0:c0 + cw])
        act_ref[:, c0:c0 + cw] = (a1 * _sigmoid(a1) * a3).astype(BF16)
    return _dot(act_ref[...], w2[...])


def _ffn_kernel(h_ref, f_ref, mod_ref, g_ref, w1_ref, w3_ref, w2_ref, o_ref, act_ref):
    y = _swiglu(f_ref[...], w1_ref, w3_ref, w2_ref, act_ref)
    o_ref[...] = h_ref[...] + mod_ref[0, 5:6, :] * _rms(y, g_ref[3:4, :])


def _ffn(h, f, mod, gains, w1, w3, w2, *, is_ctx, per_batch, tm=512):
    t, d = h.shape
    dff = w1.shape[1]
    if is_ctx:
        mod_map = lambda i: (mod.shape[0] - 1, 0, 0)
    else:
        mod_map = lambda i: (i // (per_batch // tm), 0, 0)
    row = lambda i: (i, 0)
    const = lambda shape: pl.BlockSpec(shape, lambda i: (0, 0), pipeline_mode=pl.Buffered(1))
    return pl.pallas_call(
        _ffn_kernel,
        out_shape=jax.ShapeDtypeStruct((t, d), F32),
        grid=(t // tm,),
        in_specs=[pl.BlockSpec((tm, d), row), pl.BlockSpec((tm, d), row),
                  pl.BlockSpec((1, 6, d), mod_map),
                  pl.BlockSpec(gains.shape, lambda i: (0, 0)),
                  const((d, dff)), const((d, dff)), const((dff, d))],
        out_specs=pl.BlockSpec((tm, d), row),
        scratch_shapes=[pltpu.VMEM((tm, dff), BF16)],
        compiler_params=_cparams(("parallel",)),
        name="ffn_ctx" if is_ctx else "ffn_lat",
    )(h, f, mod, gains, w1, w3, w2)


def _route_slots(idx, tile):
    n = idx.shape[0] * TOP_K
    e = idx.reshape(n)
    onehot = (e[:, None] == jnp.arange(N_EXPERTS, dtype=jnp.int32)).astype(jnp.int32)
    csum = jnp.cumsum(onehot, axis=0)
    counts = csum[-1]
    rank = jnp.sum(csum * onehot, axis=1) - 1
    padded = (counts + tile - 1) // tile * tile
    ends = jnp.cumsum(padded)
    starts = ends - padded
    pos = jnp.sum(onehot * starts[None, :], axis=1) + rank
    n_tiles = n // tile + N_EXPERTS
    tile_start = jnp.arange(n_tiles, dtype=jnp.int32) * tile
    tile_expert = jnp.minimum(jnp.sum((tile_start[:, None] >= ends[None, :]).astype(jnp.int32), axis=1),
                              N_EXPERTS - 1)
    n_active = (ends[-1] // tile).reshape(1)
    return pos.astype(jnp.int32), tile_expert.astype(jnp.int32), n_active.astype(jnp.int32)


def _dispatch_kernel(pos_ref, f_ref, xs_in_ref, xs_ref, sem):
    del xs_in_ref
    tm = f_ref.shape[0]

    def row_copy(r, k):
        p = pos_ref[0, 0, TOP_K * r + k]
        return pltpu.make_async_copy(f_ref.at[pl.ds(r, 1), :], xs_ref.at[pl.ds(p, 1), :], sem)

    def start(r, carry):
        for k in range(TOP_K):
            row_copy(r, k).start()
        return carry

    def wait(r, carry):
        for k in range(TOP_K):
            row_copy(r, k).wait()
        return carry

    lax.fori_loop(0, tm, start, 0, unroll=8)
    lax.fori_loop(0, tm, wait, 0, unroll=8)


def _dispatch(f, pos, xs, tm=256):
    t, d = f.shape
    pos3 = pos.reshape(t // tm, 1, TOP_K * tm)
    return pl.pallas_call(
        _dispatch_kernel,
        out_shape=jax.ShapeDtypeStruct(xs.shape, xs.dtype),
        grid=(t // tm,),
        in_specs=[pl.BlockSpec((1, 1, TOP_K * tm), lambda i: (i, 0, 0), memory_space=pltpu.SMEM),
                  pl.BlockSpec((tm, d), lambda i: (i, 0)),
                  pl.BlockSpec(memory_space=pl.ANY)],
        out_specs=pl.BlockSpec(memory_space=pl.ANY),
        scratch_shapes=[pltpu.SemaphoreType.DMA(())],
        input_output_aliases={2: 0},
        compiler_params=_cparams(("arbitrary",)),
        name="moe_dispatch",
    )(pos3, f, xs)


def _experts_kernel(te_ref, na_ref, x_ref, *refs, has_prev):
    del te_ref
    if has_prev:
        yprev_ref, w1_ref, w3_ref, w2_ref, y_ref, act_ref = refs
    else:
        w1_ref, w3_ref, w2_ref, y_ref, act_ref = refs
    active = pl.program_id(0) < na_ref[0]

    @pl.when(active)
    def _():
        y = _swiglu(x_ref[...].astype(BF16), w1_ref.at[0], w3_ref.at[0], w2_ref.at[0], act_ref)
        y_ref[...] = yprev_ref[...] + y if has_prev else y

    @pl.when(jnp.logical_not(active))
    def _():
        y_ref[...] = jnp.zeros_like(y_ref)


def _experts_pass(xs, y_prev, tile_expert, n_active, w1, w3, w2, *, tile, fc, c):
    p, d = xs.shape
    slot = pl.BlockSpec((tile, d), lambda j, te, na: (j, 0))
    in_specs = [slot] + ([slot] if y_prev is not None else []) + [
        pl.BlockSpec((1, d, fc), lambda j, te, na: (te[j], 0, c)),
        pl.BlockSpec((1, d, fc), lambda j, te, na: (te[j], 0, c)),
        pl.BlockSpec((1, fc, d), lambda j, te, na: (te[j], c, 0))]
    grid_spec = pltpu.PrefetchScalarGridSpec(
        num_scalar_prefetch=2, grid=(p // tile,), in_specs=in_specs, out_specs=slot,
        scratch_shapes=[pltpu.VMEM((tile, fc), BF16)])
    args = (xs,) + ((y_prev,) if y_prev is not None else ()) + (w1, w3, w2)
    return pl.pallas_call(
        functools.partial(_experts_kernel, has_prev=y_prev is not None),
        out_shape=jax.ShapeDtypeStruct((p, d), F32),
        grid_spec=grid_spec,
        compiler_params=_cparams(("arbitrary",)),
        name="moe_experts",
    )(tile_expert, n_active, *args)


def _experts(xs, tile_expert, n_active, w1, w3, w2, *, tile, fc):
    y = None
    for c in range(w1.shape[2] // fc):
        y = _experts_pass(xs, y, tile_expert, n_active, w1, w3, w2, tile=tile, fc=fc, c=c)
    return y


def _combine_kernel(pos_ref, gw_ref, h_ref, mod_ref, g_ref, y_any, o_ref, buf_ref, sem):
    tm = h_ref.shape[0]

    def row_copy(r, k):
        p = pos_ref[0, 0, TOP_K * r + k]
        return pltpu.make_async_copy(y_any.at[pl.ds(p, 1), :], buf_ref.at[k, pl.ds(r, 1), :], sem)

    def start(r, carry):
        for k in range(TOP_K):
            row_copy(r, k).start()
        return carry

    def wait(r, carry):
        for k in range(TOP_K):
            row_copy(r, k).wait()
        return carry

    lax.fori_loop(0, tm, start, 0, unroll=8)
    lax.fori_loop(0, tm, wait, 0, unroll=8)
    gw = gw_ref[...]
    y = gw[:, 0:1] * buf_ref[0] + gw[:, 1:2] * buf_ref[1]
    o_ref[...] = h_ref[...] + mod_ref[0, 5:6, :] * _rms(y, g_ref[3:4, :])


def _combine(h, pos, gw, y_sorted, mod, gains, *, is_ctx, per_batch, tm=256):
    t, d = h.shape
    pos3 = pos.reshape(t // tm, 1, TOP_K * tm)
    if is_ctx:
        mod_map = lambda i: (mod.shape[0] - 1, 0, 0)
    else:
        mod_map = lambda i: (i // (per_batch // tm), 0, 0)
    row = lambda i: (i, 0)
    return pl.pallas_call(
        _combine_kernel,
        out_shape=jax.ShapeDtypeStruct((t, d), F32),
        grid=(t // tm,),
        in_specs=[pl.BlockSpec((1, 1, TOP_K * tm), lambda i: (i, 0, 0), memory_space=pltpu.SMEM),
                  pl.BlockSpec((tm, TOP_K), row),
                  pl.BlockSpec((tm, d), row),
                  pl.BlockSpec((1, 6, d), mod_map),
                  pl.BlockSpec(gains.shape, lambda i: (0, 0)),
                  pl.BlockSpec(memory_space=pl.ANY)],
        out_specs=pl.BlockSpec((tm, d), row),
        scratch_shapes=[pltpu.VMEM((TOP_K, tm, d), F32), pltpu.SemaphoreType.DMA(())],
        compiler_params=_cparams(("arbitrary",)),
        name="moe_combine_ctx" if is_ctx else "moe_combine_lat",
    )(pos3, gw, h, mod, gains, y_sorted)


def _rope_tables(s, lc):
    def lin(pos):
        n_freq = RET_DK // 2
        inv = ROPE_BASE ** (-jnp.arange(n_freq, dtype=F32) / n_freq)
        ang = pos.astype(F32)[:, None] * inv
        cos, sin = jnp.cos(ang), jnp.sin(ang)
        return jnp.concatenate([cos, cos], -1), jnp.concatenate([-sin, sin], -1)

    t = jnp.arange(s)
    n_freq = WIN_HD // 4
    inv = ROPE_BASE ** (-jnp.arange(n_freq, dtype=F32) / n_freq)
    rowa = (t // GRID_W).astype(F32)[:, None] * inv
    cola = (t % GRID_W).astype(F32)[:, None] * inv
    ang = jnp.concatenate([rowa, cola], axis=-1)
    cos, sin = jnp.cos(ang), jnp.sin(ang)
    ca = jnp.tile(jnp.concatenate([cos, cos], -1), (1, 2))
    sa = jnp.tile(jnp.concatenate([-sin, sin], -1), (1, 2))
    cr_l, sr_l = lin(lc + jnp.arange(s))
    cr_c, sr_c = lin(jnp.arange(lc))
    return (cr_l, sr_l, ca, sa), (cr_c, sr_c, ca[:lc], sa[:lc])


def kernel(x, c, ctx, c_ctx, w_mod, b_mod, norm_gains, w_in, ret_decay, win_sink, na_rpb, w_branch, w_out,
           ffn_w1, ffn_w3, ffn_w2, moe_router, moe_w1, moe_w3, moe_w2):
    batch, s, d = x.shape
    lc = ctx.shape[1]
    depth = w_mod.shape[0]

    c_all = jnp.zeros((16, d), F32).at[:batch].set(c).at[batch].set(c_ctx)
    mods = _modulation(c_all, w_mod, b_mod)[:, :batch + 1].reshape(depth, batch + 1, 6, d)

    tabs_l, tabs_c = _rope_tables(s, lc)
    log_gamma = jnp.log1p(-jnp.exp(ret_decay.astype(F32)))
    na_blocks, na_ids = _na_bias_pairs(na_rpb, s)

    h_l = x.reshape(batch * s, d)
    h_c = ctx.reshape(batch * lc, d)
    zero_router = jnp.zeros((d, N_EXPERTS), F32)

    for layer in range(depth):
        need_ctx = layer < depth - 1
        mod = mods[layer]
        gains = norm_gains[layer]
        w_in_b = w_in[layer].astype(BF16)
        wb_b = w_branch[layer].astype(BF16)
        wo_b = w_out[layer].astype(BF16)
        i = layer // 2
        is_moe = layer % 2 == 1
        if is_moe:
            w1, w3, w2 = moe_w1[i].astype(BF16), moe_w3[i].astype(BF16), moe_w2[i].astype(BF16)
            w_router = moe_router[i]
        else:
            w1, w3, w2 = ffn_w1[i].astype(BF16), ffn_w3[i].astype(BF16), ffn_w2[i].astype(BF16)
            w_router = zero_router

        p_ret_l, p_win_l, p_na_l, p_gt_l = _inproj(h_l, mod, gains, w_in_b, *tabs_l, is_ctx=False, tm=512)
        p_ret_c, p_win_c, p_na_c, p_gt_c = _inproj(h_c, mod, gains, w_in_b, *tabs_c, is_ctx=True, tm=256)

        y_ret_l, y_ret_c = _retention(log_gamma[layer], p_ret_l, p_ret_c, batch)
        y_win_l = _window(win_sink[layer], p_win_l, p_win_c, batch)
        y_na_l = _neighbourhood(p_na_l, p_na_c, na_blocks, na_ids, layer, batch)

        h_l, f_l, idx_l, gw_l = _merge(h_l, y_ret_l, y_win_l, y_na_l, p_gt_l, mod, gains, wb_b, wo_b,
                                       w_router, with_router=is_moe, is_ctx=False, per_batch=s, tm=512)
        if need_ctx:
            y_win_c, y_na_c = _ctx_attention(win_sink[layer], p_win_c, p_na_c, batch)
            h_c, f_c, idx_c, gw_c = _merge(h_c, y_ret_c, y_win_c, y_na_c, p_gt_c, mod, gains, wb_b, wo_b,
                                           w_router, with_router=is_moe, is_ctx=True, per_batch=lc, tm=512)

        if not is_moe:
            h_l = _ffn(h_l, f_l, mod, gains, w1, w3, w2, is_ctx=False, per_batch=s)
            if need_ctx:
                h_c = _ffn(h_c, f_c, mod, gains, w1, w3, w2, is_ctx=True, per_batch=lc)
            continue

        idx_all = jnp.concatenate([idx_l, idx_c], axis=0) if need_ctx else idx_l
        pos, tile_expert, n_active = _route_slots(idx_all, MOE_TILE)
        n_slots = tile_expert.shape[0] * MOE_TILE
        xs = jnp.zeros((n_slots, d), F32)
        n_l = TOP_K * h_l.shape[0]
        xs = _dispatch(f_l, pos[:n_l], xs)
        if need_ctx:
            xs = _dispatch(f_c, pos[n_l:], xs)
        y_sorted = _experts(xs, tile_expert, n_active, w1, w3, w2, tile=MOE_TILE, fc=MOE_FC)
        h_l = _combine(h_l, pos[:n_l], gw_l, y_sorted, mod, gains, is_ctx=False, per_batch=s)
        if need_ctx:
            h_c = _combine(h_c, pos[n_l:], gw_c, y_sorted, mod, gains, is_ctx=True, per_batch=lc)
    return h_l.reshape(batch, s, d)
```

```python
import functools
import math

import jax
import jax.numpy as jnp
import numpy as np
from jax import lax
from jax.experimental import pallas as pl
from jax.experimental.pallas import tpu as pltpu

F32 = jnp.float32
BF16 = jnp.bfloat16

D_MODEL = 1024
GRID_W = 64
RET_HEADS = 4
RET_DK = 128
WIN_HEADS = 8
WIN_KV_HEADS = 2
WIN_HD = 64
WINDOW = 128
WIN_BLOCK = 128
NA_HEADS = 8
NA_HD = 64
NA_ROWS_MAX = 8
NA_COLS = 16
BRANCH_W = 512
N_BRANCH = 3
N_EXPERTS = 8
TOP_K = 2
ROPE_BASE = 10000.0
NORM_EPS = 1e-6
NEG_INF = -1e30
LOG2E = math.log2(math.e)

LANES = 128
VMEM_LIMIT = 56 * 1024 * 1024

C_RQ, C_RK, C_RV, C_RG = 0, 512, 1024, 1536
C_WQ, C_WK, C_WV = 2048, 2560, 2688
C_NQ, C_NK, C_NV = 2816, 3328, 3840
C_GATE = 4352
D_IN = 7424

RET_CHUNK = 256
NA_QROWS = 4
NA_KROWS = 12
ROW_BLK = 16
FF_CHUNK = 256
MOE_TILE = 512
MOE_FC = 1792


def _cparams(sem):
    return pltpu.CompilerParams(dimension_semantics=sem, vmem_limit_bytes=VMEM_LIMIT)


def _rms(x, g):
    return x * lax.rsqrt(jnp.mean(x * x, axis=-1, keepdims=True) + NORM_EPS) * g


def _sigmoid(x):
    return 0.5 * jnp.tanh(0.5 * x) + 0.5


def _dot(a, b):
    return jnp.dot(a, b, preferred_element_type=F32)


def _dot_nt(a, b):
    return lax.dot_general(a, b, (((1,), (1,)), ((), ())), preferred_element_type=F32)


def _dot_tn(a, b):
    return lax.dot_general(a, b, (((0,), (0,)), ((), ())), preferred_element_type=F32)


def _mod_kernel(c_ref, w_ref, b_ref, o_ref):
    c = c_ref[...]
    s = c * jax.nn.sigmoid(c)
    o_ref[0] = jnp.dot(s, w_ref[0], preferred_element_type=F32,
                       precision=lax.Precision.HIGHEST) + b_ref[0]


def _modulation(c_all, w_mod, b_mod):
    depth, d, n = w_mod.shape
    tn = 1536
    return pl.pallas_call(
        _mod_kernel,
        out_shape=jax.ShapeDtypeStruct((depth, c_all.shape[0], n), F32),
        grid=(depth, n // tn),
        in_specs=[pl.BlockSpec(c_all.shape, lambda l, j: (0, 0)),
                  pl.BlockSpec((1, d, tn), lambda l, j: (l, 0, j)),
                  pl.BlockSpec((1, 1, tn), lambda l, j: (l, 0, j))],
        out_specs=pl.BlockSpec((1, c_all.shape[0], tn), lambda l, j: (l, 0, j)),
        compiler_params=_cparams(("arbitrary", "arbitrary")),
        name="modulation",
    )(c_all, w_mod, b_mod.reshape(depth, 1, n))


def _inproj_kernel(h_ref, mod_ref, g_ref, w_ref, cr_ref, sr_ref, ca_ref, sa_ref,
                   oret_ref, owin_ref, ona_ref, og_ref, *, rope_win):
    x = h_ref[...]
    a = _rms(x, g_ref[0:1, :]) * (1.0 + mod_ref[0, 1:2, :]) + mod_ref[0, 0:1, :]
    a = a.astype(BF16)
    tm = x.shape[0]
    lane = lax.broadcasted_iota(jnp.int32, (tm, LANES), 1)
    even = lane < 64
    first_half = (lane % 64) < 32

    def mm(c0, cw):
        return _dot(a, w_ref[:, c0:c0 + cw])

    def rope_lin(v):
        return v * cr_ref[...] + pltpu.roll(v, 64, 1) * sr_ref[...]

    def rope_ax(v):
        rot = jnp.where(first_half, pltpu.roll(v, 96, 1), pltpu.roll(v, 32, 1))
        return v * ca_ref[...] + rot * sa_ref[...]

    r = mm(C_RQ, 512)
    for h in range(RET_HEADS):
        blk = rope_lin(r[:, h * 128:(h + 1) * 128]) * (RET_DK ** -0.5)
        oret_ref[:, h * 128:(h + 1) * 128] = blk.astype(BF16)
    r = mm(C_RK, 512)
    for h in range(RET_HEADS):
        blk = rope_lin(r[:, h * 128:(h + 1) * 128])
        oret_ref[:, 512 + h * 128:512 + (h + 1) * 128] = blk.astype(BF16)
    oret_ref[:, 1024:1536] = mm(C_RV, 512).astype(BF16)
    oret_ref[:, 1536:2048] = mm(C_RG, 512).astype(BF16)

    r = mm(C_WQ, 512)
    for p in range(WIN_HEADS // 2):
        blk = r[:, p * 128:(p + 1) * 128]
        if rope_win:
            blk = rope_ax(blk)
        blk = blk * (WIN_HD ** -0.5 * LOG2E)
        owin_ref[:, (2 * p) * 128:(2 * p + 1) * 128] = jnp.where(even, blk, 0.0).astype(BF16)
        owin_ref[:, (2 * p + 1) * 128:(2 * p + 2) * 128] = jnp.where(even, 0.0, blk).astype(BF16)
    r = mm(C_WK, 256)
    k2 = r[:, 0:128]
    if rope_win:
        k2 = rope_ax(k2)
    v2 = r[:, 128:256]
    for i, t in enumerate((k2, v2)):
        sw = pltpu.roll(t, 64, 1)
        owin_ref[:, 1024 + 256 * i:1024 + 256 * i + 128] = jnp.where(even, t, sw).astype(BF16)
        owin_ref[:, 1024 + 256 * i + 128:1024 + 256 * i + 256] = jnp.where(even, sw, t).astype(BF16)

    r = mm(C_NQ, 512) * (NA_HD ** -0.5 * LOG2E)
    for p in range(NA_HEADS // 2):
        blk = r[:, p * 128:(p + 1) * 128]
        ona_ref[:, (2 * p) * 128:(2 * p + 1) * 128] = jnp.where(even, blk, 0.0).astype(BF16)
        ona_ref[:, (2 * p + 1) * 128:(2 * p + 2) * 128] = jnp.where(even, 0.0, blk).astype(BF16)
    ona_ref[:, 1024:1536] = mm(C_NK, 512).astype(BF16)
    ona_ref[:, 1536:2048] = mm(C_NV, 512).astype(BF16)

    for j in range(6):
        og_ref[:, j * 512:(j + 1) * 512] = mm(C_GATE + j * 512, 512).astype(BF16)


def _inproj(h, mod, gains, w_in, cr, sr, ca, sa, *, is_ctx, tm=256):
    t, d = h.shape
    nt = t // tm
    per_batch = cr.shape[0] // tm
    if is_ctx:
        mod_map = lambda i: (mod.shape[0] - 1, 0, 0)
    else:
        mod_map = lambda i: (i // per_batch, 0, 0)
    pos_map = lambda i: (i % per_batch, 0)
    full = lambda i: (0, 0)
    row = lambda i: (i, 0)
    outs = (jax.ShapeDtypeStruct((t, 2048), BF16), jax.ShapeDtypeStruct((t, 1536), BF16),
            jax.ShapeDtypeStruct((t, 2048), BF16), jax.ShapeDtypeStruct((t, 3072), BF16))
    return pl.pallas_call(
        functools.partial(_inproj_kernel, rope_win=not is_ctx),
        out_shape=outs,
        grid=(nt,),
        in_specs=[pl.BlockSpec((tm, d), row),
                  pl.BlockSpec((1, 6, d), mod_map),
                  pl.BlockSpec(gains.shape, full),
                  pl.BlockSpec(w_in.shape, full, pipeline_mode=pl.Buffered(1)),
                  pl.BlockSpec((tm, LANES), pos_map), pl.BlockSpec((tm, LANES), pos_map),
                  pl.BlockSpec((tm, LANES), pos_map), pl.BlockSpec((tm, LANES), pos_map)],
        out_specs=(pl.BlockSpec((tm, 2048), row), pl.BlockSpec((tm, 1536), row),
                   pl.BlockSpec((tm, 2048), row), pl.BlockSpec((tm, 3072), row)),
        compiler_params=_cparams(("parallel",)),
        name="inproj_ctx" if is_ctx else "inproj_lat",
    )(h, mod, gains, w_in, cr, sr, ca, sa)


def _ret_kernel(lg_ref, ql, kl, vl, gl, qc, kc, vc, gc, yl_ref, yc_ref, acc_ref):
    C = RET_CHUNK
    h = pl.program_id(1)
    lgf = lg_ref[0, h]
    lgb = lg_ref[1, h]
    ii = lax.broadcasted_iota(jnp.int32, (C, C), 0).astype(F32)
    jj = lax.broadcasted_iota(jnp.int32, (C, C), 1).astype(F32)
    diff = ii - jj
    dmat = jnp.where(diff >= 0, jnp.exp(lgf * jnp.maximum(diff, 0.0)),
                     jnp.exp(lgb * jnp.maximum(-diff, 0.0)))
    r = lax.broadcasted_iota(jnp.int32, (C, RET_DK), 0).astype(F32)
    qdf = jnp.exp(lgf * (r + 1.0))
    kdf = jnp.exp(lgf * (C - 1.0 - r))
    qdb = jnp.exp(lgb * (C - r))
    kdb = jnp.exp(lgb * r)
    ones = jnp.ones((RET_DK, RET_DK), F32)
    cdf = jnp.exp(ones * (lgf * C))
    cdb = jnp.exp(ones * (lgb * C))

    n_lat = ql.shape[0] // C
    lc = qc.shape[0]
    chunks = [(qc, kc, vc, gc, yc_ref, 0, 0)]
    chunks += [(ql, kl, vl, gl, yl_ref, j * C, lc + j * C) for j in range(n_lat)]

    state = jnp.zeros((RET_DK, RET_DK), F32)
    for (qr, kr, vr, _, _, off, aoff) in chunks:
        q = qr[off:off + C, :]
        k = kr[off:off + C, :]
        v = vr[off:off + C, :]
        s = _dot_nt(q, k) * dmat
        y = _dot(s.astype(BF16), v)
        y = y + _dot((q.astype(F32) * qdf).astype(BF16), state.astype(BF16))
        state = state * cdf + _dot_tn((k.astype(F32) * kdf).astype(BF16), v)
        acc_ref[aoff:aoff + C, :] = y

    state = jnp.zeros((RET_DK, RET_DK), F32)
    for (qr, kr, vr, gr, yr, off, aoff) in [chunks[0]] + chunks[:0:-1]:
        q = qr[off:off + C, :]
        k = kr[off:off + C, :]
        v = vr[off:off + C, :]
        y = acc_ref[aoff:aoff + C, :] + _dot((q.astype(F32) * qdb).astype(BF16), state.astype(BF16))
        state = state * cdb + _dot_tn((k.astype(F32) * kdb).astype(BF16), v)
        mu = jnp.mean(y, axis=-1, keepdims=True)
        dlt = y - mu
        var = jnp.mean(dlt * dlt, axis=-1, keepdims=True)
        yn = dlt * lax.rsqrt(var + NORM_EPS)
        g = gr[off:off + C, :].astype(F32)
        yr[off:off + C, :] = (g * _sigmoid(g) * yn).astype(BF16)


def _retention(lg, pl_ret, pc_ret, batch):
    tl = pl_ret.shape[0]
    tc = pc_ret.shape[0]
    s = tl // batch
    lc = tc // batch
    assert lc == RET_CHUNK and s % RET_CHUNK == 0
    lat = lambda off: pl.BlockSpec((s, 128), lambda b, h, off=off: (b, off + h))
    ctx = lambda off: pl.BlockSpec((lc, 128), lambda b, h, off=off: (b, off + h))
    return pl.pallas_call(
        _ret_kernel,
        out_shape=(jax.ShapeDtypeStruct((tl, 512), BF16), jax.ShapeDtypeStruct((tc, 512), BF16)),
        grid=(batch, RET_HEADS),
        in_specs=[pl.BlockSpec(memory_space=pltpu.SMEM),
                  lat(0), lat(4), lat(8), lat(12), ctx(0), ctx(4), ctx(8), ctx(12)],
        out_specs=(pl.BlockSpec((s, 128), lambda b, h: (b, h)),
                   pl.BlockSpec((lc, 128), lambda b, h: (b, h))),
        scratch_shapes=[pltpu.VMEM((s + lc, 128), F32)],
        compiler_params=_cparams(("parallel", "parallel")),
        name="retention",
    )(lg, pl_ret, pl_ret, pl_ret, pl_ret, pc_ret, pc_ret, pc_ret, pc_ret)


def _softmax_parts(parts, sink=None):
    m = functools.reduce(jnp.maximum, [jnp.max(p, axis=-1, keepdims=True) for p in parts])
    if sink is not None:
        m = jnp.maximum(m, sink)
    es = [jnp.exp2(p - m) for p in parts]
    den = functools.reduce(lambda a, b: a + b, [jnp.sum(e, axis=-1, keepdims=True) for e in es])
    if sink is not None:
        den = den + jnp.exp2(sink - m)
    return [e.astype(BF16) for e in es], 1.0 / den


def _softmax_rows(s_ref, p_ref, sink_ref, row0, nrows, split, bias_rows, sink=None):
    for r in range(0, nrows, ROW_BLK):
        rows = slice(row0 + r, row0 + r + ROW_BLK)
        sb = s_ref[rows, :]
        parts = [sb[:, :split] + bias_rows(r), sb[:, split:]]
        m = functools.reduce(jnp.maximum, [jnp.max(p, axis=-1, keepdims=True) for p in parts])
        if sink is not None:
            m = jnp.maximum(m, sink)
            sink_ref[rows, :] = jnp.broadcast_to(jnp.exp2(sink - m), (ROW_BLK, LANES))
        p_ref[rows, :] = jnp.concatenate([jnp.exp2((p - m).astype(BF16)) for p in parts], axis=1)


def _weighted_values(p, v, extra=None):
    r = _dot(p, jnp.concatenate([v, jnp.ones(v.shape, BF16)], axis=1))
    den = r[:, LANES:]
    if extra is not None:
        den = den + extra
    return r[:, :LANES] * (1.0 / den)


def _win_kernel(sink_ref, q_ref, k0, k1, v0, v1, ck0, ck1, cv0, cv1, o_ref, s_ref, p_ref, inv_ref):
    blk = WIN_BLOCK
    nw = 3 * blk
    s_len = k0.shape[0]
    even = lax.broadcasted_iota(jnp.int32, (blk, LANES), 1) < 64
    g = WIN_HEADS // WIN_KV_HEADS
    row_iota = lax.broadcasted_iota(jnp.int32, (ROW_BLK, nw), 0)
    col_iota = lax.broadcasted_iota(jnp.int32, (ROW_BLK, nw), 1)
    for jj in range(q_ref.shape[0] // blk):
        rows = slice(jj * blk, (jj + 1) * blk)
        j = pl.program_id(1) * (q_ref.shape[0] // blk) + jj
        start = pl.multiple_of(jnp.clip((j - 1) * blk, 0, s_len - nw), blk)
        rel = (start - j * blk) + col_iota - row_iota

        def band(r, rel=rel):
            return jnp.where(jnp.abs(rel - r) <= WINDOW, 0.0, NEG_INF)

        for kv, (kr, vr, ckr, cvr) in enumerate(((k0, v0, ck0, cv0), (k1, v1, ck1, cv1))):
            c = WIN_KV_HEADS * jj + kv
            kcat = jnp.concatenate([kr[pl.ds(start, nw), :], ckr[...]], axis=0)
            vcat = jnp.concatenate([vr[pl.ds(start, nw), :], cvr[...]], axis=0)
            qst = jnp.concatenate(
                [q_ref[rows, (g * kv + i) * 128:(g * kv + i + 1) * 128] for i in range(g)], axis=0)
            s_ref[c] = _dot_nt(qst, kcat)
            for i in range(g):
                _softmax_rows(s_ref.at[c], p_ref.at[c], inv_ref.at[c], i * blk, blk, nw, band,
                              sink_ref[g * kv + i] * LOG2E)
            pv = _weighted_values(p_ref[c], vcat, inv_ref[c])
            for p in range(g // 2):
                o_ref[rows, (2 * kv + p) * 128:(2 * kv + p + 1) * 128] = jnp.where(
                    even, pv[(2 * p) * blk:(2 * p + 1) * blk],
                    pv[(2 * p + 1) * blk:(2 * p + 2) * blk]).astype(BF16)


def _window(sink, pl_win, pc_win, batch):
    tl = pl_win.shape[0]
    tc = pc_win.shape[0]
    s = tl // batch
    lc = tc // batch
    tq = 2 * WIN_BLOCK
    nb = s // tq
    n_chain = (tq // WIN_BLOCK) * WIN_KV_HEADS
    m_rows = (WIN_HEADS // WIN_KV_HEADS) * WIN_BLOCK
    n_keys = 3 * WIN_BLOCK + lc
    lat = lambda cb: pl.BlockSpec((s, 128), lambda b, j, cb=cb: (b, cb))
    ctx = lambda cb: pl.BlockSpec((lc, 128), lambda b, j, cb=cb: (b, cb))
    return pl.pallas_call(
        _win_kernel,
        out_shape=jax.ShapeDtypeStruct((tl, 512), BF16),
        grid=(batch, nb),
        in_specs=[pl.BlockSpec(memory_space=pltpu.SMEM),
                  pl.BlockSpec((tq, 1024), lambda b, j: (b * nb + j, 0)),
                  lat(8), lat(9), lat(10), lat(11), ctx(8), ctx(9), ctx(10), ctx(11)],
        out_specs=pl.BlockSpec((tq, 512), lambda b, j: (b * nb + j, 0)),
        scratch_shapes=[pltpu.VMEM((n_chain, m_rows, n_keys), F32),
                        pltpu.VMEM((n_chain, m_rows, n_keys), BF16),
                        pltpu.VMEM((n_chain, m_rows, LANES), F32)],
        compiler_params=_cparams(("parallel", "arbitrary")),
        name="window_attn",
    )(sink, pl_win, pl_win, pl_win, pl_win, pl_win, pc_win, pc_win, pc_win, pc_win)


def _na_kernel(ids_ref, q_ref, ka, kb, kc, va, vb, vc, ck, cv, pb_ref, o_ref, s_ref, p_ref):
    nq = q_ref.shape[0]
    n_nb = ka.shape[0] * 3
    even = lax.broadcasted_iota(jnp.int32, (nq, LANES), 1) < 64
    i = pl.program_id(0)
    pat = jnp.where(i == 0, 0, jnp.where(i == pl.num_programs(0) - 1, 2, 1))

    for p in range(NA_HEADS // 2):
        lanes = slice(p * 128, (p + 1) * 128)
        qst = jnp.concatenate([q_ref[:, (2 * p) * 128:(2 * p + 1) * 128],
                               q_ref[:, (2 * p + 1) * 128:(2 * p + 2) * 128]], axis=0)
        kcat = jnp.concatenate([ka[:, lanes], kb[:, lanes], kc[:, lanes], ck[:, lanes]], axis=0)
        vcat = jnp.concatenate([va[:, lanes], vb[:, lanes], vc[:, lanes], cv[:, lanes]], axis=0)
        s_ref[p] = _dot_nt(qst, kcat)
        for par in range(2):
            for qr in range(NA_QROWS):
                blocks = [pb_ref.at[0, 2 * p + par, ids_ref[pat, qr, kp]] for kp in range(NA_KROWS // 2)]

                def bias_rows(r, blocks=blocks):
                    return jnp.concatenate([b[r:r + ROW_BLK, :] for b in blocks], axis=1)

                _softmax_rows(s_ref.at[p], p_ref.at[p], None, par * nq + qr * GRID_W, GRID_W,
                              n_nb, bias_rows)
        pv = _weighted_values(p_ref[p], vcat)
        o_ref[:, lanes] = jnp.where(even, pv[:nq], pv[nq:]).astype(BF16)


def _neighbourhood(pl_na, pc_na, bias_blocks, bias_ids, layer, batch):
    tl = pl_na.shape[0]
    tc = pc_na.shape[0]
    s = tl // batch
    lc = tc // batch
    nq = NA_QROWS * GRID_W
    ng = s // nq
    kblk = nq
    nkb = s // kblk
    assert NA_KROWS * GRID_W == 3 * kblk

    def kspec(cb, d):
        return pl.BlockSpec(
            (kblk, 512), lambda i, b, d=d, cb=cb: (b * nkb + jnp.clip(i - 1, 0, nkb - 3) + d, cb))

    return pl.pallas_call(
        _na_kernel,
        out_shape=jax.ShapeDtypeStruct((tl, 512), BF16),
        grid=(ng, batch),
        in_specs=[pl.BlockSpec(memory_space=pltpu.SMEM),
                  pl.BlockSpec((nq, 1024), lambda i, b: (b * ng + i, 0)),
                  kspec(2, 0), kspec(2, 1), kspec(2, 2),
                  kspec(3, 0), kspec(3, 1), kspec(3, 2),
                  pl.BlockSpec((lc, 512), lambda i, b: (b, 2)),
                  pl.BlockSpec((lc, 512), lambda i, b: (b, 3)),
                  pl.BlockSpec((1,) + bias_blocks.shape[1:], lambda i, b: (layer, 0, 0, 0, 0),
                               pipeline_mode=pl.Buffered(1))],
        out_specs=pl.BlockSpec((nq, 512), lambda i, b: (b * ng + i, 0)),
        scratch_shapes=[pltpu.VMEM((NA_HEADS // 2, 2 * nq, 3 * kblk + lc), F32),
                        pltpu.VMEM((NA_HEADS // 2, 2 * nq, 3 * kblk + lc), BF16)],
        compiler_params=_cparams(("parallel", "arbitrary")),
        name="neighbourhood_attn",
    )(bias_ids, pl_na, pl_na, pl_na, pl_na, pl_na, pl_na, pl_na, pc_na, pc_na, bias_blocks)


def _na_pair_plan(s):
    rows = s // GRID_W
    kr = min(NA_ROWS_MAX, rows)
    ng = rows // NA_QROWS
    pairs = {}
    ids = np.zeros((3, NA_QROWS, NA_KROWS // 2), np.int32)
    for pi, i in enumerate((0, 1, ng - 1)):
        k0 = NA_QROWS * int(np.clip(i - 1, 0, ng - 3))
        for qr in range(NA_QROWS):
            r = NA_QROWS * i + qr
            r_start = int(np.clip(r - kr // 2, 0, rows - kr))
            for kp in range(NA_KROWS // 2):
                offs = tuple(
                    krow - r + NA_ROWS_MAX - 1 if r_start <= krow < r_start + kr else -1
                    for krow in (k0 + 2 * kp, k0 + 2 * kp + 1))
                ids[pi, qr, kp] = pairs.setdefault(offs, len(pairs))
    return list(pairs), ids


def _na_bias_pairs(rpb, s):
    pair_list, ids = _na_pair_plan(s)
    cq = np.arange(GRID_W)
    ck = np.arange(GRID_W)
    c_start = np.clip(cq - NA_COLS // 2, 0, GRID_W - NA_COLS)
    col_ok = (ck[None, :] >= c_start[:, None]) & (ck[None, :] < c_start[:, None] + NA_COLS)
    dc = np.clip(ck[None, :] - cq[:, None] + (NA_COLS - 1), 0, 2 * NA_COLS - 2)
    sel_c = (dc[:, :, None] == np.arange(2 * NA_COLS - 1)).astype(np.float32)
    t = jnp.einsum("lhrc,xyc->lhrxy", rpb.astype(F32), sel_c, precision=lax.Precision.HIGHEST)
    t = jnp.where(col_ok[None, None, None], t * LOG2E, NEG_INF)
    outside = jnp.full(t.shape[:2] + t.shape[3:], NEG_INF, F32)
    half = lambda off: outside if off < 0 else t[:, :, off]
    blocks = jnp.stack([jnp.concatenate([half(a), half(b)], axis=-1) for a, b in pair_list], axis=2)
    return blocks, jnp.asarray(ids)


def _ctx_attn_kernel(sink_ref, w_ref, n_ref, ow_ref, on_ref):
    lc = w_ref.shape[0]
    even = lax.broadcasted_iota(jnp.int32, (lc, LANES), 1) < 64
    g = WIN_HEADS // WIN_KV_HEADS
    for kv in range(WIN_KV_HEADS):
        k = w_ref[:, 1024 + kv * 128:1024 + (kv + 1) * 128]
        v = w_ref[:, 1280 + kv * 128:1280 + (kv + 1) * 128]
        qst = jnp.concatenate([w_ref[:, (g * kv + i) * 128:(g * kv + i + 1) * 128] for i in range(g)], axis=0)
        s = _dot_nt(qst, k)
        sm = [_softmax_parts([s[i * lc:(i + 1) * lc]], sink_ref[g * kv + i] * LOG2E) for i in range(g)]
        pv = _dot(jnp.concatenate([e[0][0] for e in sm], axis=0), v)
        for p in range(g // 2):
            ow_ref[:, (2 * kv + p) * 128:(2 * kv + p + 1) * 128] = jnp.where(
                even, pv[(2 * p) * lc:(2 * p + 1) * lc] * sm[2 * p][1],
                pv[(2 * p + 1) * lc:(2 * p + 2) * lc] * sm[2 * p + 1][1]).astype(BF16)
    for p in range(NA_HEADS // 2):
        k = n_ref[:, 1024 + p * 128:1024 + (p + 1) * 128]
        v = n_ref[:, 1536 + p * 128:1536 + (p + 1) * 128]
        qst = jnp.concatenate([n_ref[:, (2 * p) * 128:(2 * p + 1) * 128],
                               n_ref[:, (2 * p + 1) * 128:(2 * p + 2) * 128]], axis=0)
        s = _dot_nt(qst, k)
        sm = [_softmax_parts([s[i * lc:(i + 1) * lc]]) for i in range(2)]
        pv = _dot(jnp.concatenate([e[0][0] for e in sm], axis=0), v)
        on_ref[:, p * 128:(p + 1) * 128] = jnp.where(even, pv[:lc] * sm[0][1], pv[lc:] * sm[1][1]).astype(BF16)


def _ctx_attention(sink, pc_win, pc_na, batch):
    tc = pc_win.shape[0]
    lc = tc // batch
    row = lambda b: (b, 0)
    return pl.pallas_call(
        _ctx_attn_kernel,
        out_shape=(jax.ShapeDtypeStruct((tc, 512), BF16), jax.ShapeDtypeStruct((tc, 512), BF16)),
        grid=(batch,),
        in_specs=[pl.BlockSpec(memory_space=pltpu.SMEM),
                  pl.BlockSpec((lc, pc_win.shape[1]), row), pl.BlockSpec((lc, pc_na.shape[1]), row)],
        out_specs=(pl.BlockSpec((lc, 512), row), pl.BlockSpec((lc, 512), row)),
        compiler_params=_cparams(("parallel",)),
        name="context_attn",
    )(sink, pc_win, pc_na)


def _merge_kernel(h_ref, yr_ref, yw_ref, yn_ref, gt_ref, mod_ref, g_ref, wb_ref, wo_ref, wr_ref,
                  ho_ref, f_ref, idx_ref, gw_ref, *, with_router):
    z = None
    for r, y_ref in enumerate((yr_ref, yw_ref, yn_ref)):
        gate = _sigmoid(gt_ref[:, r * D_MODEL:(r + 1) * D_MODEL].astype(F32))
        t = gate * _dot(y_ref[...], wb_ref[r])
        z = t if z is None else z + t
    o = _dot(z.astype(BF16), wo_ref[...])
    h = h_ref[...] + mod_ref[0, 2:3, :] * _rms(o, g_ref[1:2, :])
    ho_ref[...] = h
    f = _rms(h, g_ref[2:3, :]) * (1.0 + mod_ref[0, 4:5, :]) + mod_ref[0, 3:4, :]
    f_ref[...] = f.astype(f_ref.dtype)
    if not with_router:
        idx_ref[...] = jnp.zeros_like(idx_ref)
        gw_ref[...] = jnp.zeros_like(gw_ref)
        return
    f_hi = f.astype(BF16)
    f_lo = (f - f_hi.astype(F32)).astype(BF16)
    wr = wr_ref[...]
    w_hi = wr.astype(BF16)
    w_lo = (wr - w_hi.astype(F32)).astype(BF16)
    logits = _dot(f_hi, w_hi) + (_dot(f_lo, w_hi) + _dot(f_hi, w_lo))
    idx = lax.broadcasted_iota(jnp.int32, logits.shape, 1)
    m1 = jnp.max(logits, axis=-1, keepdims=True)
    i1 = jnp.min(jnp.where(logits == m1, idx, N_EXPERTS), axis=-1, keepdims=True)
    rest = jnp.where(idx == i1, -jnp.inf, logits)
    m2 = jnp.max(rest, axis=-1, keepdims=True)
    i2 = jnp.min(jnp.where(rest == m2, idx, N_EXPERTS), axis=-1, keepdims=True)
    e2 = jnp.exp(m2 - m1)
    inv = 1.0 / (1.0 + e2)
    first = lax.broadcasted_iota(jnp.int32, idx_ref.shape, 1) == 0
    idx_ref[...] = jnp.where(first, i1, i2)
    gw_ref[...] = jnp.where(first, inv, e2 * inv)


def _merge(h, y_ret, y_win, y_na, gates, mod, gains, w_branch, w_out, w_router, *, with_router, is_ctx,
           per_batch, tm=256):
    t, d = h.shape
    f_dtype = F32 if with_router else BF16
    if is_ctx:
        mod_map = lambda i: (mod.shape[0] - 1, 0, 0)
    else:
        mod_map = lambda i: (i // (per_batch // tm), 0, 0)
    row = lambda i: (i, 0)
    full2 = lambda i: (0, 0)
    return pl.pallas_call(
        functools.partial(_merge_kernel, with_router=with_router),
        out_shape=(jax.ShapeDtypeStruct((t, d), F32), jax.ShapeDtypeStruct((t, d), f_dtype),
                   jax.ShapeDtypeStruct((t, TOP_K), jnp.int32), jax.ShapeDtypeStruct((t, TOP_K), F32)),
        grid=(t // tm,),
        in_specs=[pl.BlockSpec((tm, d), row),
                  pl.BlockSpec((tm, BRANCH_W), row), pl.BlockSpec((tm, BRANCH_W), row),
                  pl.BlockSpec((tm, BRANCH_W), row),
                  pl.BlockSpec((tm, N_BRANCH * d), row),
                  pl.BlockSpec((1, 6, d), mod_map),
                  pl.BlockSpec(gains.shape, full2),
                  pl.BlockSpec(w_branch.shape, lambda i: (0, 0, 0), pipeline_mode=pl.Buffered(1)),
                  pl.BlockSpec(w_out.shape, full2, pipeline_mode=pl.Buffered(1)),
                  pl.BlockSpec(w_router.shape, full2)],
        out_specs=(pl.BlockSpec((tm, d), row), pl.BlockSpec((tm, d), row),
                   pl.BlockSpec((tm, TOP_K), row), pl.BlockSpec((tm, TOP_K), row)),
        compiler_params=_cparams(("parallel",)),
        name="merge_ctx" if is_ctx else "merge_lat",
    )(h, y_ret, y_win, y_na, gates, mod, gains, w_branch, w_out, w_router)


def _swiglu(x, w1, w3, w2, act_ref):
    dff = act_ref.shape[1]
    for c0 in range(0, dff, FF_CHUNK):
        cw = min(FF_CHUNK, dff - c0)
        a1 = _dot(x, w1[:, c0:c0 + cw])
        a3 = _dot(x, w3[:, c0:c0 + cw])
        act_ref[:, c0:c0 + cw] = (a1 * _sigmoid(a1) * a3).astype(BF16)
    return _dot(act_ref[...], w2[...])


def _ffn_kernel(h_ref, f_ref, mod_ref, g_ref, w1_ref, w3_ref, w2_ref, o_ref, act_ref):
    y = _swiglu(f_ref[...], w1_ref, w3_ref, w2_ref, act_ref)
    o_ref[...] = h_ref[...] + mod_ref[0, 5:6, :] * _rms(y, g_ref[3:4, :])


def _ffn(h, f, mod, gains, w1, w3, w2, *, is_ctx, per_batch, tm=512):
    t, d = h.shape
    dff = w1.shape[1]
    if is_ctx:
        mod_map = lambda i: (mod.shape[0] - 1, 0, 0)
    else:
        mod_map = lambda i: (i // (per_batch // tm), 0, 0)
    row = lambda i: (i, 0)
    const = lambda shape: pl.BlockSpec(shape, lambda i: (0, 0), pipeline_mode=pl.Buffered(1))
    return pl.pallas_call(
        _ffn_kernel,
        out_shape=jax.ShapeDtypeStruct((t, d), F32),
        grid=(t // tm,),
        in_specs=[pl.BlockSpec((tm, d), row), pl.BlockSpec((tm, d), row),
                  pl.BlockSpec((1, 6, d), mod_map),
                  pl.BlockSpec(gains.shape, lambda i: (0, 0)),
                  const((d, dff)), const((d, dff)), const((dff, d))],
        out_specs=pl.BlockSpec((tm, d), row),
        scratch_shapes=[pltpu.VMEM((tm, dff), BF16)],
        compiler_params=_cparams(("parallel",)),
        name="ffn_ctx" if is_ctx else "ffn_lat",
    )(h, f, mod, gains, w1, w3, w2)


def _route_slots(idx, tile):
    n = idx.shape[0] * TOP_K
    e = idx.reshape(n)
    onehot = (e[:, None] == jnp.arange(N_EXPERTS, dtype=jnp.int32)).astype(jnp.int32)
    csum = jnp.cumsum(onehot, axis=0)
    counts = csum[-1]
    rank = jnp.sum(csum * onehot, axis=1) - 1
    padded = (counts + tile - 1) // tile * tile
    ends = jnp.cumsum(padded)
    starts = ends - padded
    pos = jnp.sum(onehot * starts[None, :], axis=1) + rank
    n_tiles = n // tile + N_EXPERTS
    tile_start = jnp.arange(n_tiles, dtype=jnp.int32) * tile
    tile_expert = jnp.minimum(jnp.sum((tile_start[:, None] >= ends[None, :]).astype(jnp.int32), axis=1),
                              N_EXPERTS - 1)
    n_active = (ends[-1] // tile).reshape(1)
    return pos.astype(jnp.int32), tile_expert.astype(jnp.int32), n_active.astype(jnp.int32)


def _dispatch_kernel(pos_ref, f_ref, xs_in_ref, xs_ref, sem):
    del xs_in_ref
    tm = f_ref.shape[0]

    def row_copy(r, k):
        p = pos_ref[0, 0, TOP_K * r + k]
        return pltpu.make_async_copy(f_ref.at[pl.ds(r, 1), :], xs_ref.at[pl.ds(p, 1), :], sem)

    def start(r, carry):
        for k in range(TOP_K):
            row_copy(r, k).start()
        return carry

    def wait(r, carry):
        for k in range(TOP_K):
            row_copy(r, k).wait()
        return carry

    lax.fori_loop(0, tm, start, 0, unroll=8)
    lax.fori_loop(0, tm, wait, 0, unroll=8)


def _dispatch(f, pos, xs, tm=512):
    t, d = f.shape
    pos3 = pos.reshape(t // tm, 1, TOP_K * tm)
    return pl.pallas_call(
        _dispatch_kernel,
        out_shape=jax.ShapeDtypeStruct(xs.shape, xs.dtype),
        grid=(t // tm,),
        in_specs=[pl.BlockSpec((1, 1, TOP_K * tm), lambda i: (i, 0, 0), memory_space=pltpu.SMEM),
                  pl.BlockSpec((tm, d), lambda i: (i, 0)),
                  pl.BlockSpec(memory_space=pl.ANY)],
        out_specs=pl.BlockSpec(memory_space=pl.ANY),
        scratch_shapes=[pltpu.SemaphoreType.DMA(())],
        input_output_aliases={2: 0},
        compiler_params=_cparams(("arbitrary",)),
        name="moe_dispatch",
    )(pos3, f, xs)


def _experts_kernel(te_ref, na_ref, x_ref, *refs, has_prev):
    del te_ref
    if has_prev:
        yprev_ref, w1_ref, w3_ref, w2_ref, y_ref, act_ref = refs
    else:
        w1_ref, w3_ref, w2_ref, y_ref, act_ref = refs
    active = pl.program_id(0) < na_ref[0]

    @pl.when(active)
    def _():
        y = _swiglu(x_ref[...].astype(BF16), w1_ref.at[0], w3_ref.at[0], w2_ref.at[0], act_ref)
        y_ref[...] = yprev_ref[...] + y if has_prev else y

    @pl.when(jnp.logical_not(active))
    def _():
        y_ref[...] = jnp.zeros_like(y_ref)


def _experts_pass(xs, y_prev, tile_expert, n_active, w1, w3, w2, *, tile, fc, c):
    p, d = xs.shape
    slot = pl.BlockSpec((tile, d), lambda j, te, na: (j, 0))
    in_specs = [slot] + ([slot] if y_prev is not None else []) + [
        pl.BlockSpec((1, d, fc), lambda j, te, na: (te[j], 0, c)),
        pl.BlockSpec((1, d, fc), lambda j, te, na: (te[j], 0, c)),
        pl.BlockSpec((1, fc, d), lambda j, te, na: (te[j], c, 0))]
    grid_spec = pltpu.PrefetchScalarGridSpec(
        num_scalar_prefetch=2, grid=(p // tile,), in_specs=in_specs, out_specs=slot,
        scratch_shapes=[pltpu.VMEM((tile, fc), BF16)])
    args = (xs,) + ((y_prev,) if y_prev is not None else ()) + (w1, w3, w2)
    return pl.pallas_call(
        functools.partial(_experts_kernel, has_prev=y_prev is not None),
        out_shape=jax.ShapeDtypeStruct((p, d), F32),
        grid_spec=grid_spec,
        compiler_params=_cparams(("arbitrary",)),
        name="moe_experts",
    )(tile_expert, n_active, *args)


def _experts(xs, tile_expert, n_active, w1, w3, w2, *, tile, fc):
    y = None
    for c in range(w1.shape[2] // fc):
        y = _experts_pass(xs, y, tile_expert, n_active, w1, w3, w2, tile=tile, fc=fc, c=c)
    return y


def _combine_kernel(pos_ref, gw_ref, h_ref, mod_ref, g_ref, y_any, o_ref, buf_ref, sem):
    tm = h_ref.shape[0]

    def row_copy(r, k):
        p = pos_ref[0, 0, TOP_K * r + k]
        return pltpu.make_async_copy(y_any.at[pl.ds(p, 1), :], buf_ref.at[k, pl.ds(r, 1), :], sem)

    def start(r, carry):
        for k in range(TOP_K):
            row_copy(r, k).start()
        return carry

    def wait(r, carry):
        for k in range(TOP_K):
            row_copy(r, k).wait()
        return carry

    lax.fori_loop(0, tm, start, 0, unroll=8)
    lax.fori_loop(0, tm, wait, 0, unroll=8)
    gw = gw_ref[...]
    y = gw[:, 0:1] * buf_ref[0] + gw[:, 1:2] * buf_ref[1]
    o_ref[...] = h_ref[...] + mod_ref[0, 5:6, :] * _rms(y, g_ref[3:4, :])


def _combine(h, pos, gw, y_sorted, mod, gains, *, is_ctx, per_batch, tm=512):
    t, d = h.shape
    pos3 = pos.reshape(t // tm, 1, TOP_K * tm)
    if is_ctx:
        mod_map = lambda i: (mod.shape[0] - 1, 0, 0)
    else:
        mod_map = lambda i: (i // (per_batch // tm), 0, 0)
    row = lambda i: (i, 0)
    return pl.pallas_call(
        _combine_kernel,
        out_shape=jax.ShapeDtypeStruct((t, d), F32),
        grid=(t // tm,),
        in_specs=[pl.BlockSpec((1, 1, TOP_K * tm), lambda i: (i, 0, 0), memory_space=pltpu.SMEM),
                  pl.BlockSpec((tm, TOP_K), row),
                  pl.BlockSpec((tm, d), row),
                  pl.BlockSpec((1, 6, d), mod_map),
                  pl.BlockSpec(gains.shape, lambda i: (0, 0)),
                  pl.BlockSpec(memory_space=pl.ANY)],
        out_specs=pl.BlockSpec((tm, d), row),
        scratch_shapes=[pltpu.VMEM((TOP_K, tm, d), F32), pltpu.SemaphoreType.DMA(())],
        compiler_params=_cparams(("arbitrary",)),
        name="moe_combine_ctx" if is_ctx else "moe_combine_lat",
    )(pos3, gw, h, mod, gains, y_sorted)


def _rope_tables(s, lc):
    def lin(pos):
        n_freq = RET_DK // 2
        inv = ROPE_BASE ** (-jnp.arange(n_freq, dtype=F32) / n_freq)
        ang = pos.astype(F32)[:, None] * inv
        cos, sin = jnp.cos(ang), jnp.sin(ang)
        return jnp.concatenate([cos, cos], -1), jnp.concatenate([-sin, sin], -1)

    t = jnp.arange(s)
    n_freq = WIN_HD // 4
    inv = ROPE_BASE ** (-jnp.arange(n_freq, dtype=F32) / n_freq)
    rowa = (t // GRID_W).astype(F32)[:, None] * inv
    cola = (t % GRID_W).astype(F32)[:, None] * inv
    ang = jnp.concatenate([rowa, cola], axis=-1)
    cos, sin = jnp.cos(ang), jnp.sin(ang)
    ca = jnp.tile(jnp.concatenate([cos, cos], -1), (1, 2))
    sa = jnp.tile(jnp.concatenate([-sin, sin], -1), (1, 2))
    cr_l, sr_l = lin(lc + jnp.arange(s))
    cr_c, sr_c = lin(jnp.arange(lc))
    return (cr_l, sr_l, ca, sa), (cr_c, sr_c, ca[:lc], sa[:lc])


def kernel(x, c, ctx, c_ctx, w_mod, b_mod, norm_gains, w_in, ret_decay, win_sink, na_rpb, w_branch, w_out,
           ffn_w1, ffn_w3, ffn_w2, moe_router, moe_w1, moe_w3, moe_w2):
    batch, s, d = x.shape
    lc = ctx.shape[1]
    depth = w_mod.shape[0]

    c_all = jnp.zeros((16, d), F32).at[:batch].set(c).at[batch].set(c_ctx)
    mods = _modulation(c_all, w_mod, b_mod)[:, :batch + 1].reshape(depth, batch + 1, 6, d)

    tabs_l, tabs_c = _rope_tables(s, lc)
    log_gamma = jnp.log1p(-jnp.exp(ret_decay.astype(F32)))
    na_blocks, na_ids = _na_bias_pairs(na_rpb, s)

    h_l = x.reshape(batch * s, d)
    h_c = ctx.reshape(batch * lc, d)
    zero_router = jnp.zeros((d, N_EXPERTS), F32)

    for layer in range(depth):
        need_ctx = layer < depth - 1
        mod = mods[layer]
        gains = norm_gains[layer]
        w_in_b = w_in[layer].astype(BF16)
        wb_b = w_branch[layer].astype(BF16)
        wo_b = w_out[layer].astype(BF16)
        i = layer // 2
        is_moe = layer % 2 == 1
        if is_moe:
            w1, w3, w2 = moe_w1[i].astype(BF16), moe_w3[i].astype(BF16), moe_w2[i].astype(BF16)
            w_router = moe_router[i]
        else:
            w1, w3, w2 = ffn_w1[i].astype(BF16), ffn_w3[i].astype(BF16), ffn_w2[i].astype(BF16)
            w_router = zero_router

        p_ret_l, p_win_l, p_na_l, p_gt_l = _inproj(h_l, mod, gains, w_in_b, *tabs_l, is_ctx=False, tm=512)
        p_ret_c, p_win_c, p_na_c, p_gt_c = _inproj(h_c, mod, gains, w_in_b, *tabs_c, is_ctx=True, tm=256)

        y_ret_l, y_ret_c = _retention(log_gamma[layer], p_ret_l, p_ret_c, batch)
        y_win_l = _window(win_sink[layer], p_win_l, p_win_c, batch)
        y_na_l = _neighbourhood(p_na_l, p_na_c, na_blocks, na_ids, layer, batch)

        h_l, f_l, idx_l, gw_l = _merge(h_l, y_ret_l, y_win_l, y_na_l, p_gt_l, mod, gains, wb_b, wo_b,
                                       w_router, with_router=is_moe, is_ctx=False, per_batch=s, tm=512)
        if need_ctx:
            y_win_c, y_na_c = _ctx_attention(win_sink[layer], p_win_c, p_na_c, batch)
            h_c, f_c, idx_c, gw_c = _merge(h_c, y_ret_c, y_win_c, y_na_c, p_gt_c, mod, gains, wb_b, wo_b,
                                           w_router, with_router=is_moe, is_ctx=True, per_batch=lc, tm=512)

        if not is_moe:
            h_l = _ffn(h_l, f_l, mod, gains, w1, w3, w2, is_ctx=False, per_batch=s)
            if need_ctx:
                h_c = _ffn(h_c, f_c, mod, gains, w1, w3, w2, is_ctx=True, per_batch=lc)
            continue

        idx_all = jnp.concatenate([idx_l, idx_c], axis=0) if need_ctx else idx_l
        pos, tile_expert, n_active = _route_slots(idx_all, MOE_TILE)
        n_slots = tile_expert.shape[0] * MOE_TILE
        xs = jnp.zeros((n_slots, d), F32)
        n_l = TOP_K * h_l.shape[0]
        xs = _dispatch(f_l, pos[:n_l], xs)
        if need_ctx:
            xs = _dispatch(f_c, pos[n_l:], xs)
        y_sorted = _experts(xs, tile_expert, n_active, w1, w3, w2, tile=MOE_TILE, fc=MOE_FC)
        h_l = _combine(h_l, pos[:n_l], gw_l, y_sorted, mod, gains, is_ctx=False, per_batch=s)
        if need_ctx:
            h_c = _combine(h_c, pos[n_l:], gw_c, y_sorted, mod, gains, is_ctx=True, per_batch=lc)
    return h_l.reshape(batch, s, d)
```

```python
import functools
import math

import jax
import jax.numpy as jnp
import numpy as np
from jax import lax
from jax.experimental import pallas as pl
from jax.experimental.pallas import tpu as pltpu

F32 = jnp.float32
BF16 = jnp.bfloat16

D_MODEL = 1024
GRID_W = 64
RET_HEADS = 4
RET_DK = 128
WIN_HEADS = 8
WIN_KV_HEADS = 2
WIN_HD = 64
WINDOW = 128
WIN_BLOCK = 128
NA_HEADS = 8
NA_HD = 64
NA_ROWS_MAX = 8
NA_COLS = 16
BRANCH_W = 512
N_BRANCH = 3
N_EXPERTS = 8
TOP_K = 2
ROPE_BASE = 10000.0
NORM_EPS = 1e-6
NEG_INF = -1e30
LOG2E = math.log2(math.e)

LANES = 128
VMEM_LIMIT = 56 * 1024 * 1024

C_RQ, C_RK, C_RV, C_RG = 0, 512, 1024, 1536
C_WQ, C_WK, C_WV = 2048, 2560, 2688
C_NQ, C_NK, C_NV = 2816, 3328, 3840
C_GATE = 4352
D_IN = 7424

RET_CHUNK = 256
NA_QROWS = 4
NA_KROWS = 12
ROW_BLK = 16
FF_CHUNK = 256
MOE_TILE = 512
MOE_FC = 1792


def _cparams(sem):
    return pltpu.CompilerParams(dimension_semantics=sem, vmem_limit_bytes=VMEM_LIMIT)


def _rms(x, g):
    return x * lax.rsqrt(jnp.mean(x * x, axis=-1, keepdims=True) + NORM_EPS) * g


def _sigmoid(x):
    return 0.5 * jnp.tanh(0.5 * x) + 0.5


def _dot(a, b):
    return jnp.dot(a, b, preferred_element_type=F32)


def _dot_nt(a, b):
    return lax.dot_general(a, b, (((1,), (1,)), ((), ())), preferred_element_type=F32)


def _dot_tn(a, b):
    return lax.dot_general(a, b, (((0,), (0,)), ((), ())), preferred_element_type=F32)


def _mod_kernel(c_ref, w_ref, b_ref, o_ref):
    c = c_ref[...]
    s = c * jax.nn.sigmoid(c)
    o_ref[0] = jnp.dot(s, w_ref[0], preferred_element_type=F32,
                       precision=lax.Precision.HIGHEST) + b_ref[0]


def _modulation(c_all, w_mod, b_mod):
    depth, d, n = w_mod.shape
    tn = 1536
    return pl.pallas_call(
        _mod_kernel,
        out_shape=jax.ShapeDtypeStruct((depth, c_all.shape[0], n), F32),
        grid=(depth, n // tn),
        in_specs=[pl.BlockSpec(c_all.shape, lambda l, j: (0, 0)),
                  pl.BlockSpec((1, d, tn), lambda l, j: (l, 0, j)),
                  pl.BlockSpec((1, 1, tn), lambda l, j: (l, 0, j))],
        out_specs=pl.BlockSpec((1, c_all.shape[0], tn), lambda l, j: (l, 0, j)),
        compiler_params=_cparams(("arbitrary", "arbitrary")),
        name="modulation",
    )(c_all, w_mod, b_mod.reshape(depth, 1, n))


def _inproj_kernel(h_ref, mod_ref, g_ref, w_ref, cr_ref, sr_ref, ca_ref, sa_ref,
                   oret_ref, owin_ref, ona_ref, og_ref, *, rope_win):
    x = h_ref[...]
    a = _rms(x, g_ref[0:1, :]) * (1.0 + mod_ref[0, 1:2, :]) + mod_ref[0, 0:1, :]
    a = a.astype(BF16)
    tm = x.shape[0]
    lane = lax.broadcasted_iota(jnp.int32, (tm, LANES), 1)
    even = lane < 64
    first_half = (lane % 64) < 32

    def mm(c0, cw):
        return _dot(a, w_ref[:, c0:c0 + cw])

    def rope_lin(v):
        return v * cr_ref[...] + pltpu.roll(v, 64, 1) * sr_ref[...]

    def rope_ax(v):
        rot = jnp.where(first_half, pltpu.roll(v, 96, 1), pltpu.roll(v, 32, 1))
        return v * ca_ref[...] + rot * sa_ref[...]

    r = mm(C_RQ, 512)
    for h in range(RET_HEADS):
        blk = rope_lin(r[:, h * 128:(h + 1) * 128]) * (RET_DK ** -0.5)
        oret_ref[:, h * 128:(h + 1) * 128] = blk.astype(BF16)
    r = mm(C_RK, 512)
    for h in range(RET_HEADS):
        blk = rope_lin(r[:, h * 128:(h + 1) * 128])
        oret_ref[:, 512 + h * 128:512 + (h + 1) * 128] = blk.astype(BF16)
    oret_ref[:, 1024:1536] = mm(C_RV, 512).astype(BF16)
    oret_ref[:, 1536:2048] = mm(C_RG, 512).astype(BF16)

    r = mm(C_WQ, 512)
    for p in range(WIN_HEADS // 2):
        blk = r[:, p * 128:(p + 1) * 128]
        if rope_win:
            blk = rope_ax(blk)
        blk = blk * (WIN_HD ** -0.5 * LOG2E)
        owin_ref[:, (2 * p) * 128:(2 * p + 1) * 128] = jnp.where(even, blk, 0.0).astype(BF16)
        owin_ref[:, (2 * p + 1) * 128:(2 * p + 2) * 128] = jnp.where(even, 0.0, blk).astype(BF16)
    r = mm(C_WK, 256)
    k2 = r[:, 0:128]
    if rope_win:
        k2 = rope_ax(k2)
    v2 = r[:, 128:256]
    for i, t in enumerate((k2, v2)):
        sw = pltpu.roll(t, 64, 1)
        owin_ref[:, 1024 + 256 * i:1024 + 256 * i + 128] = jnp.where(even, t, sw).astype(BF16)
        owin_ref[:, 1024 + 256 * i + 128:1024 + 256 * i + 256] = jnp.where(even, sw, t).astype(BF16)

    r = mm(C_NQ, 512) * (NA_HD ** -0.5 * LOG2E)
    for p in range(NA_HEADS // 2):
        blk = r[:, p * 128:(p + 1) * 128]
        ona_ref[:, (2 * p) * 128:(2 * p + 1) * 128] = jnp.where(even, blk, 0.0).astype(BF16)
        ona_ref[:, (2 * p + 1) * 128:(2 * p + 2) * 128] = jnp.where(even, 0.0, blk).astype(BF16)
    ona_ref[:, 1024:1536] = mm(C_NK, 512).astype(BF16)
    ona_ref[:, 1536:2048] = mm(C_NV, 512).astype(BF16)

    for j in range(6):
        og_ref[:, j * 512:(j + 1) * 512] = _sigmoid(mm(C_GATE + j * 512, 512)).astype(BF16)


def _inproj(h, mod, gains, w_in, cr, sr, ca, sa, *, is_ctx, tm=256):
    t, d = h.shape
    nt = t // tm
    per_batch = cr.shape[0] // tm
    if is_ctx:
        mod_map = lambda i: (mod.shape[0] - 1, 0, 0)
    else:
        mod_map = lambda i: (i // per_batch, 0, 0)
    pos_map = lambda i: (i % per_batch, 0)
    full = lambda i: (0, 0)
    row = lambda i: (i, 0)
    outs = (jax.ShapeDtypeStruct((t, 2048), BF16), jax.ShapeDtypeStruct((t, 1536), BF16),
            jax.ShapeDtypeStruct((t, 2048), BF16), jax.ShapeDtypeStruct((t, 3072), BF16))
    return pl.pallas_call(
        functools.partial(_inproj_kernel, rope_win=not is_ctx),
        out_shape=outs,
        grid=(nt,),
        in_specs=[pl.BlockSpec((tm, d), row),
                  pl.BlockSpec((1, 6, d), mod_map),
                  pl.BlockSpec(gains.shape, full),
                  pl.BlockSpec(w_in.shape, full, pipeline_mode=pl.Buffered(1)),
                  pl.BlockSpec((tm, LANES), pos_map), pl.BlockSpec((tm, LANES), pos_map),
                  pl.BlockSpec((tm, LANES), pos_map), pl.BlockSpec((tm, LANES), pos_map)],
        out_specs=(pl.BlockSpec((tm, 2048), row), pl.BlockSpec((tm, 1536), row),
                   pl.BlockSpec((tm, 2048), row), pl.BlockSpec((tm, 3072), row)),
        compiler_params=_cparams(("parallel",)),
        name="inproj_ctx" if is_ctx else "inproj_lat",
    )(h, mod, gains, w_in, cr, sr, ca, sa)


def _ret_kernel(lg_ref, ql, kl, vl, gl, qc, kc, vc, gc, yl_ref, yc_ref, acc_ref):
    for h in range(RET_HEADS):
        _ret_head(h, lg_ref, ql, kl, vl, gl, qc, kc, vc, gc, yl_ref, yc_ref, acc_ref)


def _ret_head(h, lg_ref, ql, kl, vl, gl, qc, kc, vc, gc, yl_ref, yc_ref, acc_ref):
    C = RET_CHUNK
    hd = slice(h * RET_DK, (h + 1) * RET_DK)
    lgf = lg_ref[0, h]
    lgb = lg_ref[1, h]
    ii = lax.broadcasted_iota(jnp.int32, (C, C), 0).astype(F32)
    jj = lax.broadcasted_iota(jnp.int32, (C, C), 1).astype(F32)
    diff = ii - jj
    dmat = jnp.where(diff >= 0, jnp.exp(lgf * jnp.maximum(diff, 0.0)),
                     jnp.exp(lgb * jnp.maximum(-diff, 0.0)))
    r = lax.broadcasted_iota(jnp.int32, (C, RET_DK), 0).astype(F32)
    qdf = jnp.exp(lgf * (r + 1.0))
    kdf = jnp.exp(lgf * (C - 1.0 - r))
    qdb = jnp.exp(lgb * (C - r))
    kdb = jnp.exp(lgb * r)
    ones = jnp.ones((RET_DK, RET_DK), F32)
    cdf = jnp.exp(ones * (lgf * C))
    cdb = jnp.exp(ones * (lgb * C))

    n_lat = ql.shape[0] // C
    lc = qc.shape[0]
    chunks = [(qc, kc, vc, gc, yc_ref, 0, 0)]
    chunks += [(ql, kl, vl, gl, yl_ref, j * C, lc + j * C) for j in range(n_lat)]

    state = jnp.zeros((RET_DK, RET_DK), F32)
    for (qr, kr, vr, _, _, off, aoff) in chunks:
        q = qr[off:off + C, hd]
        k = kr[off:off + C, hd]
        v = vr[off:off + C, hd]
        s = _dot_nt(q, k) * dmat
        y = _dot(s.astype(BF16), v)
        y = y + _dot((q.astype(F32) * qdf).astype(BF16), state.astype(BF16))
        state = state * cdf + _dot_tn((k.astype(F32) * kdf).astype(BF16), v)
        acc_ref[aoff:aoff + C, hd] = y

    state = jnp.zeros((RET_DK, RET_DK), F32)
    for (qr, kr, vr, gr, yr, off, aoff) in [chunks[0]] + chunks[:0:-1]:
        q = qr[off:off + C, hd]
        k = kr[off:off + C, hd]
        v = vr[off:off + C, hd]
        y = acc_ref[aoff:aoff + C, hd] + _dot((q.astype(F32) * qdb).astype(BF16), state.astype(BF16))
        state = state * cdb + _dot_tn((k.astype(F32) * kdb).astype(BF16), v)
        mu = jnp.mean(y, axis=-1, keepdims=True)
        dlt = y - mu
        var = jnp.mean(dlt * dlt, axis=-1, keepdims=True)
        yn = dlt * lax.rsqrt(var + NORM_EPS)
        g = gr[off:off + C, hd].astype(F32)
        yr[off:off + C, hd] = (g * _sigmoid(g) * yn).astype(BF16)


def _retention(lg, pl_ret, pc_ret, batch):
    tl = pl_ret.shape[0]
    tc = pc_ret.shape[0]
    s = tl // batch
    lc = tc // batch
    assert lc == RET_CHUNK and s % RET_CHUNK == 0
    w = RET_HEADS * RET_DK
    lat = lambda cb: pl.BlockSpec((s, w), lambda b, cb=cb: (b, cb))
    ctx = lambda cb: pl.BlockSpec((lc, w), lambda b, cb=cb: (b, cb))
    return pl.pallas_call(
        _ret_kernel,
        out_shape=(jax.ShapeDtypeStruct((tl, w), BF16), jax.ShapeDtypeStruct((tc, w), BF16)),
        grid=(batch,),
        in_specs=[pl.BlockSpec(memory_space=pltpu.SMEM),
                  lat(0), lat(1), lat(2), lat(3), ctx(0), ctx(1), ctx(2), ctx(3)],
        out_specs=(pl.BlockSpec((s, w), lambda b: (b, 0)),
                   pl.BlockSpec((lc, w), lambda b: (b, 0))),
        scratch_shapes=[pltpu.VMEM((s + lc, w), F32)],
        compiler_params=_cparams(("parallel",)),
        name="retention",
    )(lg, pl_ret, pl_ret, pl_ret, pl_ret, pc_ret, pc_ret, pc_ret, pc_ret)


def _softmax_parts(parts, sink=None):
    m = functools.reduce(jnp.maximum, [jnp.max(p, axis=-1, keepdims=True) for p in parts])
    if sink is not None:
        m = jnp.maximum(m, sink)
    es = [jnp.exp2(p - m) for p in parts]
    den = functools.reduce(lambda a, b: a + b, [jnp.sum(e, axis=-1, keepdims=True) for e in es])
    if sink is not None:
        den = den + jnp.exp2(sink - m)
    return [e.astype(BF16) for e in es], 1.0 / den


def _softmax_rows(s_ref, p_ref, sink_ref, row0, nrows, split, bias_rows, sink=None):
    for r in range(0, nrows, ROW_BLK):
        rows = slice(row0 + r, row0 + r + ROW_BLK)
        sb = s_ref[rows, :]
        parts = [sb[:, :split] + bias_rows(r), sb[:, split:]]
        m = functools.reduce(jnp.maximum, [jnp.max(p, axis=-1, keepdims=True) for p in parts])
        if sink is not None:
            m = jnp.maximum(m, sink)
            sink_ref[rows, :] = jnp.broadcast_to(jnp.exp2(sink - m), (ROW_BLK, LANES))
        p_ref[rows, :] = jnp.concatenate([jnp.exp2((p - m).astype(BF16)) for p in parts], axis=1)


def _weighted_values(p, v, extra=None):
    r = _dot(p, jnp.concatenate([v, jnp.ones(v.shape, BF16)], axis=1))
    den = r[:, LANES:]
    if extra is not None:
        den = den + extra
    return r[:, :LANES] * (1.0 / den)


def _win_kernel(sink_ref, q_ref, k0, k1, v0, v1, ck0, ck1, cv0, cv1, o_ref, s_ref, p_ref, inv_ref):
    blk = WIN_BLOCK
    nw = 3 * blk
    s_len = k0.shape[0]
    even = lax.broadcasted_iota(jnp.int32, (blk, LANES), 1) < 64
    g = WIN_HEADS // WIN_KV_HEADS
    row_iota = lax.broadcasted_iota(jnp.int32, (ROW_BLK, nw), 0)
    col_iota = lax.broadcasted_iota(jnp.int32, (ROW_BLK, nw), 1)
    for jj in range(q_ref.shape[0] // blk):
        rows = slice(jj * blk, (jj + 1) * blk)
        j = pl.program_id(1) * (q_ref.shape[0] // blk) + jj
        start = pl.multiple_of(jnp.clip((j - 1) * blk, 0, s_len - nw), blk)
        rel = (start - j * blk) + col_iota - row_iota

        def band(r, rel=rel):
            return jnp.where(jnp.abs(rel - r) <= WINDOW, 0.0, NEG_INF)

        for kv, (kr, vr, ckr, cvr) in enumerate(((k0, v0, ck0, cv0), (k1, v1, ck1, cv1))):
            c = WIN_KV_HEADS * jj + kv
            kcat = jnp.concatenate([kr[pl.ds(start, nw), :], ckr[...]], axis=0)
            vcat = jnp.concatenate([vr[pl.ds(start, nw), :], cvr[...]], axis=0)
            qst = jnp.concatenate(
                [q_ref[rows, (g * kv + i) * 128:(g * kv + i + 1) * 128] for i in range(g)], axis=0)
            s_ref[c] = _dot_nt(qst, kcat)
            for i in range(g):
                _softmax_rows(s_ref.at[c], p_ref.at[c], inv_ref.at[c], i * blk, blk, nw, band,
                              sink_ref[g * kv + i] * LOG2E)
            pv = _weighted_values(p_ref[c], vcat, inv_ref[c])
            for p in range(g // 2):
                o_ref[rows, (2 * kv + p) * 128:(2 * kv + p + 1) * 128] = jnp.where(
                    even, pv[(2 * p) * blk:(2 * p + 1) * blk],
                    pv[(2 * p + 1) * blk:(2 * p + 2) * blk]).astype(BF16)


def _window(sink, pl_win, pc_win, batch):
    tl = pl_win.shape[0]
    tc = pc_win.shape[0]
    s = tl // batch
    lc = tc // batch
    tq = 2 * WIN_BLOCK
    nb = s // tq
    n_chain = (tq // WIN_BLOCK) * WIN_KV_HEADS
    m_rows = (WIN_HEADS // WIN_KV_HEADS) * WIN_BLOCK
    n_keys = 3 * WIN_BLOCK + lc
    lat = lambda cb: pl.BlockSpec((s, 128), lambda b, j, cb=cb: (b, cb))
    ctx = lambda cb: pl.BlockSpec((lc, 128), lambda b, j, cb=cb: (b, cb))
    return pl.pallas_call(
        _win_kernel,
        out_shape=jax.ShapeDtypeStruct((tl, 512), BF16),
        grid=(batch, nb),
        in_specs=[pl.BlockSpec(memory_space=pltpu.SMEM),
                  pl.BlockSpec((tq, 1024), lambda b, j: (b * nb + j, 0)),
                  lat(8), lat(9), lat(10), lat(11), ctx(8), ctx(9), ctx(10), ctx(11)],
        out_specs=pl.BlockSpec((tq, 512), lambda b, j: (b * nb + j, 0)),
        scratch_shapes=[pltpu.VMEM((n_chain, m_rows, n_keys), F32),
                        pltpu.VMEM((n_chain, m_rows, n_keys), BF16),
                        pltpu.VMEM((n_chain, m_rows, LANES), F32)],
        compiler_params=_cparams(("parallel", "arbitrary")),
        name="window_attn",
    )(sink, pl_win, pl_win, pl_win, pl_win, pl_win, pc_win, pc_win, pc_win, pc_win)


def _na_kernel(ids_ref, q_ref, ka, kb, kc, va, vb, vc, ck, cv, pb_ref, o_ref, s_ref, p_ref):
    nq = q_ref.shape[0]
    n_nb = ka.shape[0] * 3
    even = lax.broadcasted_iota(jnp.int32, (nq, LANES), 1) < 64
    i = pl.program_id(0)
    pat = jnp.where(i == 0, 0, jnp.where(i == pl.num_programs(0) - 1, 2, 1))

    for p in range(NA_HEADS // 2):
        lanes = slice(p * 128, (p + 1) * 128)
        qst = jnp.concatenate([q_ref[:, (2 * p) * 128:(2 * p + 1) * 128],
                               q_ref[:, (2 * p + 1) * 128:(2 * p + 2) * 128]], axis=0)
        kcat = jnp.concatenate([ka[:, lanes], kb[:, lanes], kc[:, lanes], ck[:, lanes]], axis=0)
        vcat = jnp.concatenate([va[:, lanes], vb[:, lanes], vc[:, lanes], cv[:, lanes]], axis=0)
        s_ref[p] = _dot_nt(qst, kcat)
        for par in range(2):
            for qr in range(NA_QROWS):
                blocks = [pb_ref.at[0, 2 * p + par, ids_ref[pat, qr, kp]] for kp in range(NA_KROWS // 2)]

                def bias_rows(r, blocks=blocks):
                    return jnp.concatenate([b[r:r + ROW_BLK, :] for b in blocks], axis=1)

                _softmax_rows(s_ref.at[p], p_ref.at[p], None, par * nq + qr * GRID_W, GRID_W,
                              n_nb, bias_rows)
        pv = _weighted_values(p_ref[p], vcat)
        o_ref[:, lanes] = jnp.where(even, pv[:nq], pv[nq:]).astype(BF16)


def _neighbourhood(pl_na, pc_na, bias_blocks, bias_ids, layer, batch):
    tl = pl_na.shape[0]
    tc = pc_na.shape[0]
    s = tl // batch
    lc = tc // batch
    nq = NA_QROWS * GRID_W
    ng = s // nq
    kblk = nq
    nkb = s // kblk
    assert NA_KROWS * GRID_W == 3 * kblk

    def kspec(cb, d):
        return pl.BlockSpec(
            (kblk, 512), lambda i, b, d=d, cb=cb: (b * nkb + jnp.clip(i - 1, 0, nkb - 3) + d, cb))

    return pl.pallas_call(
        _na_kernel,
        out_shape=jax.ShapeDtypeStruct((tl, 512), BF16),
        grid=(ng, batch),
        in_specs=[pl.BlockSpec(memory_space=pltpu.SMEM),
                  pl.BlockSpec((nq, 1024), lambda i, b: (b * ng + i, 0)),
                  kspec(2, 0), kspec(2, 1), kspec(2, 2),
                  kspec(3, 0), kspec(3, 1), kspec(3, 2),
                  pl.BlockSpec((lc, 512), lambda i, b: (b, 2)),
                  pl.BlockSpec((lc, 512), lambda i, b: (b, 3)),
                  pl.BlockSpec((1,) + bias_blocks.shape[1:], lambda i, b: (layer, 0, 0, 0, 0),
                               pipeline_mode=pl.Buffered(1))],
        out_specs=pl.BlockSpec((nq, 512), lambda i, b: (b * ng + i, 0)),
        scratch_shapes=[pltpu.VMEM((NA_HEADS // 2, 2 * nq, 3 * kblk + lc), F32),
                        pltpu.VMEM((NA_HEADS // 2, 2 * nq, 3 * kblk + lc), BF16)],
        compiler_params=_cparams(("parallel", "arbitrary")),
        name="neighbourhood_attn",
    )(bias_ids, pl_na, pl_na, pl_na, pl_na, pl_na, pl_na, pl_na, pc_na, pc_na, bias_blocks)


def _na_pair_plan(s):
    rows = s // GRID_W
    kr = min(NA_ROWS_MAX, rows)
    ng = rows // NA_QROWS
    pairs = {}
    ids = np.zeros((3, NA_QROWS, NA_KROWS // 2), np.int32)
    for pi, i in enumerate((0, 1, ng - 1)):
        k0 = NA_QROWS * int(np.clip(i - 1, 0, ng - 3))
        for qr in range(NA_QROWS):
            r = NA_QROWS * i + qr
            r_start = int(np.clip(r - kr // 2, 0, rows - kr))
            for kp in range(NA_KROWS // 2):
                offs = tuple(
                    krow - r + NA_ROWS_MAX - 1 if r_start <= krow < r_start + kr else -1
                    for krow in (k0 + 2 * kp, k0 + 2 * kp + 1))
                ids[pi, qr, kp] = pairs.setdefault(offs, len(pairs))
    return list(pairs), ids


def _na_bias_pairs(rpb, s):
    pair_list, ids = _na_pair_plan(s)
    cq = np.arange(GRID_W)
    ck = np.arange(GRID_W)
    c_start = np.clip(cq - NA_COLS // 2, 0, GRID_W - NA_COLS)
    col_ok = (ck[None, :] >= c_start[:, None]) & (ck[None, :] < c_start[:, None] + NA_COLS)
    dc = np.clip(ck[None, :] - cq[:, None] + (NA_COLS - 1), 0, 2 * NA_COLS - 2)
    sel_c = (dc[:, :, None] == np.arange(2 * NA_COLS - 1)).astype(np.float32)
    t = jnp.einsum("lhrc,xyc->lhrxy", rpb.astype(F32), sel_c, precision=lax.Precision.HIGHEST)
    t = jnp.where(col_ok[None, None, None], t * LOG2E, NEG_INF)
    outside = jnp.full(t.shape[:2] + t.shape[3:], NEG_INF, F32)
    half = lambda off: outside if off < 0 else t[:, :, off]
    blocks = jnp.stack([jnp.concatenate([half(a), half(b)], axis=-1) for a, b in pair_list], axis=2)
    return blocks, jnp.asarray(ids)


def _ctx_attn_kernel(sink_ref, w_ref, n_ref, ow_ref, on_ref):
    lc = w_ref.shape[0]
    even = lax.broadcasted_iota(jnp.int32, (lc, LANES), 1) < 64
    g = WIN_HEADS // WIN_KV_HEADS
    for kv in range(WIN_KV_HEADS):
        k = w_ref[:, 1024 + kv * 128:1024 + (kv + 1) * 128]
        v = w_ref[:, 1280 + kv * 128:1280 + (kv + 1) * 128]
        qst = jnp.concatenate([w_ref[:, (g * kv + i) * 128:(g * kv + i + 1) * 128] for i in range(g)], axis=0)
        s = _dot_nt(qst, k)
        sm = [_softmax_parts([s[i * lc:(i + 1) * lc]], sink_ref[g * kv + i] * LOG2E) for i in range(g)]
        pv = _dot(jnp.concatenate([e[0][0] for e in sm], axis=0), v)
        for p in range(g // 2):
            ow_ref[:, (2 * kv + p) * 128:(2 * kv + p + 1) * 128] = jnp.where(
                even, pv[(2 * p) * lc:(2 * p + 1) * lc] * sm[2 * p][1],
                pv[(2 * p + 1) * lc:(2 * p + 2) * lc] * sm[2 * p + 1][1]).astype(BF16)
    for p in range(NA_HEADS // 2):
        k = n_ref[:, 1024 + p * 128:1024 + (p + 1) * 128]
        v = n_ref[:, 1536 + p * 128:1536 + (p + 1) * 128]
        qst = jnp.concatenate([n_ref[:, (2 * p) * 128:(2 * p + 1) * 128],
                               n_ref[:, (2 * p + 1) * 128:(2 * p + 2) * 128]], axis=0)
        s = _dot_nt(qst, k)
        sm = [_softmax_parts([s[i * lc:(i + 1) * lc]]) for i in range(2)]
        pv = _dot(jnp.concatenate([e[0][0] for e in sm], axis=0), v)
        on_ref[:, p * 128:(p + 1) * 128] = jnp.where(even, pv[:lc] * sm[0][1], pv[lc:] * sm[1][1]).astype(BF16)


def _ctx_attention(sink, pc_win, pc_na, batch):
    tc = pc_win.shape[0]
    lc = tc // batch
    row = lambda b: (b, 0)
    return pl.pallas_call(
        _ctx_attn_kernel,
        out_shape=(jax.ShapeDtypeStruct((tc, 512), BF16), jax.ShapeDtypeStruct((tc, 512), BF16)),
        grid=(batch,),
        in_specs=[pl.BlockSpec(memory_space=pltpu.SMEM),
                  pl.BlockSpec((lc, pc_win.shape[1]), row), pl.BlockSpec((lc, pc_na.shape[1]), row)],
        out_specs=(pl.BlockSpec((lc, 512), row), pl.BlockSpec((lc, 512), row)),
        compiler_params=_cparams(("parallel",)),
        name="context_attn",
    )(sink, pc_win, pc_na)


def _merge_kernel(h_ref, yr_ref, yw_ref, yn_ref, gt_ref, mod_ref, g_ref, wb_ref, wo_ref, wr_ref,
                  ho_ref, f_ref, idx_ref, gw_ref, *, with_router):
    n_split = 2
    rows_per = h_ref.shape[0] // n_split
    for part in range(n_split):
        _merge_rows(slice(part * rows_per, (part + 1) * rows_per), h_ref, yr_ref, yw_ref, yn_ref, gt_ref,
                    mod_ref, g_ref, wb_ref, wo_ref, wr_ref, ho_ref, f_ref, idx_ref, gw_ref, with_router)


def _merge_rows(rows, h_ref, yr_ref, yw_ref, yn_ref, gt_ref, mod_ref, g_ref, wb_ref, wo_ref, wr_ref,
                ho_ref, f_ref, idx_ref, gw_ref, with_router):
    z = None
    for r, y_ref in enumerate((yr_ref, yw_ref, yn_ref)):
        gate = gt_ref[rows, r * D_MODEL:(r + 1) * D_MODEL].astype(F32)
        t = gate * _dot(y_ref[rows, :], wb_ref[r])
        z = t if z is None else z + t
    o = _dot(z.astype(BF16), wo_ref[...])
    h = h_ref[rows, :] + mod_ref[0, 2:3, :] * _rms(o, g_ref[1:2, :])
    ho_ref[rows, :] = h
    f = _rms(h, g_ref[2:3, :]) * (1.0 + mod_ref[0, 4:5, :]) + mod_ref[0, 3:4, :]
    f_ref[rows, :] = f.astype(f_ref.dtype)
    if not with_router:
        idx_ref[rows, :] = jnp.zeros((rows.stop - rows.start, TOP_K), jnp.int32)
        gw_ref[rows, :] = jnp.zeros((rows.stop - rows.start, TOP_K), F32)
        return
    f_hi = f.astype(BF16)
    f_lo = (f - f_hi.astype(F32)).astype(BF16)
    wr = wr_ref[...]
    w_hi = wr.astype(BF16)
    w_lo = (wr - w_hi.astype(F32)).astype(BF16)
    logits = _dot(f_hi, w_hi) + (_dot(f_lo, w_hi) + _dot(f_hi, w_lo))
    idx = lax.broadcasted_iota(jnp.int32, logits.shape, 1)
    m1 = jnp.max(logits, axis=-1, keepdims=True)
    i1 = jnp.min(jnp.where(logits == m1, idx, N_EXPERTS), axis=-1, keepdims=True)
    rest = jnp.where(idx == i1, -jnp.inf, logits)
    m2 = jnp.max(rest, axis=-1, keepdims=True)
    i2 = jnp.min(jnp.where(rest == m2, idx, N_EXPERTS), axis=-1, keepdims=True)
    e2 = jnp.exp(m2 - m1)
    inv = 1.0 / (1.0 + e2)
    first = lax.broadcasted_iota(jnp.int32, (logits.shape[0], TOP_K), 1) == 0
    idx_ref[rows, :] = jnp.where(first, i1, i2)
    gw_ref[rows, :] = jnp.where(first, inv, e2 * inv)


def _merge(h, y_ret, y_win, y_na, gates, mod, gains, w_branch, w_out, w_router, *, with_router, is_ctx,
           per_batch, tm=256):
    t, d = h.shape
    f_dtype = F32 if with_router else BF16
    if is_ctx:
        mod_map = lambda i: (mod.shape[0] - 1, 0, 0)
    else:
        mod_map = lambda i: (i // (per_batch // tm), 0, 0)
    row = lambda i: (i, 0)
    full2 = lambda i: (0, 0)
    return pl.pallas_call(
        functools.partial(_merge_kernel, with_router=with_router),
        out_shape=(jax.ShapeDtypeStruct((t, d), F32), jax.ShapeDtypeStruct((t, d), f_dtype),
                   jax.ShapeDtypeStruct((t, TOP_K), jnp.int32), jax.ShapeDtypeStruct((t, TOP_K), F32)),
        grid=(t // tm,),
        in_specs=[pl.BlockSpec((tm, d), row),
                  pl.BlockSpec((tm, BRANCH_W), row), pl.BlockSpec((tm, BRANCH_W), row),
                  pl.BlockSpec((tm, BRANCH_W), row),
                  pl.BlockSpec((tm, N_BRANCH * d), row),
                  pl.BlockSpec((1, 6, d), mod_map),
                  pl.BlockSpec(gains.shape, full2),
                  pl.BlockSpec(w_branch.shape, lambda i: (0, 0, 0), pipeline_mode=pl.Buffered(1)),
                  pl.BlockSpec(w_out.shape, full2, pipeline_mode=pl.Buffered(1)),
                  pl.BlockSpec(w_router.shape, full2)],
        out_specs=(pl.BlockSpec((tm, d), row), pl.BlockSpec((tm, d), row),
                   pl.BlockSpec((tm, TOP_K), row), pl.BlockSpec((tm, TOP_K), row)),
        compiler_params=_cparams(("parallel",)),
        name="merge_ctx" if is_ctx else "merge_lat",
    )(h, y_ret, y_win, y_na, gates, mod, gains, w_branch, w_out, w_router)


def _swiglu(x, w1, w3, w2, act_ref):
    dff = act_ref.shape[1]
    for c0 in range(0, dff, FF_CHUNK):
        cw = min(FF_CHUNK, dff - c0)
        a1 = _dot(x, w1[:, c0:c0 + cw])
        a3 = _dot(x, w3[:, c0:c0 + cw])
        act_ref[:, c0:c0 + cw] = (a1 * _sigmoid(a1) * a3).astype(BF16)
    return _dot(act_ref[...], w2[...])


def _ffn_kernel(h_ref, f_ref, mod_ref, g_ref, w1_ref, w3_ref, w2_ref, o_ref, act_ref):
    y = _swiglu(f_ref[...], w1_ref, w3_ref, w2_ref, act_ref)
    o_ref[...] = h_ref[...] + mod_ref[0, 5:6, :] * _rms(y, g_ref[3:4, :])


def _ffn(h, f, mod, gains, w1, w3, w2, *, is_ctx, per_batch, tm=512):
    t, d = h.shape
    dff = w1.shape[1]
    if is_ctx:
        mod_map = lambda i: (mod.shape[0] - 1, 0, 0)
    else:
        mod_map = lambda i: (i // (per_batch // tm), 0, 0)
    row = lambda i: (i, 0)
    const = lambda shape: pl.BlockSpec(shape, lambda i: (0, 0), pipeline_mode=pl.Buffered(1))
    return pl.pallas_call(
        _ffn_kernel,
        out_shape=jax.ShapeDtypeStruct((t, d), F32),
        grid=(t // tm,),
        in_specs=[pl.BlockSpec((tm, d), row), pl.BlockSpec((tm, d), row),
                  pl.BlockSpec((1, 6, d), mod_map),
                  pl.BlockSpec(gains.shape, lambda i: (0, 0)),
                  const((d, dff)), const((d, dff)), const((dff, d))],
        out_specs=pl.BlockSpec((tm, d), row),
        scratch_shapes=[pltpu.VMEM((tm, dff), BF16)],
        compiler_params=_cparams(("parallel",)),
        name="ffn_ctx" if is_ctx else "ffn_lat",
    )(h, f, mod, gains, w1, w3, w2)


def _route_slots(idx, tile):
    n = idx.shape[0] * TOP_K
    e = idx.reshape(n)
    onehot = (e[:, None] == jnp.arange(N_EXPERTS, dtype=jnp.int32)).astype(jnp.int32)
    csum = jnp.cumsum(onehot, axis=0)
    counts = csum[-1]
    rank = jnp.sum(csum * onehot, axis=1) - 1
    padded = (counts + tile - 1) // tile * tile
    ends = jnp.cumsum(padded)
    starts = ends - padded
    pos = jnp.sum(onehot * starts[None, :], axis=1) + rank
    n_tiles = n // tile + N_EXPERTS
    tile_start = jnp.arange(n_tiles, dtype=jnp.int32) * tile
    tile_expert = jnp.minimum(jnp.sum((tile_start[:, None] >= ends[None, :]).astype(jnp.int32), axis=1),
                              N_EXPERTS - 1)
    n_active = (ends[-1] // tile).reshape(1)
    return pos.astype(jnp.int32), tile_expert.astype(jnp.int32), n_active.astype(jnp.int32)


def _dispatch_kernel(pos_ref, f_ref, xs_in_ref, xs_ref, sem):
    del xs_in_ref
    tm = f_ref.shape[0]

    def row_copy(r, k):
        p = pos_ref[0, 0, TOP_K * r + k]
        return pltpu.make_async_copy(f_ref.at[pl.ds(r, 1), :], xs_ref.at[pl.ds(p, 1), :], sem)

    def start(r, carry):
        for k in range(TOP_K):
            row_copy(r, k).start()
        return carry

    def wait(r, carry):
        for k in range(TOP_K):
            row_copy(r, k).wait()
        return carry

    lax.fori_loop(0, tm, start, 0, unroll=8)
    lax.fori_loop(0, tm, wait, 0, unroll=8)


def _dispatch(f, pos, xs, tm=512):
    t, d = f.shape
    pos3 = pos.reshape(t // tm, 1, TOP_K * tm)
    return pl.pallas_call(
        _dispatch_kernel,
        out_shape=jax.ShapeDtypeStruct(xs.shape, xs.dtype),
        grid=(t // tm,),
        in_specs=[pl.BlockSpec((1, 1, TOP_K * tm), lambda i: (i, 0, 0), memory_space=pltpu.SMEM),
                  pl.BlockSpec((tm, d), lambda i: (i, 0)),
                  pl.BlockSpec(memory_space=pl.ANY)],
        out_specs=pl.BlockSpec(memory_space=pl.ANY),
        scratch_shapes=[pltpu.SemaphoreType.DMA(())],
        input_output_aliases={2: 0},
        compiler_params=_cparams(("arbitrary",)),
        name="moe_dispatch",
    )(pos3, f, xs)


def _experts_kernel(te_ref, na_ref, x_ref, *refs, has_prev):
    del te_ref
    if has_prev:
        yprev_ref, w1_ref, w3_ref, w2_ref, y_ref, act_ref = refs
    else:
        w1_ref, w3_ref, w2_ref, y_ref, act_ref = refs
    active = pl.program_id(0) < na_ref[0]

    @pl.when(active)
    def _():
        y = _swiglu(x_ref[...].astype(BF16), w1_ref.at[0], w3_ref.at[0], w2_ref.at[0], act_ref)
        y_ref[...] = yprev_ref[...] + y if has_prev else y

    @pl.when(jnp.logical_not(active))
    def _():
        y_ref[...] = jnp.zeros_like(y_ref)


def _experts_pass(xs, y_prev, tile_expert, n_active, w1, w3, w2, *, tile, fc, c):
    p, d = xs.shape
    slot = pl.BlockSpec((tile, d), lambda j, te, na: (j, 0))
    in_specs = [slot] + ([slot] if y_prev is not None else []) + [
        pl.BlockSpec((1, d, fc), lambda j, te, na: (te[j], 0, c)),
        pl.BlockSpec((1, d, fc), lambda j, te, na: (te[j], 0, c)),
        pl.BlockSpec((1, fc, d), lambda j, te, na: (te[j], c, 0))]
    grid_spec = pltpu.PrefetchScalarGridSpec(
        num_scalar_prefetch=2, grid=(p // tile,), in_specs=in_specs, out_specs=slot,
        scratch_shapes=[pltpu.VMEM((tile, fc), BF16)])
    args = (xs,) + ((y_prev,) if y_prev is not None else ()) + (w1, w3, w2)
    return pl.pallas_call(
        functools.partial(_experts_kernel, has_prev=y_prev is not None),
        out_shape=jax.ShapeDtypeStruct((p, d), F32),
        grid_spec=grid_spec,
        compiler_params=_cparams(("arbitrary",)),
        name="moe_experts",
    )(tile_expert, n_active, *args)


def _experts(xs, tile_expert, n_active, w1, w3, w2, *, tile, fc):
    y = None
    for c in range(w1.shape[2] // fc):
        y = _experts_pass(xs, y, tile_expert, n_active, w1, w3, w2, tile=tile, fc=fc, c=c)
    return y


def _combine_kernel(pos_ref, gw_ref, h_ref, mod_ref, g_ref, y_any, o_ref, buf_ref, sem):
    tm = h_ref.shape[0]

    def row_copy(r, k):
        p = pos_ref[0, 0, TOP_K * r + k]
        return pltpu.make_async_copy(y_any.at[pl.ds(p, 1), :], buf_ref.at[k, pl.ds(r, 1), :], sem)

    def start(r, carry):
        for k in range(TOP_K):
            row_copy(r, k).start()
        return carry

    def wait(r, carry):
        for k in range(TOP_K):
            row_copy(r, k).wait()
        return carry

    lax.fori_loop(0, tm, start, 0, unroll=8)
    lax.fori_loop(0, tm, wait, 0, unroll=8)
    gw = gw_ref[...]
    y = gw[:, 0:1] * buf_ref[0] + gw[:, 1:2] * buf_ref[1]
    o_ref[...] = h_ref[...] + mod_ref[0, 5:6, :] * _rms(y, g_ref[3:4, :])


def _combine(h, pos, gw, y_sorted, mod, gains, *, is_ctx, per_batch, tm=512):
    t, d = h.shape
    pos3 = pos.reshape(t // tm, 1, TOP_K * tm)
    if is_ctx:
        mod_map = lambda i: (mod.shape[0] - 1, 0, 0)
    else:
        mod_map = lambda i: (i // (per_batch // tm), 0, 0)
    row = lambda i: (i, 0)
    return pl.pallas_call(
        _combine_kernel,
        out_shape=jax.ShapeDtypeStruct((t, d), F32),
        grid=(t // tm,),
        in_specs=[pl.BlockSpec((1, 1, TOP_K * tm), lambda i: (i, 0, 0), memory_space=pltpu.SMEM),
                  pl.BlockSpec((tm, TOP_K), row),
                  pl.BlockSpec((tm, d), row),
                  pl.BlockSpec((1, 6, d), mod_map),
                  pl.BlockSpec(gains.shape, lambda i: (0, 0)),
                  pl.BlockSpec(memory_space=pl.ANY)],
        out_specs=pl.BlockSpec((tm, d), row),
        scratch_shapes=[pltpu.VMEM((TOP_K, tm, d), F32), pltpu.SemaphoreType.DMA(())],
        compiler_params=_cparams(("arbitrary",)),
        name="moe_combine_ctx" if is_ctx else "moe_combine_lat",
    )(pos3, gw, h, mod, gains, y_sorted)


def _rope_tables(s, lc):
    def lin(pos):
        n_freq = RET_DK // 2
        inv = ROPE_BASE ** (-jnp.arange(n_freq, dtype=F32) / n_freq)
        ang = pos.astype(F32)[:, None] * inv
        cos, sin = jnp.cos(ang), jnp.sin(ang)
        return jnp.concatenate([cos, cos], -1), jnp.concatenate([-sin, sin], -1)

    t = jnp.arange(s)
    n_freq = WIN_HD // 4
    inv = ROPE_BASE ** (-jnp.arange(n_freq, dtype=F32) / n_freq)
    rowa = (t // GRID_W).astype(F32)[:, None] * inv
    cola = (t % GRID_W).astype(F32)[:, None] * inv
    ang = jnp.concatenate([rowa, cola], axis=-1)
    cos, sin = jnp.cos(ang), jnp.sin(ang)
    ca = jnp.tile(jnp.concatenate([cos, cos], -1), (1, 2))
    sa = jnp.tile(jnp.concatenate([-sin, sin], -1), (1, 2))
    cr_l, sr_l = lin(lc + jnp.arange(s))
    cr_c, sr_c = lin(jnp.arange(lc))
    return (cr_l, sr_l, ca, sa), (cr_c, sr_c, ca[:lc], sa[:lc])


def kernel(x, c, ctx, c_ctx, w_mod, b_mod, norm_gains, w_in, ret_decay, win_sink, na_rpb, w_branch, w_out,
           ffn_w1, ffn_w3, ffn_w2, moe_router, moe_w1, moe_w3, moe_w2):
    batch, s, d = x.shape
    lc = ctx.shape[1]
    depth = w_mod.shape[0]

    c_all = jnp.zeros((16, d), F32).at[:batch].set(c).at[batch].set(c_ctx)
    mods = _modulation(c_all, w_mod, b_mod)[:, :batch + 1].reshape(depth, batch + 1, 6, d)

    tabs_l, tabs_c = _rope_tables(s, lc)
    log_gamma = jnp.log1p(-jnp.exp(ret_decay.astype(F32)))
    na_blocks, na_ids = _na_bias_pairs(na_rpb, s)

    h_l = x.reshape(batch * s, d)
    h_c = ctx.reshape(batch * lc, d)
    zero_router = jnp.zeros((d, N_EXPERTS), F32)

    for layer in range(depth):
        need_ctx = layer < depth - 1
        mod = mods[layer]
        gains = norm_gains[layer]
        w_in_b = w_in[layer].astype(BF16)
        wb_b = w_branch[layer].astype(BF16)
        wo_b = w_out[layer].astype(BF16)
        i = layer // 2
        is_moe = layer % 2 == 1
        if is_moe:
            w1, w3, w2 = moe_w1[i].astype(BF16), moe_w3[i].astype(BF16), moe_w2[i].astype(BF16)
            w_router = moe_router[i]
        else:
            w1, w3, w2 = ffn_w1[i].astype(BF16), ffn_w3[i].astype(BF16), ffn_w2[i].astype(BF16)
            w_router = zero_router

        p_ret_l, p_win_l, p_na_l, p_gt_l = _inproj(h_l, mod, gains, w_in_b, *tabs_l, is_ctx=False, tm=512)
        p_ret_c, p_win_c, p_na_c, p_gt_c = _inproj(h_c, mod, gains, w_in_b, *tabs_c, is_ctx=True, tm=256)

        y_ret_l, y_ret_c = _retention(log_gamma[layer], p_ret_l, p_ret_c, batch)
        y_win_l = _window(win_sink[layer], p_win_l, p_win_c, batch)
        y_na_l = _neighbourhood(p_na_l, p_na_c, na_blocks, na_ids, layer, batch)

        h_l, f_l, idx_l, gw_l = _merge(h_l, y_ret_l, y_win_l, y_na_l, p_gt_l, mod, gains, wb_b, wo_b,
                                       w_router, with_router=is_moe, is_ctx=False, per_batch=s, tm=512)
        if need_ctx:
            y_win_c, y_na_c = _ctx_attention(win_sink[layer], p_win_c, p_na_c, batch)
            h_c, f_c, idx_c, gw_c = _merge(h_c, y_ret_c, y_win_c, y_na_c, p_gt_c, mod, gains, wb_b, wo_b,
                                           w_router, with_router=is_moe, is_ctx=True, per_batch=lc, tm=512)

        if not is_moe:
            h_l = _ffn(h_l, f_l, mod, gains, w1, w3, w2, is_ctx=False, per_batch=s)
            if need_ctx:
                h_c = _ffn(h_c, f_c, mod, gains, w1, w3, w2, is_ctx=True, per_batch=lc)
            continue

        idx_all = jnp.concatenate([idx_l, idx_c], axis=0) if need_ctx else idx_l
        pos, tile_expert, n_active = _route_slots(idx_all, MOE_TILE)
        n_slots = tile_expert.shape[0] * MOE_TILE
        xs = jnp.zeros((n_slots, d), F32)
        n_l = TOP_K * h_l.shape[0]
        xs = _dispatch(f_l, pos[:n_l], xs)
        if need_ctx:
            xs = _dispatch(f_c, pos[n_l:], xs)
        y_sorted = _experts(xs, tile_expert, n_active, w1, w3, w2, tile=MOE_TILE, fc=MOE_FC)
        h_l = _combine(h_l, pos[:n_l], gw_l, y_sorted, mod, gains, is_ctx=False, per_batch=s)
        if need_ctx:
            h_c = _combine(h_c, pos[n_l:], gw_c, y_sorted, mod, gains, is_ctx=True, per_batch=lc)
    return h_l.reshape(batch, s, d)
```

```python
import functools
import math

import jax
import jax.numpy as jnp
import numpy as np
from jax import lax
from jax.experimental import pallas as pl
from jax.experimental.pallas import tpu as pltpu

F32 = jnp.float32
BF16 = jnp.bfloat16

D_MODEL = 1024
GRID_W = 64
RET_HEADS = 4
RET_DK = 128
WIN_HEADS = 8
WIN_KV_HEADS = 2
WIN_HD = 64
WINDOW = 128
WIN_BLOCK = 128
NA_HEADS = 8
NA_HD = 64
NA_ROWS_MAX = 8
NA_COLS = 16
BRANCH_W = 512
N_BRANCH = 3
N_EXPERTS = 8
TOP_K = 2
ROPE_BASE = 10000.0
NORM_EPS = 1e-6
NEG_INF = -1e30
LOG2E = math.log2(math.e)

LANES = 128
HALF = LANES // 2
VMEM_LIMIT = 56 * 1024 * 1024

TM_TOKENS = 512
TM_PROJ_CTX = 256

C_RQ, C_RK, C_RV, C_RG = 0, 512, 1024, 1536
C_WQ, C_WK, C_WV = 2048, 2560, 2688
C_NQ, C_NK, C_NV = 2816, 3328, 3840
C_GATE = 4352

RET_CHUNK = 256
NA_QROWS = 4
NA_KROWS = 12
ROW_BLK = 16
FF_CHUNK = 256
MOE_TILE = 512
MOE_FC = 1792


def _cparams(sem):
    return pltpu.CompilerParams(dimension_semantics=sem, vmem_limit_bytes=VMEM_LIMIT)


def _rms(x, g):
    return x * lax.rsqrt(jnp.mean(x * x, axis=-1, keepdims=True) + NORM_EPS) * g


def _sigmoid(x):
    return 0.5 * jnp.tanh(0.5 * x) + 0.5


def _dot(a, b):
    return jnp.dot(a, b, preferred_element_type=F32)


def _dot_nt(a, b):
    return lax.dot_general(a, b, (((1,), (1,)), ((), ())), preferred_element_type=F32)


def _dot_tn(a, b):
    return lax.dot_general(a, b, (((0,), (0,)), ((), ())), preferred_element_type=F32)


def _mod_kernel(c_ref, w_ref, b_ref, o_ref):
    c = c_ref[...]
    s = c * jax.nn.sigmoid(c)
    o_ref[0] = jnp.dot(s, w_ref[0], preferred_element_type=F32,
                       precision=lax.Precision.HIGHEST) + b_ref[0]


def _modulation(c_all, w_mod, b_mod):
    depth, d, n = w_mod.shape
    tn = 1536
    return pl.pallas_call(
        _mod_kernel,
        out_shape=jax.ShapeDtypeStruct((depth, c_all.shape[0], n), F32),
        grid=(depth, n // tn),
        in_specs=[pl.BlockSpec(c_all.shape, lambda l, j: (0, 0)),
                  pl.BlockSpec((1, d, tn), lambda l, j: (l, 0, j)),
                  pl.BlockSpec((1, 1, tn), lambda l, j: (l, 0, j))],
        out_specs=pl.BlockSpec((1, c_all.shape[0], tn), lambda l, j: (l, 0, j)),
        compiler_params=_cparams(("arbitrary", "arbitrary")),
        name="modulation",
    )(c_all, w_mod, b_mod.reshape(depth, 1, n))


def _inproj_kernel(h_ref, mod_ref, g_ref, w_ref, cr_ref, sr_ref, ca_ref, sa_ref,
                   oret_ref, owin_ref, ona_ref, og_ref, *, rope_win):
    x = h_ref[...]
    a = _rms(x, g_ref[0:1, :]) * (1.0 + mod_ref[0, 1:2, :]) + mod_ref[0, 0:1, :]
    a = a.astype(BF16)
    tm = x.shape[0]
    lane = lax.broadcasted_iota(jnp.int32, (tm, LANES), 1)
    even = lane < HALF
    first_half = (lane % HALF) < HALF // 2

    def mm(c0, cw):
        return _dot(a, w_ref[:, c0:c0 + cw])

    def rope_lin(v):
        return v * cr_ref[...] + pltpu.roll(v, HALF, 1) * sr_ref[...]

    def rope_ax(v):
        rot = jnp.where(first_half, pltpu.roll(v, LANES - HALF // 2, 1), pltpu.roll(v, HALF // 2, 1))
        return v * ca_ref[...] + rot * sa_ref[...]

    r = mm(C_RQ, 512)
    for h in range(RET_HEADS):
        blk = rope_lin(r[:, h * 128:(h + 1) * 128]) * (RET_DK ** -0.5)
        oret_ref[:, h * 128:(h + 1) * 128] = blk.astype(BF16)
    r = mm(C_RK, 512)
    for h in range(RET_HEADS):
        blk = rope_lin(r[:, h * 128:(h + 1) * 128])
        oret_ref[:, 512 + h * 128:512 + (h + 1) * 128] = blk.astype(BF16)
    oret_ref[:, 1024:1536] = mm(C_RV, 512).astype(BF16)
    oret_ref[:, 1536:2048] = mm(C_RG, 512).astype(BF16)

    r = mm(C_WQ, 512)
    for p in range(WIN_HEADS // 2):
        blk = r[:, p * 128:(p + 1) * 128]
        if rope_win:
            blk = rope_ax(blk)
        blk = blk * (WIN_HD ** -0.5 * LOG2E)
        owin_ref[:, (2 * p) * 128:(2 * p + 1) * 128] = jnp.where(even, blk, 0.0).astype(BF16)
        owin_ref[:, (2 * p + 1) * 128:(2 * p + 2) * 128] = jnp.where(even, 0.0, blk).astype(BF16)
    r = mm(C_WK, 256)
    k2 = r[:, 0:128]
    if rope_win:
        k2 = rope_ax(k2)
    v2 = r[:, 128:256]
    for i, t in enumerate((k2, v2)):
        sw = pltpu.roll(t, HALF, 1)
        owin_ref[:, 1024 + 256 * i:1024 + 256 * i + 128] = jnp.where(even, t, sw).astype(BF16)
        owin_ref[:, 1024 + 256 * i + 128:1024 + 256 * i + 256] = jnp.where(even, sw, t).astype(BF16)

    r = mm(C_NQ, 512) * (NA_HD ** -0.5 * LOG2E)
    for p in range(NA_HEADS // 2):
        blk = r[:, p * 128:(p + 1) * 128]
        ona_ref[:, (2 * p) * 128:(2 * p + 1) * 128] = jnp.where(even, blk, 0.0).astype(BF16)
        ona_ref[:, (2 * p + 1) * 128:(2 * p + 2) * 128] = jnp.where(even, 0.0, blk).astype(BF16)
    ona_ref[:, 1024:1536] = mm(C_NK, 512).astype(BF16)
    ona_ref[:, 1536:2048] = mm(C_NV, 512).astype(BF16)

    for j in range(6):
        og_ref[:, j * 512:(j + 1) * 512] = _sigmoid(mm(C_GATE + j * 512, 512)).astype(BF16)


def _inproj(h, mod, gains, w_in, cr, sr, ca, sa, *, is_ctx):
    t, d = h.shape
    tm = TM_PROJ_CTX if is_ctx else TM_TOKENS
    nt = t // tm
    per_batch = cr.shape[0] // tm
    if is_ctx:
        mod_map = lambda i: (mod.shape[0] - 1, 0, 0)
    else:
        mod_map = lambda i: (i // per_batch, 0, 0)
    pos_map = lambda i: (i % per_batch, 0)
    full = lambda i: (0, 0)
    row = lambda i: (i, 0)
    outs = (jax.ShapeDtypeStruct((t, 2048), BF16), jax.ShapeDtypeStruct((t, 1536), BF16),
            jax.ShapeDtypeStruct((t, 2048), BF16), jax.ShapeDtypeStruct((t, 3072), BF16))
    return pl.pallas_call(
        functools.partial(_inproj_kernel, rope_win=not is_ctx),
        out_shape=outs,
        grid=(nt,),
        in_specs=[pl.BlockSpec((tm, d), row),
                  pl.BlockSpec((1, 6, d), mod_map),
                  pl.BlockSpec(gains.shape, full),
                  pl.BlockSpec(w_in.shape, full, pipeline_mode=pl.Buffered(1)),
                  pl.BlockSpec((tm, LANES), pos_map), pl.BlockSpec((tm, LANES), pos_map),
                  pl.BlockSpec((tm, LANES), pos_map), pl.BlockSpec((tm, LANES), pos_map)],
        out_specs=(pl.BlockSpec((tm, 2048), row), pl.BlockSpec((tm, 1536), row),
                   pl.BlockSpec((tm, 2048), row), pl.BlockSpec((tm, 3072), row)),
        compiler_params=_cparams(("parallel",)),
        name="inproj_ctx" if is_ctx else "inproj_lat",
    )(h, mod, gains, w_in, cr, sr, ca, sa)


def _ret_kernel(lg_ref, ql, kl, vl, gl, qc, kc, vc, gc, yl_ref, yc_ref, acc_ref):
    for h in range(RET_HEADS):
        _ret_head(h, lg_ref, ql, kl, vl, gl, qc, kc, vc, gc, yl_ref, yc_ref, acc_ref)


def _ret_head(h, lg_ref, ql, kl, vl, gl, qc, kc, vc, gc, yl_ref, yc_ref, acc_ref):
    C = RET_CHUNK
    hd = slice(h * RET_DK, (h + 1) * RET_DK)
    lgf = lg_ref[0, h]
    lgb = lg_ref[1, h]
    ii = lax.broadcasted_iota(jnp.int32, (C, C), 0).astype(F32)
    jj = lax.broadcasted_iota(jnp.int32, (C, C), 1).astype(F32)
    diff = ii - jj
    dmat = jnp.where(diff >= 0, jnp.exp(lgf * jnp.maximum(diff, 0.0)),
                     jnp.exp(lgb * jnp.maximum(-diff, 0.0)))
    r = lax.broadcasted_iota(jnp.int32, (C, RET_DK), 0).astype(F32)
    qdf = jnp.exp(lgf * (r + 1.0))
    kdf = jnp.exp(lgf * (C - 1.0 - r))
    qdb = jnp.exp(lgb * (C - r))
    kdb = jnp.exp(lgb * r)
    ones = jnp.ones((RET_DK, RET_DK), F32)
    cdf = jnp.exp(ones * (lgf * C))
    cdb = jnp.exp(ones * (lgb * C))

    n_lat = ql.shape[0] // C
    lc = qc.shape[0]
    chunks = [(qc, kc, vc, gc, yc_ref, 0, 0)]
    chunks += [(ql, kl, vl, gl, yl_ref, j * C, lc + j * C) for j in range(n_lat)]

    state = jnp.zeros((RET_DK, RET_DK), F32)
    for (qr, kr, vr, _, _, off, aoff) in chunks:
        q = qr[off:off + C, hd]
        k = kr[off:off + C, hd]
        v = vr[off:off + C, hd]
        s = _dot_nt(q, k) * dmat
        y = _dot(s.astype(BF16), v)
        y = y + qdf * _dot(q, state.astype(BF16))
        state = state * cdf + _dot_tn((k.astype(F32) * kdf).astype(BF16), v)
        acc_ref[aoff:aoff + C, hd] = y

    state = jnp.zeros((RET_DK, RET_DK), F32)
    for (qr, kr, vr, gr, yr, off, aoff) in [chunks[0]] + chunks[:0:-1]:
        q = qr[off:off + C, hd]
        k = kr[off:off + C, hd]
        v = vr[off:off + C, hd]
        y = acc_ref[aoff:aoff + C, hd] + qdb * _dot(q, state.astype(BF16))
        state = state * cdb + _dot_tn((k.astype(F32) * kdb).astype(BF16), v)
        mu = jnp.mean(y, axis=-1, keepdims=True)
        dlt = y - mu
        var = jnp.mean(dlt * dlt, axis=-1, keepdims=True)
        yn = dlt * lax.rsqrt(var + NORM_EPS)
        g = gr[off:off + C, hd].astype(F32)
        yr[off:off + C, hd] = (g * _sigmoid(g) * yn).astype(BF16)


def _retention(lg, pl_ret, pc_ret, batch):
    tl = pl_ret.shape[0]
    tc = pc_ret.shape[0]
    s = tl // batch
    lc = tc // batch
    assert lc == RET_CHUNK and s % RET_CHUNK == 0
    w = RET_HEADS * RET_DK
    lat = lambda cb: pl.BlockSpec((s, w), lambda b, cb=cb: (b, cb))
    ctx = lambda cb: pl.BlockSpec((lc, w), lambda b, cb=cb: (b, cb))
    return pl.pallas_call(
        _ret_kernel,
        out_shape=(jax.ShapeDtypeStruct((tl, w), BF16), jax.ShapeDtypeStruct((tc, w), BF16)),
        grid=(batch,),
        in_specs=[pl.BlockSpec(memory_space=pltpu.SMEM),
                  lat(0), lat(1), lat(2), lat(3), ctx(0), ctx(1), ctx(2), ctx(3)],
        out_specs=(pl.BlockSpec((s, w), lambda b: (b, 0)),
                   pl.BlockSpec((lc, w), lambda b: (b, 0))),
        scratch_shapes=[pltpu.VMEM((s + lc, w), F32)],
        compiler_params=_cparams(("parallel",)),
        name="retention",
    )(lg, pl_ret, pl_ret, pl_ret, pl_ret, pc_ret, pc_ret, pc_ret, pc_ret)


def _softmax_parts(parts, sink=None):
    m = functools.reduce(jnp.maximum, [jnp.max(p, axis=-1, keepdims=True) for p in parts])
    if sink is not None:
        m = jnp.maximum(m, sink)
    es = [jnp.exp2(p - m) for p in parts]
    den = functools.reduce(lambda a, b: a + b, [jnp.sum(e, axis=-1, keepdims=True) for e in es])
    if sink is not None:
        den = den + jnp.exp2(sink - m)
    return [e.astype(BF16) for e in es], 1.0 / den


def _softmax_rows(s_ref, p_ref, sink_ref, row0, nrows, split, bias_rows, sink=None):
    for r in range(0, nrows, ROW_BLK):
        rows = slice(row0 + r, row0 + r + ROW_BLK)
        sb = s_ref[rows, :]
        parts = [sb[:, :split] + bias_rows(r), sb[:, split:]]
        m = functools.reduce(jnp.maximum, [jnp.max(p, axis=-1, keepdims=True) for p in parts])
        if sink is not None:
            m = jnp.maximum(m, sink)
            sink_ref[rows, :] = jnp.broadcast_to(jnp.exp2(sink - m), (ROW_BLK, LANES))
        p_ref[rows, :] = jnp.concatenate([jnp.exp2((p - m).astype(BF16)) for p in parts], axis=1)


def _weighted_values(p, v, extra=None):
    r = _dot(p, jnp.concatenate([v, jnp.ones(v.shape, BF16)], axis=1))
    den = r[:, LANES:]
    if extra is not None:
        den = den + extra
    return r[:, :LANES] * (1.0 / den)


def _win_kernel(sink_ref, q_ref, k0, k1, v0, v1, ck0, ck1, cv0, cv1, o_ref, s_ref, p_ref, sterm_ref):
    blk = WIN_BLOCK
    nw = 3 * blk
    s_len = k0.shape[0]
    even = lax.broadcasted_iota(jnp.int32, (blk, LANES), 1) < HALF
    g = WIN_HEADS // WIN_KV_HEADS
    row_iota = lax.broadcasted_iota(jnp.int32, (ROW_BLK, nw), 0)
    col_iota = lax.broadcasted_iota(jnp.int32, (ROW_BLK, nw), 1)
    for jj in range(q_ref.shape[0] // blk):
        rows = slice(jj * blk, (jj + 1) * blk)
        j = pl.program_id(1) * (q_ref.shape[0] // blk) + jj
        start = pl.multiple_of(jnp.clip((j - 1) * blk, 0, s_len - nw), blk)
        rel = (start - j * blk) + col_iota - row_iota

        def band(r, rel=rel):
            return jnp.where(jnp.abs(rel - r) <= WINDOW, 0.0, NEG_INF)

        for kv, (kr, vr, ckr, cvr) in enumerate(((k0, v0, ck0, cv0), (k1, v1, ck1, cv1))):
            c = WIN_KV_HEADS * jj + kv
            kcat = jnp.concatenate([kr[pl.ds(start, nw), :], ckr[...]], axis=0)
            vcat = jnp.concatenate([vr[pl.ds(start, nw), :], cvr[...]], axis=0)
            qst = jnp.concatenate(
                [q_ref[rows, (g * kv + i) * 128:(g * kv + i + 1) * 128] for i in range(g)], axis=0)
            s_ref[c] = _dot_nt(qst, kcat)
            for i in range(g):
                _softmax_rows(s_ref.at[c], p_ref.at[c], sterm_ref.at[c], i * blk, blk, nw, band,
                              sink_ref[g * kv + i] * LOG2E)
            pv = _weighted_values(p_ref[c], vcat, sterm_ref[c])
            for p in range(g // 2):
                o_ref[rows, (2 * kv + p) * 128:(2 * kv + p + 1) * 128] = jnp.where(
                    even, pv[(2 * p) * blk:(2 * p + 1) * blk],
                    pv[(2 * p + 1) * blk:(2 * p + 2) * blk]).astype(BF16)


def _window(sink, pl_win, pc_win, batch):
    tl = pl_win.shape[0]
    tc = pc_win.shape[0]
    s = tl // batch
    lc = tc // batch
    tq = 2 * WIN_BLOCK
    nb = s // tq
    n_chain = (tq // WIN_BLOCK) * WIN_KV_HEADS
    m_rows = (WIN_HEADS // WIN_KV_HEADS) * WIN_BLOCK
    n_keys = 3 * WIN_BLOCK + lc
    lat = lambda cb: pl.BlockSpec((s, 128), lambda b, j, cb=cb: (b, cb))
    ctx = lambda cb: pl.BlockSpec((lc, 128), lambda b, j, cb=cb: (b, cb))
    return pl.pallas_call(
        _win_kernel,
        out_shape=jax.ShapeDtypeStruct((tl, 512), BF16),
        grid=(batch, nb),
        in_specs=[pl.BlockSpec(memory_space=pltpu.SMEM),
                  pl.BlockSpec((tq, 1024), lambda b, j: (b * nb + j, 0)),
                  lat(8), lat(9), lat(10), lat(11), ctx(8), ctx(9), ctx(10), ctx(11)],
        out_specs=pl.BlockSpec((tq, 512), lambda b, j: (b * nb + j, 0)),
        scratch_shapes=[pltpu.VMEM((n_chain, m_rows, n_keys), F32),
                        pltpu.VMEM((n_chain, m_rows, n_keys), BF16),
                        pltpu.VMEM((n_chain, m_rows, LANES), F32)],
        compiler_params=_cparams(("parallel", "arbitrary")),
        name="window_attn",
    )(sink, pl_win, pl_win, pl_win, pl_win, pl_win, pc_win, pc_win, pc_win, pc_win)


def _na_kernel(ids_ref, q_ref, ka, kb, kc, va, vb, vc, ck, cv, pb_ref, o_ref, s_ref, p_ref):
    nq = q_ref.shape[0]
    n_nb = ka.shape[0] * 3
    even = lax.broadcasted_iota(jnp.int32, (nq, LANES), 1) < HALF
    i = pl.program_id(0)
    pat = jnp.where(i == 0, 0, jnp.where(i == pl.num_programs(0) - 1, 2, 1))

    for p in range(NA_HEADS // 2):
        lanes = slice(p * 128, (p + 1) * 128)
        qst = jnp.concatenate([q_ref[:, (2 * p) * 128:(2 * p + 1) * 128],
                               q_ref[:, (2 * p + 1) * 128:(2 * p + 2) * 128]], axis=0)
        kcat = jnp.concatenate([ka[:, lanes], kb[:, lanes], kc[:, lanes], ck[:, lanes]], axis=0)
        vcat = jnp.concatenate([va[:, lanes], vb[:, lanes], vc[:, lanes], cv[:, lanes]], axis=0)
        s_ref[p] = _dot_nt(qst, kcat)
        for par in range(2):
            for qr in range(NA_QROWS):
                blocks = [pb_ref.at[0, 2 * p + par, ids_ref[pat, qr, kp]] for kp in range(NA_KROWS // 2)]

                def bias_rows(r, blocks=blocks):
                    return jnp.concatenate([b[r:r + ROW_BLK, :] for b in blocks], axis=1)

                _softmax_rows(s_ref.at[p], p_ref.at[p], None, par * nq + qr * GRID_W, GRID_W,
                              n_nb, bias_rows)
        pv = _weighted_values(p_ref[p], vcat)
        o_ref[:, lanes] = jnp.where(even, pv[:nq], pv[nq:]).astype(BF16)


def _neighbourhood(pl_na, pc_na, bias_blocks, bias_ids, layer, batch):
    tl = pl_na.shape[0]
    tc = pc_na.shape[0]
    s = tl // batch
    lc = tc // batch
    nq = NA_QROWS * GRID_W
    ng = s // nq
    kblk = nq
    nkb = s // kblk
    assert NA_KROWS * GRID_W == 3 * kblk

    def kspec(cb, d):
        return pl.BlockSpec(
            (kblk, 512), lambda i, b, d=d, cb=cb: (b * nkb + jnp.clip(i - 1, 0, nkb - 3) + d, cb))

    return pl.pallas_call(
        _na_kernel,
        out_shape=jax.ShapeDtypeStruct((tl, 512), BF16),
        grid=(ng, batch),
        in_specs=[pl.BlockSpec(memory_space=pltpu.SMEM),
                  pl.BlockSpec((nq, 1024), lambda i, b: (b * ng + i, 0)),
                  kspec(2, 0), kspec(2, 1), kspec(2, 2),
                  kspec(3, 0), kspec(3, 1), kspec(3, 2),
                  pl.BlockSpec((lc, 512), lambda i, b: (b, 2)),
                  pl.BlockSpec((lc, 512), lambda i, b: (b, 3)),
                  pl.BlockSpec((1,) + bias_blocks.shape[1:], lambda i, b: (layer, 0, 0, 0, 0),
                               pipeline_mode=pl.Buffered(1))],
        out_specs=pl.BlockSpec((nq, 512), lambda i, b: (b * ng + i, 0)),
        scratch_shapes=[pltpu.VMEM((NA_HEADS // 2, 2 * nq, 3 * kblk + lc), F32),
                        pltpu.VMEM((NA_HEADS // 2, 2 * nq, 3 * kblk + lc), BF16)],
        compiler_params=_cparams(("parallel", "arbitrary")),
        name="neighbourhood_attn",
    )(bias_ids, pl_na, pl_na, pl_na, pl_na, pl_na, pl_na, pl_na, pc_na, pc_na, bias_blocks)


def _na_pair_plan(s):
    rows = s // GRID_W
    kr = min(NA_ROWS_MAX, rows)
    ng = rows // NA_QROWS
    pairs = {}
    ids = np.zeros((3, NA_QROWS, NA_KROWS // 2), np.int32)
    for pi, i in enumerate((0, 1, ng - 1)):
        k0 = NA_QROWS * int(np.clip(i - 1, 0, ng - 3))
        for qr in range(NA_QROWS):
            r = NA_QROWS * i + qr
            r_start = int(np.clip(r - kr // 2, 0, rows - kr))
            for kp in range(NA_KROWS // 2):
                offs = tuple(
                    krow - r + NA_ROWS_MAX - 1 if r_start <= krow < r_start + kr else -1
                    for krow in (k0 + 2 * kp, k0 + 2 * kp + 1))
                ids[pi, qr, kp] = pairs.setdefault(offs, len(pairs))
    return list(pairs), ids


def _na_bias_pairs(rpb, s):
    pair_list, ids = _na_pair_plan(s)
    cq = np.arange(GRID_W)
    ck = np.arange(GRID_W)
    c_start = np.clip(cq - NA_COLS // 2, 0, GRID_W - NA_COLS)
    col_ok = (ck[None, :] >= c_start[:, None]) & (ck[None, :] < c_start[:, None] + NA_COLS)
    dc = np.clip(ck[None, :] - cq[:, None] + (NA_COLS - 1), 0, 2 * NA_COLS - 2)
    sel_c = (dc[:, :, None] == np.arange(2 * NA_COLS - 1)).astype(np.float32)
    t = jnp.einsum("lhrc,xyc->lhrxy", rpb.astype(F32), sel_c, precision=lax.Precision.HIGHEST)
    t = jnp.where(col_ok[None, None, None], t * LOG2E, NEG_INF)
    outside = jnp.full(t.shape[:2] + t.shape[3:], NEG_INF, F32)
    half = lambda off: outside if off < 0 else t[:, :, off]
    blocks = jnp.stack([jnp.concatenate([half(a), half(b)], axis=-1) for a, b in pair_list], axis=2)
    return blocks, jnp.asarray(ids)


def _ctx_attn_kernel(sink_ref, w_ref, n_ref, ow_ref, on_ref):
    lc = w_ref.shape[0]
    even = lax.broadcasted_iota(jnp.int32, (lc, LANES), 1) < HALF
    g = WIN_HEADS // WIN_KV_HEADS
    for kv in range(WIN_KV_HEADS):
        k = w_ref[:, 1024 + kv * 128:1024 + (kv + 1) * 128]
        v = w_ref[:, 1280 + kv * 128:1280 + (kv + 1) * 128]
        qst = jnp.concatenate([w_ref[:, (g * kv + i) * 128:(g * kv + i + 1) * 128] for i in range(g)], axis=0)
        s = _dot_nt(qst, k)
        sm = [_softmax_parts([s[i * lc:(i + 1) * lc]], sink_ref[g * kv + i] * LOG2E) for i in range(g)]
        pv = _dot(jnp.concatenate([e[0][0] for e in sm], axis=0), v)
        for p in range(g // 2):
            ow_ref[:, (2 * kv + p) * 128:(2 * kv + p + 1) * 128] = jnp.where(
                even, pv[(2 * p) * lc:(2 * p + 1) * lc] * sm[2 * p][1],
                pv[(2 * p + 1) * lc:(2 * p + 2) * lc] * sm[2 * p + 1][1]).astype(BF16)
    for p in range(NA_HEADS // 2):
        k = n_ref[:, 1024 + p * 128:1024 + (p + 1) * 128]
        v = n_ref[:, 1536 + p * 128:1536 + (p + 1) * 128]
        qst = jnp.concatenate([n_ref[:, (2 * p) * 128:(2 * p + 1) * 128],
                               n_ref[:, (2 * p + 1) * 128:(2 * p + 2) * 128]], axis=0)
        s = _dot_nt(qst, k)
        sm = [_softmax_parts([s[i * lc:(i + 1) * lc]]) for i in range(2)]
        pv = _dot(jnp.concatenate([e[0][0] for e in sm], axis=0), v)
        on_ref[:, p * 128:(p + 1) * 128] = jnp.where(even, pv[:lc] * sm[0][1], pv[lc:] * sm[1][1]).astype(BF16)


def _ctx_attention(sink, pc_win, pc_na, batch):
    tc = pc_win.shape[0]
    lc = tc // batch
    row = lambda b: (b, 0)
    return pl.pallas_call(
        _ctx_attn_kernel,
        out_shape=(jax.ShapeDtypeStruct((tc, 512), BF16), jax.ShapeDtypeStruct((tc, 512), BF16)),
        grid=(batch,),
        in_specs=[pl.BlockSpec(memory_space=pltpu.SMEM),
                  pl.BlockSpec((lc, pc_win.shape[1]), row), pl.BlockSpec((lc, pc_na.shape[1]), row)],
        out_specs=(pl.BlockSpec((lc, 512), row), pl.BlockSpec((lc, 512), row)),
        compiler_params=_cparams(("parallel",)),
        name="context_attn",
    )(sink, pc_win, pc_na)


def _merge_kernel(h_ref, yr_ref, yw_ref, yn_ref, gt_ref, mod_ref, g_ref, wb_ref, wo_ref, wr_ref,
                  ho_ref, f_ref, idx_ref, gw_ref, *, with_router):
    n_split = 2
    rows_per = h_ref.shape[0] // n_split
    for part in range(n_split):
        _merge_rows(slice(part * rows_per, (part + 1) * rows_per), h_ref, yr_ref, yw_ref, yn_ref, gt_ref,
                    mod_ref, g_ref, wb_ref, wo_ref, wr_ref, ho_ref, f_ref, idx_ref, gw_ref, with_router)


def _merge_rows(rows, h_ref, yr_ref, yw_ref, yn_ref, gt_ref, mod_ref, g_ref, wb_ref, wo_ref, wr_ref,
                ho_ref, f_ref, idx_ref, gw_ref, with_router):
    z = None
    for r, y_ref in enumerate((yr_ref, yw_ref, yn_ref)):
        gate = gt_ref[rows, r * D_MODEL:(r + 1) * D_MODEL].astype(F32)
        t = gate * _dot(y_ref[rows, :], wb_ref[r])
        z = t if z is None else z + t
    o = _dot(z.astype(BF16), wo_ref[...])
    h = h_ref[rows, :] + mod_ref[0, 2:3, :] * _rms(o, g_ref[1:2, :])
    ho_ref[rows, :] = h
    f = _rms(h, g_ref[2:3, :]) * (1.0 + mod_ref[0, 4:5, :]) + mod_ref[0, 3:4, :]
    f_ref[rows, :] = f.astype(f_ref.dtype)
    if not with_router:
        idx_ref[rows, :] = jnp.zeros((rows.stop - rows.start, TOP_K), jnp.int32)
        gw_ref[rows, :] = jnp.zeros((rows.stop - rows.start, TOP_K), F32)
        return
    f_hi = f.astype(BF16)
    f_lo = (f - f_hi.astype(F32)).astype(BF16)
    wr = wr_ref[...]
    w_hi = wr.astype(BF16)
    w_lo = (wr - w_hi.astype(F32)).astype(BF16)
    logits = _dot(f_hi, w_hi) + (_dot(f_lo, w_hi) + _dot(f_hi, w_lo))
    idx = lax.broadcasted_iota(jnp.int32, logits.shape, 1)
    m1 = jnp.max(logits, axis=-1, keepdims=True)
    i1 = jnp.min(jnp.where(logits == m1, idx, N_EXPERTS), axis=-1, keepdims=True)
    rest = jnp.where(idx == i1, -jnp.inf, logits)
    m2 = jnp.max(rest, axis=-1, keepdims=True)
    i2 = jnp.min(jnp.where(rest == m2, idx, N_EXPERTS), axis=-1, keepdims=True)
    e2 = jnp.exp(m2 - m1)
    inv = 1.0 / (1.0 + e2)
    first = lax.broadcasted_iota(jnp.int32, (logits.shape[0], TOP_K), 1) == 0
    idx_ref[rows, :] = jnp.where(first, i1, i2)
    gw_ref[rows, :] = jnp.where(first, inv, e2 * inv)


def _merge(h, y_ret, y_win, y_na, gates, mod, gains, w_branch, w_out, w_router, *, with_router, is_ctx,
           per_batch, tm=TM_TOKENS):
    t, d = h.shape
    f_dtype = F32 if with_router else BF16
    if is_ctx:
        mod_map = lambda i: (mod.shape[0] - 1, 0, 0)
    else:
        mod_map = lambda i: (i // (per_batch // tm), 0, 0)
    row = lambda i: (i, 0)
    full2 = lambda i: (0, 0)
    return pl.pallas_call(
        functools.partial(_merge_kernel, with_router=with_router),
        out_shape=(jax.ShapeDtypeStruct((t, d), F32), jax.ShapeDtypeStruct((t, d), f_dtype),
                   jax.ShapeDtypeStruct((t, TOP_K), jnp.int32), jax.ShapeDtypeStruct((t, TOP_K), F32)),
        grid=(t // tm,),
        in_specs=[pl.BlockSpec((tm, d), row),
                  pl.BlockSpec((tm, BRANCH_W), row), pl.BlockSpec((tm, BRANCH_W), row),
                  pl.BlockSpec((tm, BRANCH_W), row),
                  pl.BlockSpec((tm, N_BRANCH * d), row),
                  pl.BlockSpec((1, 6, d), mod_map),
                  pl.BlockSpec(gains.shape, full2),
                  pl.BlockSpec(w_branch.shape, lambda i: (0, 0, 0), pipeline_mode=pl.Buffered(1)),
                  pl.BlockSpec(w_out.shape, full2, pipeline_mode=pl.Buffered(1)),
                  pl.BlockSpec(w_router.shape, full2)],
        out_specs=(pl.BlockSpec((tm, d), row), pl.BlockSpec((tm, d), row),
                   pl.BlockSpec((tm, TOP_K), row), pl.BlockSpec((tm, TOP_K), row)),
        compiler_params=_cparams(("parallel",)),
        name="merge_ctx" if is_ctx else "merge_lat",
    )(h, y_ret, y_win, y_na, gates, mod, gains, w_branch, w_out, w_router)


def _swiglu(x, w1, w3, w2, act_ref):
    dff = act_ref.shape[1]
    for c0 in range(0, dff, FF_CHUNK):
        cw = min(FF_CHUNK, dff - c0)
        a1 = _dot(x, w1[:, c0:c0 + cw])
        a3 = _dot(x, w3[:, c0:c0 + cw])
        act_ref[:, c0:c0 + cw] = (a1 * _sigmoid(a1) * a3).astype(BF16)
    return _dot(act_ref[...], w2[...])


def _ffn_kernel(h_ref, f_ref, mod_ref, g_ref, w1_ref, w3_ref, w2_ref, o_ref, act_ref):
    y = _swiglu(f_ref[...], w1_ref, w3_ref, w2_ref, act_ref)
    o_ref[...] = h_ref[...] + mod_ref[0, 5:6, :] * _rms(y, g_ref[3:4, :])


def _ffn(h, f, mod, gains, w1, w3, w2, *, is_ctx, per_batch, tm=TM_TOKENS):
    t, d = h.shape
    dff = w1.shape[1]
    if is_ctx:
        mod_map = lambda i: (mod.shape[0] - 1, 0, 0)
    else:
        mod_map = lambda i: (i // (per_batch // tm), 0, 0)
    row = lambda i: (i, 0)
    const = lambda shape: pl.BlockSpec(shape, lambda i: (0, 0), pipeline_mode=pl.Buffered(1))
    return pl.pallas_call(
        _ffn_kernel,
        out_shape=jax.ShapeDtypeStruct((t, d), F32),
        grid=(t // tm,),
        in_specs=[pl.BlockSpec((tm, d), row), pl.BlockSpec((tm, d), row),
                  pl.BlockSpec((1, 6, d), mod_map),
                  pl.BlockSpec(gains.shape, lambda i: (0, 0)),
                  const((d, dff)), const((d, dff)), const((dff, d))],
        out_specs=pl.BlockSpec((tm, d), row),
        scratch_shapes=[pltpu.VMEM((tm, dff), BF16)],
        compiler_params=_cparams(("parallel",)),
        name="ffn_ctx" if is_ctx else "ffn_lat",
    )(h, f, mod, gains, w1, w3, w2)


def _route_slots(idx, tile):
    n = idx.shape[0] * TOP_K
    e = idx.reshape(n)
    onehot = (e[:, None] == jnp.arange(N_EXPERTS, dtype=jnp.int32)).astype(jnp.int32)
    csum = jnp.cumsum(onehot, axis=0)
    counts = csum[-1]
    rank = jnp.sum(csum * onehot, axis=1) - 1
    padded = (counts + tile - 1) // tile * tile
    ends = jnp.cumsum(padded)
    starts = ends - padded
    pos = jnp.sum(onehot * starts[None, :], axis=1) + rank
    n_tiles = n // tile + N_EXPERTS
    tile_start = jnp.arange(n_tiles, dtype=jnp.int32) * tile
    tile_expert = jnp.minimum(jnp.sum((tile_start[:, None] >= ends[None, :]).astype(jnp.int32), axis=1),
                              N_EXPERTS - 1)
    n_active = (ends[-1] // tile).reshape(1)
    return pos.astype(jnp.int32), tile_expert.astype(jnp.int32), n_active.astype(jnp.int32)


def _dispatch_kernel(pos_ref, f_ref, xs_in_ref, xs_ref, sem):
    del xs_in_ref
    tm = f_ref.shape[0]

    def row_copy(r, k):
        p = pos_ref[0, 0, TOP_K * r + k]
        return pltpu.make_async_copy(f_ref.at[pl.ds(r, 1), :], xs_ref.at[pl.ds(p, 1), :], sem)

    def start(r, carry):
        for k in range(TOP_K):
            row_copy(r, k).start()
        return carry

    def wait(r, carry):
        for k in range(TOP_K):
            row_copy(r, k).wait()
        return carry

    lax.fori_loop(0, tm, start, 0, unroll=8)
    lax.fori_loop(0, tm, wait, 0, unroll=8)


def _dispatch(f, pos, xs, tm=TM_TOKENS):
    t, d = f.shape
    pos3 = pos.reshape(t // tm, 1, TOP_K * tm)
    return pl.pallas_call(
        _dispatch_kernel,
        out_shape=jax.ShapeDtypeStruct(xs.shape, xs.dtype),
        grid=(t // tm,),
        in_specs=[pl.BlockSpec((1, 1, TOP_K * tm), lambda i: (i, 0, 0), memory_space=pltpu.SMEM),
                  pl.BlockSpec((tm, d), lambda i: (i, 0)),
                  pl.BlockSpec(memory_space=pl.ANY)],
        out_specs=pl.BlockSpec(memory_space=pl.ANY),
        scratch_shapes=[pltpu.SemaphoreType.DMA(())],
        input_output_aliases={2: 0},
        compiler_params=_cparams(("arbitrary",)),
        name="moe_dispatch",
    )(pos3, f, xs)


def _experts_kernel(te_ref, na_ref, x_ref, *refs, has_prev):
    del te_ref
    if has_prev:
        yprev_ref, w1_ref, w3_ref, w2_ref, y_ref, act_ref = refs
    else:
        w1_ref, w3_ref, w2_ref, y_ref, act_ref = refs
    active = pl.program_id(0) < na_ref[0]

    @pl.when(active)
    def _():
        y = _swiglu(x_ref[...].astype(BF16), w1_ref.at[0], w3_ref.at[0], w2_ref.at[0], act_ref)
        y_ref[...] = yprev_ref[...] + y if has_prev else y

    @pl.when(jnp.logical_not(active))
    def _():
        y_ref[...] = jnp.zeros_like(y_ref)


def _experts_pass(xs, y_prev, tile_expert, n_active, w1, w3, w2, *, tile, fc, c):
    p, d = xs.shape
    slot = pl.BlockSpec((tile, d), lambda j, te, na: (j, 0))
    in_specs = [slot] + ([slot] if y_prev is not None else []) + [
        pl.BlockSpec((1, d, fc), lambda j, te, na: (te[j], 0, c)),
        pl.BlockSpec((1, d, fc), lambda j, te, na: (te[j], 0, c)),
        pl.BlockSpec((1, fc, d), lambda j, te, na: (te[j], c, 0))]
    grid_spec = pltpu.PrefetchScalarGridSpec(
        num_scalar_prefetch=2, grid=(p // tile,), in_specs=in_specs, out_specs=slot,
        scratch_shapes=[pltpu.VMEM((tile, fc), BF16)])
    args = (xs,) + ((y_prev,) if y_prev is not None else ()) + (w1, w3, w2)
    return pl.pallas_call(
        functools.partial(_experts_kernel, has_prev=y_prev is not None),
        out_shape=jax.ShapeDtypeStruct((p, d), F32),
        grid_spec=grid_spec,
        compiler_params=_cparams(("arbitrary",)),
        name="moe_experts",
    )(tile_expert, n_active, *args)


def _experts(xs, tile_expert, n_active, w1, w3, w2, *, tile, fc):
    y = None
    for c in range(w1.shape[2] // fc):
        y = _experts_pass(xs, y, tile_expert, n_active, w1, w3, w2, tile=tile, fc=fc, c=c)
    return y


def _combine_kernel(pos_ref, gw_ref, h_ref, mod_ref, g_ref, y_any, o_ref, buf_ref, sem):
    tm = h_ref.shape[0]

    def row_copy(r, k):
        p = pos_ref[0, 0, TOP_K * r + k]
        return pltpu.make_async_copy(y_any.at[pl.ds(p, 1), :], buf_ref.at[k, pl.ds(r, 1), :], sem)

    def start(r, carry):
        for k in range(TOP_K):
            row_copy(r, k).start()
        return carry

    def wait(r, carry):
        for k in range(TOP_K):
            row_copy(r, k).wait()
        return carry

    lax.fori_loop(0, tm, start, 0, unroll=8)
    lax.fori_loop(0, tm, wait, 0, unroll=8)
    gw = gw_ref[...]
    y = gw[:, 0:1] * buf_ref[0] + gw[:, 1:2] * buf_ref[1]
    o_ref[...] = h_ref[...] + mod_ref[0, 5:6, :] * _rms(y, g_ref[3:4, :])


def _combine(h, pos, gw, y_sorted, mod, gains, *, is_ctx, per_batch, tm=TM_TOKENS):
    t, d = h.shape
    pos3 = pos.reshape(t // tm, 1, TOP_K * tm)
    if is_ctx:
        mod_map = lambda i: (mod.shape[0] - 1, 0, 0)
    else:
        mod_map = lambda i: (i // (per_batch // tm), 0, 0)
    row = lambda i: (i, 0)
    return pl.pallas_call(
        _combine_kernel,
        out_shape=jax.ShapeDtypeStruct((t, d), F32),
        grid=(t // tm,),
        in_specs=[pl.BlockSpec((1, 1, TOP_K * tm), lambda i: (i, 0, 0), memory_space=pltpu.SMEM),
                  pl.BlockSpec((tm, TOP_K), row),
                  pl.BlockSpec((tm, d), row),
                  pl.BlockSpec((1, 6, d), mod_map),
                  pl.BlockSpec(gains.shape, lambda i: (0, 0)),
                  pl.BlockSpec(memory_space=pl.ANY)],
        out_specs=pl.BlockSpec((tm, d), row),
        scratch_shapes=[pltpu.VMEM((TOP_K, tm, d), F32), pltpu.SemaphoreType.DMA(())],
        compiler_params=_cparams(("arbitrary",)),
        name="moe_combine_ctx" if is_ctx else "moe_combine_lat",
    )(pos3, gw, h, mod, gains, y_sorted)


def _rope_tables(s, lc):
    def lin(pos):
        n_freq = RET_DK // 2
        inv = ROPE_BASE ** (-jnp.arange(n_freq, dtype=F32) / n_freq)
        ang = pos.astype(F32)[:, None] * inv
        cos, sin = jnp.cos(ang), jnp.sin(ang)
        return jnp.concatenate([cos, cos], -1), jnp.concatenate([-sin, sin], -1)

    t = jnp.arange(s)
    n_freq = WIN_HD // 4
    inv = ROPE_BASE ** (-jnp.arange(n_freq, dtype=F32) / n_freq)
    rowa = (t // GRID_W).astype(F32)[:, None] * inv
    cola = (t % GRID_W).astype(F32)[:, None] * inv
    ang = jnp.concatenate([rowa, cola], axis=-1)
    cos, sin = jnp.cos(ang), jnp.sin(ang)
    ca = jnp.tile(jnp.concatenate([cos, cos], -1), (1, 2))
    sa = jnp.tile(jnp.concatenate([-sin, sin], -1), (1, 2))
    cr_l, sr_l = lin(lc + jnp.arange(s))
    cr_c, sr_c = lin(jnp.arange(lc))
    return (cr_l, sr_l, ca, sa), (cr_c, sr_c, ca[:lc], sa[:lc])


def kernel(x, c, ctx, c_ctx, w_mod, b_mod, norm_gains, w_in, ret_decay, win_sink, na_rpb, w_branch, w_out,
           ffn_w1, ffn_w3, ffn_w2, moe_router, moe_w1, moe_w3, moe_w2):
    batch, s, d = x.shape
    lc = ctx.shape[1]
    depth = w_mod.shape[0]

    c_all = jnp.zeros((16, d), F32).at[:batch].set(c).at[batch].set(c_ctx)
    mods = _modulation(c_all, w_mod, b_mod)[:, :batch + 1].reshape(depth, batch + 1, 6, d)

    tabs_l, tabs_c = _rope_tables(s, lc)
    log_gamma = jnp.log1p(-jnp.exp(ret_decay.astype(F32)))
    na_blocks, na_ids = _na_bias_pairs(na_rpb, s)

    h_l = x.reshape(batch * s, d)
    h_c = ctx.reshape(batch * lc, d)
    zero_router = jnp.zeros((d, N_EXPERTS), F32)

    for layer in range(depth):
        need_ctx = layer < depth - 1
        mod = mods[layer]
        gains = norm_gains[layer]
        w_in_b = w_in[layer].astype(BF16)
        wb_b = w_branch[layer].astype(BF16)
        wo_b = w_out[layer].astype(BF16)
        i = layer // 2
        is_moe = layer % 2 == 1
        if is_moe:
            w1, w3, w2 = moe_w1[i].astype(BF16), moe_w3[i].astype(BF16), moe_w2[i].astype(BF16)
            w_router = moe_router[i]
        else:
            w1, w3, w2 = ffn_w1[i].astype(BF16), ffn_w3[i].astype(BF16), ffn_w2[i].astype(BF16)
            w_router = zero_router

        p_ret_l, p_win_l, p_na_l, p_gt_l = _inproj(h_l, mod, gains, w_in_b, *tabs_l, is_ctx=False)
        p_ret_c, p_win_c, p_na_c, p_gt_c = _inproj(h_c, mod, gains, w_in_b, *tabs_c, is_ctx=True)

        y_ret_l, y_ret_c = _retention(log_gamma[layer], p_ret_l, p_ret_c, batch)
        y_win_l = _window(win_sink[layer], p_win_l, p_win_c, batch)
        y_na_l = _neighbourhood(p_na_l, p_na_c, na_blocks, na_ids, layer, batch)

        h_l, f_l, idx_l, gw_l = _merge(h_l, y_ret_l, y_win_l, y_na_l, p_gt_l, mod, gains, wb_b, wo_b,
                                       w_router, with_router=is_moe, is_ctx=False, per_batch=s)
        if need_ctx:
            y_win_c, y_na_c = _ctx_attention(win_sink[layer], p_win_c, p_na_c, batch)
            h_c, f_c, idx_c, gw_c = _merge(h_c, y_ret_c, y_win_c, y_na_c, p_gt_c, mod, gains, wb_b, wo_b,
                                           w_router, with_router=is_moe, is_ctx=True, per_batch=lc)

        if not is_moe:
            h_l = _ffn(h_l, f_l, mod, gains, w1, w3, w2, is_ctx=False, per_batch=s)
            if need_ctx:
                h_c = _ffn(h_c, f_c, mod, gains, w1, w3, w2, is_ctx=True, per_batch=lc)
            continue

        idx_all = jnp.concatenate([idx_l, idx_c], axis=0) if need_ctx else idx_l
        pos, tile_expert, n_active = _route_slots(idx_all, MOE_TILE)
        n_slots = tile_expert.shape[0] * MOE_TILE
        xs = jnp.zeros((n_slots, d), F32)
        n_l = TOP_K * h_l.shape[0]
        xs = _dispatch(f_l, pos[:n_l], xs)
        if need_ctx:
            xs = _dispatch(f_c, pos[n_l:], xs)
        y_sorted = _experts(xs, tile_expert, n_active, w1, w3, w2, tile=MOE_TILE, fc=MOE_FC)
        h_l = _combine(h_l, pos[:n_l], gw_l, y_sorted, mod, gains, is_ctx=False, per_batch=s)
        if need_ctx:
            h_c = _combine(h_c, pos[n_l:], gw_c, y_sorted, mod, gains, is_ctx=True, per_batch=lc)
    return h_l.reshape(batch, s, d)
```

```python
import functools
import math

import jax
import jax.numpy as jnp
import numpy as np
from jax import lax
from jax.experimental import pallas as pl
from jax.experimental.pallas import tpu as pltpu

F32 = jnp.float32
BF16 = jnp.bfloat16

D_MODEL = 1024
GRID_W = 64
RET_HEADS = 4
RET_DK = 128
WIN_HEADS = 8
WIN_KV_HEADS = 2
WIN_HD = 64
WINDOW = 128
WIN_BLOCK = 128
NA_HEADS = 8
NA_HD = 64
NA_ROWS_MAX = 8
NA_COLS = 16
BRANCH_W = 512
N_BRANCH = 3
N_EXPERTS = 8
TOP_K = 2
ROPE_BASE = 10000.0
NORM_EPS = 1e-6
NEG_INF = -1e30
LOG2E = math.log2(math.e)

LANES = 128
HALF = LANES // 2
VMEM_LIMIT = 56 * 1024 * 1024

TM_TOKENS = 512
TM_PROJ_CTX = 256

C_RQ, C_RK, C_RV, C_RG = 0, 512, 1024, 1536
C_WQ, C_WK, C_WV = 2048, 2560, 2688
C_NQ, C_NK, C_NV = 2816, 3328, 3840
C_GATE = 4352

RET_CHUNK = 256
NA_QROWS = 4
NA_KROWS = 12
ROW_BLK = 16
FF_CHUNK = 256
MOE_TILE = 512
MOE_FC = 1792


def _cparams(sem):
    return pltpu.CompilerParams(dimension_semantics=sem, vmem_limit_bytes=VMEM_LIMIT)


def _rms(x, g):
    return x * lax.rsqrt(jnp.mean(x * x, axis=-1, keepdims=True) + NORM_EPS) * g


def _sigmoid(x):
    return 0.5 * jnp.tanh(0.5 * x) + 0.5


def _dot(a, b):
    return jnp.dot(a, b, preferred_element_type=F32)


def _dot_nt(a, b):
    return lax.dot_general(a, b, (((1,), (1,)), ((), ())), preferred_element_type=F32)


def _dot_tn(a, b):
    return lax.dot_general(a, b, (((0,), (0,)), ((), ())), preferred_element_type=F32)


def _mod_kernel(c_ref, w_ref, b_ref, o_ref):
    c = c_ref[...]
    s = c * jax.nn.sigmoid(c)
    o_ref[0] = jnp.dot(s, w_ref[0], preferred_element_type=F32,
                       precision=lax.Precision.HIGHEST) + b_ref[0]


def _modulation(c_all, w_mod, b_mod):
    depth, d, n = w_mod.shape
    tn = 1536
    return pl.pallas_call(
        _mod_kernel,
        out_shape=jax.ShapeDtypeStruct((depth, c_all.shape[0], n), F32),
        grid=(depth, n // tn),
        in_specs=[pl.BlockSpec(c_all.shape, lambda l, j: (0, 0)),
                  pl.BlockSpec((1, d, tn), lambda l, j: (l, 0, j)),
                  pl.BlockSpec((1, 1, tn), lambda l, j: (l, 0, j))],
        out_specs=pl.BlockSpec((1, c_all.shape[0], tn), lambda l, j: (l, 0, j)),
        compiler_params=_cparams(("arbitrary", "arbitrary")),
        name="modulation",
    )(c_all, w_mod, b_mod.reshape(depth, 1, n))


def _inproj_kernel(h_ref, mod_ref, g_ref, w_ref, cr_ref, sr_ref, ca_ref, sa_ref,
                   oret_ref, owin_ref, ona_ref, og_ref, *, rope_win):
    x = h_ref[...]
    a = _rms(x, g_ref[0:1, :]) * (1.0 + mod_ref[0, 1:2, :]) + mod_ref[0, 0:1, :]
    a = a.astype(BF16)
    tm = x.shape[0]
    lane = lax.broadcasted_iota(jnp.int32, (tm, LANES), 1)
    even = lane < HALF
    first_half = (lane % HALF) < HALF // 2

    def mm(c0, cw):
        return _dot(a, w_ref[:, c0:c0 + cw])

    def rope_lin(v):
        return v * cr_ref[...] + pltpu.roll(v, HALF, 1) * sr_ref[...]

    def rope_ax(v):
        rot = jnp.where(first_half, pltpu.roll(v, LANES - HALF // 2, 1), pltpu.roll(v, HALF // 2, 1))
        return v * ca_ref[...] + rot * sa_ref[...]

    r = mm(C_RQ, 512)
    for h in range(RET_HEADS):
        blk = rope_lin(r[:, h * 128:(h + 1) * 128]) * (RET_DK ** -0.5)
        oret_ref[:, h * 128:(h + 1) * 128] = blk.astype(BF16)
    r = mm(C_RK, 512)
    for h in range(RET_HEADS):
        blk = rope_lin(r[:, h * 128:(h + 1) * 128])
        oret_ref[:, 512 + h * 128:512 + (h + 1) * 128] = blk.astype(BF16)
    oret_ref[:, 1024:1536] = mm(C_RV, 512).astype(BF16)
    oret_ref[:, 1536:2048] = mm(C_RG, 512).astype(BF16)

    r = mm(C_WQ, 512)
    for p in range(WIN_HEADS // 2):
        blk = r[:, p * 128:(p + 1) * 128]
        if rope_win:
            blk = rope_ax(blk)
        blk = blk * (WIN_HD ** -0.5 * LOG2E)
        owin_ref[:, (2 * p) * 128:(2 * p + 1) * 128] = jnp.where(even, blk, 0.0).astype(BF16)
        owin_ref[:, (2 * p + 1) * 128:(2 * p + 2) * 128] = jnp.where(even, 0.0, blk).astype(BF16)
    r = mm(C_WK, 256)
    k2 = r[:, 0:128]
    if rope_win:
        k2 = rope_ax(k2)
    v2 = r[:, 128:256]
    for i, t in enumerate((k2, v2)):
        sw = pltpu.roll(t, HALF, 1)
        owin_ref[:, 1024 + 256 * i:1024 + 256 * i + 128] = jnp.where(even, t, sw).astype(BF16)
        owin_ref[:, 1024 + 256 * i + 128:1024 + 256 * i + 256] = jnp.where(even, sw, t).astype(BF16)

    r = mm(C_NQ, 512) * (NA_HD ** -0.5 * LOG2E)
    for p in range(NA_HEADS // 2):
        blk = r[:, p * 128:(p + 1) * 128]
        ona_ref[:, (2 * p) * 128:(2 * p + 1) * 128] = jnp.where(even, blk, 0.0).astype(BF16)
        ona_ref[:, (2 * p + 1) * 128:(2 * p + 2) * 128] = jnp.where(even, 0.0, blk).astype(BF16)
    ona_ref[:, 1024:1536] = mm(C_NK, 512).astype(BF16)
    ona_ref[:, 1536:2048] = mm(C_NV, 512).astype(BF16)

    for j in range(6):
        og_ref[:, j * 512:(j + 1) * 512] = _sigmoid(mm(C_GATE + j * 512, 512)).astype(BF16)


def _inproj(h, mod, gains, w_in, cr, sr, ca, sa, *, is_ctx):
    t, d = h.shape
    tm = TM_PROJ_CTX if is_ctx else TM_TOKENS
    nt = t // tm
    per_batch = cr.shape[0] // tm
    if is_ctx:
        mod_map = lambda i: (mod.shape[0] - 1, 0, 0)
    else:
        mod_map = lambda i: (i // per_batch, 0, 0)
    pos_map = lambda i: (i % per_batch, 0)
    full = lambda i: (0, 0)
    row = lambda i: (i, 0)
    outs = (jax.ShapeDtypeStruct((t, 2048), BF16), jax.ShapeDtypeStruct((t, 1536), BF16),
            jax.ShapeDtypeStruct((t, 2048), BF16), jax.ShapeDtypeStruct((t, 3072), BF16))
    return pl.pallas_call(
        functools.partial(_inproj_kernel, rope_win=not is_ctx),
        out_shape=outs,
        grid=(nt,),
        in_specs=[pl.BlockSpec((tm, d), row),
                  pl.BlockSpec((1, 6, d), mod_map),
                  pl.BlockSpec(gains.shape, full),
                  pl.BlockSpec(w_in.shape, full, pipeline_mode=pl.Buffered(1)),
                  pl.BlockSpec((tm, LANES), pos_map), pl.BlockSpec((tm, LANES), pos_map),
                  pl.BlockSpec((tm, LANES), pos_map), pl.BlockSpec((tm, LANES), pos_map)],
        out_specs=(pl.BlockSpec((tm, 2048), row), pl.BlockSpec((tm, 1536), row),
                   pl.BlockSpec((tm, 2048), row), pl.BlockSpec((tm, 3072), row)),
        compiler_params=_cparams(("parallel",)),
        name="inproj_ctx" if is_ctx else "inproj_lat",
    )(h, mod, gains, w_in, cr, sr, ca, sa)


def _ret_kernel(lg_ref, ql, kl, vl, gl, qc, kc, vc, gc, yl_ref, yc_ref, acc_ref):
    for h in range(RET_HEADS):
        _ret_head(h, lg_ref, ql, kl, vl, gl, qc, kc, vc, gc, yl_ref, yc_ref, acc_ref)


def _ret_head(h, lg_ref, ql, kl, vl, gl, qc, kc, vc, gc, yl_ref, yc_ref, acc_ref):
    C = RET_CHUNK
    hd = slice(h * RET_DK, (h + 1) * RET_DK)
    lgf = lg_ref[0, h]
    lgb = lg_ref[1, h]
    ii = lax.broadcasted_iota(jnp.int32, (C, C), 0).astype(F32)
    jj = lax.broadcasted_iota(jnp.int32, (C, C), 1).astype(F32)
    diff = ii - jj
    dmat = jnp.where(diff >= 0, jnp.exp(lgf * jnp.maximum(diff, 0.0)),
                     jnp.exp(lgb * jnp.maximum(-diff, 0.0)))
    r = lax.broadcasted_iota(jnp.int32, (C, RET_DK), 0).astype(F32)
    qdf = jnp.exp(lgf * (r + 1.0))
    kdf = jnp.exp(lgf * (C - 1.0 - r))
    qdb = jnp.exp(lgb * (C - r))
    kdb = jnp.exp(lgb * r)
    ones = jnp.ones((RET_DK, RET_DK), F32)
    cdf = jnp.exp(ones * (lgf * C))
    cdb = jnp.exp(ones * (lgb * C))

    n_lat = ql.shape[0] // C
    lc = qc.shape[0]
    chunks = [(qc, kc, vc, gc, yc_ref, 0, 0)]
    chunks += [(ql, kl, vl, gl, yl_ref, j * C, lc + j * C) for j in range(n_lat)]

    state = jnp.zeros((RET_DK, RET_DK), F32)
    for (qr, kr, vr, _, _, off, aoff) in chunks:
        q = qr[off:off + C, hd]
        k = kr[off:off + C, hd]
        v = vr[off:off + C, hd]
        s = _dot_nt(q, k) * dmat
        y = _dot(s.astype(BF16), v)
        y = y + qdf * _dot(q, state.astype(BF16))
        state = state * cdf + _dot_tn((k.astype(F32) * kdf).astype(BF16), v)
        acc_ref[aoff:aoff + C, hd] = y

    state = jnp.zeros((RET_DK, RET_DK), F32)
    for (qr, kr, vr, gr, yr, off, aoff) in [chunks[0]] + chunks[:0:-1]:
        q = qr[off:off + C, hd]
        k = kr[off:off + C, hd]
        v = vr[off:off + C, hd]
        y = acc_ref[aoff:aoff + C, hd] + qdb * _dot(q, state.astype(BF16))
        state = state * cdb + _dot_tn((k.astype(F32) * kdb).astype(BF16), v)
        mu = jnp.mean(y, axis=-1, keepdims=True)
        dlt = y - mu
        var = jnp.mean(dlt * dlt, axis=-1, keepdims=True)
        yn = dlt * lax.rsqrt(var + NORM_EPS)
        g = gr[off:off + C, hd].astype(F32)
        yr[off:off + C, hd] = (g * _sigmoid(g) * yn).astype(BF16)


def _retention(lg, pl_ret, pc_ret, batch):
    tl = pl_ret.shape[0]
    tc = pc_ret.shape[0]
    s = tl // batch
    lc = tc // batch
    assert lc == RET_CHUNK and s % RET_CHUNK == 0
    w = RET_HEADS * RET_DK
    lat = lambda cb: pl.BlockSpec((s, w), lambda b, cb=cb: (b, cb))
    ctx = lambda cb: pl.BlockSpec((lc, w), lambda b, cb=cb: (b, cb))
    return pl.pallas_call(
        _ret_kernel,
        out_shape=(jax.ShapeDtypeStruct((tl, w), BF16), jax.ShapeDtypeStruct((tc, w), BF16)),
        grid=(batch,),
        in_specs=[pl.BlockSpec(memory_space=pltpu.SMEM),
                  lat(0), lat(1), lat(2), lat(3), ctx(0), ctx(1), ctx(2), ctx(3)],
        out_specs=(pl.BlockSpec((s, w), lambda b: (b, 0)),
                   pl.BlockSpec((lc, w), lambda b: (b, 0))),
        scratch_shapes=[pltpu.VMEM((s + lc, w), F32)],
        compiler_params=_cparams(("parallel",)),
        name="retention",
    )(lg, pl_ret, pl_ret, pl_ret, pl_ret, pc_ret, pc_ret, pc_ret, pc_ret)


def _softmax_parts(parts, sink=None):
    m = functools.reduce(jnp.maximum, [jnp.max(p, axis=-1, keepdims=True) for p in parts])
    if sink is not None:
        m = jnp.maximum(m, sink)
    es = [jnp.exp2(p - m) for p in parts]
    den = functools.reduce(lambda a, b: a + b, [jnp.sum(e, axis=-1, keepdims=True) for e in es])
    if sink is not None:
        den = den + jnp.exp2(sink - m)
    return [e.astype(BF16) for e in es], 1.0 / den


def _softmax_rows(s_ref, p_ref, sink_ref, row0, nrows, split, bias_rows, sink=None):
    for r in range(0, nrows, ROW_BLK):
        rows = slice(row0 + r, row0 + r + ROW_BLK)
        sb = s_ref[rows, :]
        parts = [sb[:, :split] + bias_rows(r), sb[:, split:]]
        m = functools.reduce(jnp.maximum, [jnp.max(p, axis=-1, keepdims=True) for p in parts])
        if sink is not None:
            m = jnp.maximum(m, sink)
            sink_ref[rows, :] = jnp.broadcast_to(jnp.exp2(sink - m), (ROW_BLK, LANES))
        p_ref[rows, :] = jnp.concatenate([jnp.exp2((p - m).astype(BF16)) for p in parts], axis=1)


def _weighted_values(p, v, extra=None):
    r = _dot(p, jnp.concatenate([v, jnp.ones(v.shape, BF16)], axis=1))
    den = r[:, LANES:]
    if extra is not None:
        den = den + extra
    return r[:, :LANES] * (1.0 / den)


def _win_kernel(sink_ref, q_ref, k0, k1, v0, v1, ck0, ck1, cv0, cv1, o_ref, s_ref, p_ref, sterm_ref):
    blk = WIN_BLOCK
    nw = 3 * blk
    s_len = k0.shape[0]
    even = lax.broadcasted_iota(jnp.int32, (blk, LANES), 1) < HALF
    g = WIN_HEADS // WIN_KV_HEADS
    row_iota = lax.broadcasted_iota(jnp.int32, (ROW_BLK, nw), 0)
    col_iota = lax.broadcasted_iota(jnp.int32, (ROW_BLK, nw), 1)
    for jj in range(q_ref.shape[0] // blk):
        rows = slice(jj * blk, (jj + 1) * blk)
        j = pl.program_id(1) * (q_ref.shape[0] // blk) + jj
        start = pl.multiple_of(jnp.clip((j - 1) * blk, 0, s_len - nw), blk)
        rel = (start - j * blk) + col_iota - row_iota

        def band(r, rel=rel):
            return jnp.where(jnp.abs(rel - r) <= WINDOW, 0.0, NEG_INF)

        for kv, (kr, vr, ckr, cvr) in enumerate(((k0, v0, ck0, cv0), (k1, v1, ck1, cv1))):
            c = WIN_KV_HEADS * jj + kv
            kcat = jnp.concatenate([kr[pl.ds(start, nw), :], ckr[...]], axis=0)
            vcat = jnp.concatenate([vr[pl.ds(start, nw), :], cvr[...]], axis=0)
            qst = jnp.concatenate(
                [q_ref[rows, (g * kv + i) * 128:(g * kv + i + 1) * 128] for i in range(g)], axis=0)
            s_ref[c] = _dot_nt(qst, kcat)
            for i in range(g):
                _softmax_rows(s_ref.at[c], p_ref.at[c], sterm_ref.at[c], i * blk, blk, nw, band,
                              sink_ref[g * kv + i] * LOG2E)
            pv = _weighted_values(p_ref[c], vcat, sterm_ref[c])
            for p in range(g // 2):
                o_ref[rows, (2 * kv + p) * 128:(2 * kv + p + 1) * 128] = jnp.where(
                    even, pv[(2 * p) * blk:(2 * p + 1) * blk],
                    pv[(2 * p + 1) * blk:(2 * p + 2) * blk]).astype(BF16)


def _window(sink, pl_win, pc_win, batch):
    tl = pl_win.shape[0]
    tc = pc_win.shape[0]
    s = tl // batch
    lc = tc // batch
    tq = 2 * WIN_BLOCK
    nb = s // tq
    n_chain = (tq // WIN_BLOCK) * WIN_KV_HEADS
    m_rows = (WIN_HEADS // WIN_KV_HEADS) * WIN_BLOCK
    n_keys = 3 * WIN_BLOCK + lc
    lat = lambda cb: pl.BlockSpec((s, 128), lambda b, j, cb=cb: (b, cb))
    ctx = lambda cb: pl.BlockSpec((lc, 128), lambda b, j, cb=cb: (b, cb))
    return pl.pallas_call(
        _win_kernel,
        out_shape=jax.ShapeDtypeStruct((tl, 512), BF16),
        grid=(batch, nb),
        in_specs=[pl.BlockSpec(memory_space=pltpu.SMEM),
                  pl.BlockSpec((tq, 1024), lambda b, j: (b * nb + j, 0)),
                  lat(8), lat(9), lat(10), lat(11), ctx(8), ctx(9), ctx(10), ctx(11)],
        out_specs=pl.BlockSpec((tq, 512), lambda b, j: (b * nb + j, 0)),
        scratch_shapes=[pltpu.VMEM((n_chain, m_rows, n_keys), F32),
                        pltpu.VMEM((n_chain, m_rows, n_keys), BF16),
                        pltpu.VMEM((n_chain, m_rows, LANES), F32)],
        compiler_params=_cparams(("parallel", "arbitrary")),
        name="window_attn",
    )(sink, pl_win, pl_win, pl_win, pl_win, pl_win, pc_win, pc_win, pc_win, pc_win)


def _na_kernel(ids_ref, q_ref, ka, kb, kc, va, vb, vc, ck, cv, pb_ref, o_ref, s_ref, p_ref):
    nq = q_ref.shape[0]
    n_nb = ka.shape[0] * 3
    even = lax.broadcasted_iota(jnp.int32, (nq, LANES), 1) < HALF
    i = pl.program_id(0)
    pat = jnp.where(i == 0, 0, jnp.where(i == pl.num_programs(0) - 1, 2, 1))

    for p in range(NA_HEADS // 2):
        lanes = slice(p * 128, (p + 1) * 128)
        qst = jnp.concatenate([q_ref[:, (2 * p) * 128:(2 * p + 1) * 128],
                               q_ref[:, (2 * p + 1) * 128:(2 * p + 2) * 128]], axis=0)
        kcat = jnp.concatenate([ka[:, lanes], kb[:, lanes], kc[:, lanes], ck[:, lanes]], axis=0)
        vcat = jnp.concatenate([va[:, lanes], vb[:, lanes], vc[:, lanes], cv[:, lanes]], axis=0)
        s_ref[p] = _dot_nt(qst, kcat)
        for par in range(2):
            for qr in range(NA_QROWS):
                blocks = [pb_ref.at[0, 2 * p + par, ids_ref[pat, qr, kp]] for kp in range(NA_KROWS // 2)]

                def bias_rows(r, blocks=blocks):
                    return jnp.concatenate([b[r:r + ROW_BLK, :] for b in blocks], axis=1)

                _softmax_rows(s_ref.at[p], p_ref.at[p], None, par * nq + qr * GRID_W, GRID_W,
                              n_nb, bias_rows)
        pv = _weighted_values(p_ref[p], vcat)
        o_ref[:, lanes] = jnp.where(even, pv[:nq], pv[nq:]).astype(BF16)


def _neighbourhood(pl_na, pc_na, bias_blocks, bias_ids, layer, batch):
    tl = pl_na.shape[0]
    tc = pc_na.shape[0]
    s = tl // batch
    lc = tc // batch
    nq = NA_QROWS * GRID_W
    ng = s // nq
    kblk = nq
    nkb = s // kblk
    assert NA_KROWS * GRID_W == 3 * kblk

    def kspec(cb, d):
        return pl.BlockSpec(
            (kblk, 512), lambda i, b, d=d, cb=cb: (b * nkb + jnp.clip(i - 1, 0, nkb - 3) + d, cb))

    return pl.pallas_call(
        _na_kernel,
        out_shape=jax.ShapeDtypeStruct((tl, 512), BF16),
        grid=(ng, batch),
        in_specs=[pl.BlockSpec(memory_space=pltpu.SMEM),
                  pl.BlockSpec((nq, 1024), lambda i, b: (b * ng + i, 0)),
                  kspec(2, 0), kspec(2, 1), kspec(2, 2),
                  kspec(3, 0), kspec(3, 1), kspec(3, 2),
                  pl.BlockSpec((lc, 512), lambda i, b: (b, 2)),
                  pl.BlockSpec((lc, 512), lambda i, b: (b, 3)),
                  pl.BlockSpec((1,) + bias_blocks.shape[1:], lambda i, b: (layer, 0, 0, 0, 0),
                               pipeline_mode=pl.Buffered(1))],
        out_specs=pl.BlockSpec((nq, 512), lambda i, b: (b * ng + i, 0)),
        scratch_shapes=[pltpu.VMEM((NA_HEADS // 2, 2 * nq, 3 * kblk + lc), F32),
                        pltpu.VMEM((NA_HEADS // 2, 2 * nq, 3 * kblk + lc), BF16)],
        compiler_params=_cparams(("parallel", "arbitrary")),
        name="neighbourhood_attn",
    )(bias_ids, pl_na, pl_na, pl_na, pl_na, pl_na, pl_na, pl_na, pc_na, pc_na, bias_blocks)


def _na_pair_plan(s):
    rows = s // GRID_W
    kr = min(NA_ROWS_MAX, rows)
    ng = rows // NA_QROWS
    pairs = {}
    ids = np.zeros((3, NA_QROWS, NA_KROWS // 2), np.int32)
    for pi, i in enumerate((0, 1, ng - 1)):
        k0 = NA_QROWS * int(np.clip(i - 1, 0, ng - 3))
        for qr in range(NA_QROWS):
            r = NA_QROWS * i + qr
            r_start = int(np.clip(r - kr // 2, 0, rows - kr))
            for kp in range(NA_KROWS // 2):
                offs = tuple(
                    krow - r + NA_ROWS_MAX - 1 if r_start <= krow < r_start + kr else -1
                    for krow in (k0 + 2 * kp, k0 + 2 * kp + 1))
                ids[pi, qr, kp] = pairs.setdefault(offs, len(pairs))
    return list(pairs), ids


def _na_bias_pairs(rpb, s):
    pair_list, ids = _na_pair_plan(s)
    cq = np.arange(GRID_W)
    ck = np.arange(GRID_W)
    c_start = np.clip(cq - NA_COLS // 2, 0, GRID_W - NA_COLS)
    col_ok = (ck[None, :] >= c_start[:, None]) & (ck[None, :] < c_start[:, None] + NA_COLS)
    dc = np.clip(ck[None, :] - cq[:, None] + (NA_COLS - 1), 0, 2 * NA_COLS - 2)
    sel_c = (dc[:, :, None] == np.arange(2 * NA_COLS - 1)).astype(np.float32)
    t = jnp.einsum("lhrc,xyc->lhrxy", rpb.astype(F32), sel_c, precision=lax.Precision.HIGHEST)
    t = jnp.where(col_ok[None, None, None], t * LOG2E, NEG_INF)
    outside = jnp.full(t.shape[:2] + t.shape[3:], NEG_INF, F32)
    half = lambda off: outside if off < 0 else t[:, :, off]
    blocks = jnp.stack([jnp.concatenate([half(a), half(b)], axis=-1) for a, b in pair_list], axis=2)
    return blocks, jnp.asarray(ids)


def _ctx_attn_kernel(sink_ref, w_ref, n_ref, ow_ref, on_ref):
    lc = w_ref.shape[0]
    even = lax.broadcasted_iota(jnp.int32, (lc, LANES), 1) < HALF
    g = WIN_HEADS // WIN_KV_HEADS
    for kv in range(WIN_KV_HEADS):
        k = w_ref[:, 1024 + kv * 128:1024 + (kv + 1) * 128]
        v = w_ref[:, 1280 + kv * 128:1280 + (kv + 1) * 128]
        qst = jnp.concatenate([w_ref[:, (g * kv + i) * 128:(g * kv + i + 1) * 128] for i in range(g)], axis=0)
        s = _dot_nt(qst, k)
        sm = [_softmax_parts([s[i * lc:(i + 1) * lc]], sink_ref[g * kv + i] * LOG2E) for i in range(g)]
        pv = _dot(jnp.concatenate([e[0][0] for e in sm], axis=0), v)
        for p in range(g // 2):
            ow_ref[:, (2 * kv + p) * 128:(2 * kv + p + 1) * 128] = jnp.where(
                even, pv[(2 * p) * lc:(2 * p + 1) * lc] * sm[2 * p][1],
                pv[(2 * p + 1) * lc:(2 * p + 2) * lc] * sm[2 * p + 1][1]).astype(BF16)
    for p in range(NA_HEADS // 2):
        k = n_ref[:, 1024 + p * 128:1024 + (p + 1) * 128]
        v = n_ref[:, 1536 + p * 128:1536 + (p + 1) * 128]
        qst = jnp.concatenate([n_ref[:, (2 * p) * 128:(2 * p + 1) * 128],
                               n_ref[:, (2 * p + 1) * 128:(2 * p + 2) * 128]], axis=0)
        s = _dot_nt(qst, k)
        sm = [_softmax_parts([s[i * lc:(i + 1) * lc]]) for i in range(2)]
        pv = _dot(jnp.concatenate([e[0][0] for e in sm], axis=0), v)
        on_ref[:, p * 128:(p + 1) * 128] = jnp.where(even, pv[:lc] * sm[0][1], pv[lc:] * sm[1][1]).astype(BF16)


def _ctx_attention(sink, pc_win, pc_na, batch):
    tc = pc_win.shape[0]
    lc = tc // batch
    row = lambda b: (b, 0)
    return pl.pallas_call(
        _ctx_attn_kernel,
        out_shape=(jax.ShapeDtypeStruct((tc, 512), BF16), jax.ShapeDtypeStruct((tc, 512), BF16)),
        grid=(batch,),
        in_specs=[pl.BlockSpec(memory_space=pltpu.SMEM),
                  pl.BlockSpec((lc, pc_win.shape[1]), row), pl.BlockSpec((lc, pc_na.shape[1]), row)],
        out_specs=(pl.BlockSpec((lc, 512), row), pl.BlockSpec((lc, 512), row)),
        compiler_params=_cparams(("parallel",)),
        name="context_attn",
    )(sink, pc_win, pc_na)


def _merge_kernel(h_ref, yr_ref, yw_ref, yn_ref, gt_ref, mod_ref, g_ref, wb_ref, wo_ref, wr_ref,
                  ho_ref, f_ref, idx_ref, gw_ref, *, with_router):
    n_split = 2
    rows_per = h_ref.shape[0] // n_split
    for part in range(n_split):
        _merge_rows(slice(part * rows_per, (part + 1) * rows_per), h_ref, yr_ref, yw_ref, yn_ref, gt_ref,
                    mod_ref, g_ref, wb_ref, wo_ref, wr_ref, ho_ref, f_ref, idx_ref, gw_ref, with_router)


def _merge_rows(rows, h_ref, yr_ref, yw_ref, yn_ref, gt_ref, mod_ref, g_ref, wb_ref, wo_ref, wr_ref,
                ho_ref, f_ref, idx_ref, gw_ref, with_router):
    z = None
    for r, y_ref in enumerate((yr_ref, yw_ref, yn_ref)):
        gate = gt_ref[rows, r * D_MODEL:(r + 1) * D_MODEL].astype(F32)
        t = gate * _dot(y_ref[rows, :], wb_ref[r])
        z = t if z is None else z + t
    o = _dot(z.astype(BF16), wo_ref[...])
    h = h_ref[rows, :] + mod_ref[0, 2:3, :] * _rms(o, g_ref[1:2, :])
    ho_ref[rows, :] = h
    f = _rms(h, g_ref[2:3, :]) * (1.0 + mod_ref[0, 4:5, :]) + mod_ref[0, 3:4, :]
    f_ref[rows, :] = f.astype(f_ref.dtype)
    if not with_router:
        idx_ref[rows, :] = jnp.zeros((rows.stop - rows.start, TOP_K), jnp.int32)
        gw_ref[rows, :] = jnp.zeros((rows.stop - rows.start, TOP_K), F32)
        return
    f_hi = f.astype(BF16)
    f_lo = (f - f_hi.astype(F32)).astype(BF16)
    wr = wr_ref[...]
    w_hi = wr.astype(BF16)
    w_lo = (wr - w_hi.astype(F32)).astype(BF16)
    logits = _dot(f_hi, w_hi) + (_dot(f_lo, w_hi) + _dot(f_hi, w_lo))
    idx = lax.broadcasted_iota(jnp.int32, logits.shape, 1)
    m1 = jnp.max(logits, axis=-1, keepdims=True)
    i1 = jnp.min(jnp.where(logits == m1, idx, N_EXPERTS), axis=-1, keepdims=True)
    rest = jnp.where(idx == i1, -jnp.inf, logits)
    m2 = jnp.max(rest, axis=-1, keepdims=True)
    i2 = jnp.min(jnp.where(rest == m2, idx, N_EXPERTS), axis=-1, keepdims=True)
    e2 = jnp.exp(m2 - m1)
    inv = 1.0 / (1.0 + e2)
    first = lax.broadcasted_iota(jnp.int32, (logits.shape[0], TOP_K), 1) == 0
    idx_ref[rows, :] = jnp.where(first, i1, i2)
    gw_ref[rows, :] = jnp.where(first, inv, e2 * inv)


def _merge(h, y_ret, y_win, y_na, gates, mod, gains, w_branch, w_out, w_router, *, with_router, is_ctx,
           per_batch, tm=TM_TOKENS):
    t, d = h.shape
    f_dtype = F32 if with_router else BF16
    if is_ctx:
        mod_map = lambda i: (mod.shape[0] - 1, 0, 0)
    else:
        mod_map = lambda i: (i // (per_batch // tm), 0, 0)
    row = lambda i: (i, 0)
    full2 = lambda i: (0, 0)
    return pl.pallas_call(
        functools.partial(_merge_kernel, with_router=with_router),
        out_shape=(jax.ShapeDtypeStruct((t, d), F32), jax.ShapeDtypeStruct((t, d), f_dtype),
                   jax.ShapeDtypeStruct((t, TOP_K), jnp.int32), jax.ShapeDtypeStruct((t, TOP_K), F32)),
        grid=(t // tm,),
        in_specs=[pl.BlockSpec((tm, d), row),
                  pl.BlockSpec((tm, BRANCH_W), row), pl.BlockSpec((tm, BRANCH_W), row),
                  pl.BlockSpec((tm, BRANCH_W), row),
                  pl.BlockSpec((tm, N_BRANCH * d), row),
                  pl.BlockSpec((1, 6, d), mod_map),
                  pl.BlockSpec(gains.shape, full2),
                  pl.BlockSpec(w_branch.shape, lambda i: (0, 0, 0), pipeline_mode=pl.Buffered(1)),
                  pl.BlockSpec(w_out.shape, full2, pipeline_mode=pl.Buffered(1)),
                  pl.BlockSpec(w_router.shape, full2)],
        out_specs=(pl.BlockSpec((tm, d), row), pl.BlockSpec((tm, d), row),
                   pl.BlockSpec((tm, TOP_K), row), pl.BlockSpec((tm, TOP_K), row)),
        compiler_params=_cparams(("parallel",)),
        name="merge_ctx" if is_ctx else "merge_lat",
    )(h, y_ret, y_win, y_na, gates, mod, gains, w_branch, w_out, w_router)


def _swiglu(x, w1, w3, w2, act_ref):
    dff = act_ref.shape[1]
    for c0 in range(0, dff, FF_CHUNK):
        cw = min(FF_CHUNK, dff - c0)
        a1 = _dot(x, w1[:, c0:c0 + cw])
        a3 = _dot(x, w3[:, c0:c0 + cw])
        act_ref[:, c0:c0 + cw] = (a1 * _sigmoid(a1) * a3).astype(BF16)
    return _dot(act_ref[...], w2[...])


def _ffn_kernel(h_ref, f_ref, mod_ref, g_ref, w1_ref, w3_ref, w2_ref, o_ref, act_ref):
    y = _swiglu(f_ref[...], w1_ref, w3_ref, w2_ref, act_ref)
    o_ref[...] = h_ref[...] + mod_ref[0, 5:6, :] * _rms(y, g_ref[3:4, :])


def _ffn(h, f, mod, gains, w1, w3, w2, *, is_ctx, per_batch, tm=TM_TOKENS):
    t, d = h.shape
    dff = w1.shape[1]
    if is_ctx:
        mod_map = lambda i: (mod.shape[0] - 1, 0, 0)
    else:
        mod_map = lambda i: (i // (per_batch // tm), 0, 0)
    row = lambda i: (i, 0)
    const = lambda shape: pl.BlockSpec(shape, lambda i: (0, 0), pipeline_mode=pl.Buffered(1))
    return pl.pallas_call(
        _ffn_kernel,
        out_shape=jax.ShapeDtypeStruct((t, d), F32),
        grid=(t // tm,),
        in_specs=[pl.BlockSpec((tm, d), row), pl.BlockSpec((tm, d), row),
                  pl.BlockSpec((1, 6, d), mod_map),
                  pl.BlockSpec(gains.shape, lambda i: (0, 0)),
                  const((d, dff)), const((d, dff)), const((dff, d))],
        out_specs=pl.BlockSpec((tm, d), row),
        scratch_shapes=[pltpu.VMEM((tm, dff), BF16)],
        compiler_params=_cparams(("parallel",)),
        name="ffn_ctx" if is_ctx else "ffn_lat",
    )(h, f, mod, gains, w1, w3, w2)


def _route_slots(idx, tile):
    n = idx.shape[0] * TOP_K
    e = idx.reshape(n)
    onehot = (e[:, None] == jnp.arange(N_EXPERTS, dtype=jnp.int32)).astype(jnp.int32)
    csum = jnp.cumsum(onehot, axis=0)
    counts = csum[-1]
    rank = jnp.sum(csum * onehot, axis=1) - 1
    padded = (counts + tile - 1) // tile * tile
    ends = jnp.cumsum(padded)
    starts = ends - padded
    pos = jnp.sum(onehot * starts[None, :], axis=1) + rank
    n_tiles = n // tile + N_EXPERTS
    tile_start = jnp.arange(n_tiles, dtype=jnp.int32) * tile
    tile_expert = jnp.minimum(jnp.sum((tile_start[:, None] >= ends[None, :]).astype(jnp.int32), axis=1),
                              N_EXPERTS - 1)
    n_active = (ends[-1] // tile).reshape(1)
    return pos.astype(jnp.int32), tile_expert.astype(jnp.int32), n_active.astype(jnp.int32)


def _dispatch_kernel(pos_ref, f_ref, xs_in_ref, xs_ref, sem):
    del xs_in_ref
    tm = f_ref.shape[0]

    def row_copy(r, k):
        p = pos_ref[0, 0, TOP_K * r + k]
        return pltpu.make_async_copy(f_ref.at[pl.ds(r, 1), :], xs_ref.at[pl.ds(p, 1), :], sem)

    def start(r, carry):
        for k in range(TOP_K):
            row_copy(r, k).start(priority=k % 2)
        return carry

    def wait(r, carry):
        for k in range(TOP_K):
            row_copy(r, k).wait()
        return carry

    lax.fori_loop(0, tm, start, 0, unroll=8)
    lax.fori_loop(0, tm, wait, 0, unroll=8)


def _dispatch(f, pos, xs, tm=TM_TOKENS):
    t, d = f.shape
    pos3 = pos.reshape(t // tm, 1, TOP_K * tm)
    return pl.pallas_call(
        _dispatch_kernel,
        out_shape=jax.ShapeDtypeStruct(xs.shape, xs.dtype),
        grid=(t // tm,),
        in_specs=[pl.BlockSpec((1, 1, TOP_K * tm), lambda i: (i, 0, 0), memory_space=pltpu.SMEM),
                  pl.BlockSpec((tm, d), lambda i: (i, 0)),
                  pl.BlockSpec(memory_space=pl.ANY)],
        out_specs=pl.BlockSpec(memory_space=pl.ANY),
        scratch_shapes=[pltpu.SemaphoreType.DMA(())],
        input_output_aliases={2: 0},
        compiler_params=_cparams(("arbitrary",)),
        name="moe_dispatch",
    )(pos3, f, xs)


def _experts_kernel(te_ref, na_ref, x_ref, *refs, has_prev):
    del te_ref
    if has_prev:
        yprev_ref, w1_ref, w3_ref, w2_ref, y_ref, act_ref = refs
    else:
        w1_ref, w3_ref, w2_ref, y_ref, act_ref = refs
    active = pl.program_id(0) < na_ref[0]

    @pl.when(active)
    def _():
        y = _swiglu(x_ref[...].astype(BF16), w1_ref.at[0], w3_ref.at[0], w2_ref.at[0], act_ref)
        y_ref[...] = yprev_ref[...] + y if has_prev else y

    @pl.when(jnp.logical_not(active))
    def _():
        y_ref[...] = jnp.zeros_like(y_ref)


def _experts_pass(xs, y_prev, tile_expert, n_active, w1, w3, w2, *, tile, fc, c):
    p, d = xs.shape
    slot = pl.BlockSpec((tile, d), lambda j, te, na: (j, 0))
    in_specs = [slot] + ([slot] if y_prev is not None else []) + [
        pl.BlockSpec((1, d, fc), lambda j, te, na: (te[j], 0, c)),
        pl.BlockSpec((1, d, fc), lambda j, te, na: (te[j], 0, c)),
        pl.BlockSpec((1, fc, d), lambda j, te, na: (te[j], c, 0))]
    grid_spec = pltpu.PrefetchScalarGridSpec(
        num_scalar_prefetch=2, grid=(p // tile,), in_specs=in_specs, out_specs=slot,
        scratch_shapes=[pltpu.VMEM((tile, fc), BF16)])
    args = (xs,) + ((y_prev,) if y_prev is not None else ()) + (w1, w3, w2)
    return pl.pallas_call(
        functools.partial(_experts_kernel, has_prev=y_prev is not None),
        out_shape=jax.ShapeDtypeStruct((p, d), F32),
        grid_spec=grid_spec,
        compiler_params=_cparams(("arbitrary",)),
        name="moe_experts",
    )(tile_expert, n_active, *args)


def _experts(xs, tile_expert, n_active, w1, w3, w2, *, tile, fc):
    y = None
    for c in range(w1.shape[2] // fc):
        y = _experts_pass(xs, y, tile_expert, n_active, w1, w3, w2, tile=tile, fc=fc, c=c)
    return y


def _combine_kernel(pos_ref, gw_ref, h_ref, mod_ref, g_ref, y_any, o_ref, buf_ref, sem):
    tm = h_ref.shape[0]

    def row_copy(r, k):
        p = pos_ref[0, 0, TOP_K * r + k]
        return pltpu.make_async_copy(y_any.at[pl.ds(p, 1), :], buf_ref.at[k, pl.ds(r, 1), :], sem)

    def start(r, carry):
        for k in range(TOP_K):
            row_copy(r, k).start(priority=k % 2)
        return carry

    def wait(r, carry):
        for k in range(TOP_K):
            row_copy(r, k).wait()
        return carry

    lax.fori_loop(0, tm, start, 0, unroll=8)
    lax.fori_loop(0, tm, wait, 0, unroll=8)
    gw = gw_ref[...]
    y = gw[:, 0:1] * buf_ref[0] + gw[:, 1:2] * buf_ref[1]
    o_ref[...] = h_ref[...] + mod_ref[0, 5:6, :] * _rms(y, g_ref[3:4, :])


def _combine(h, pos, gw, y_sorted, mod, gains, *, is_ctx, per_batch, tm=TM_TOKENS):
    t, d = h.shape
    pos3 = pos.reshape(t // tm, 1, TOP_K * tm)
    if is_ctx:
        mod_map = lambda i: (mod.shape[0] - 1, 0, 0)
    else:
        mod_map = lambda i: (i // (per_batch // tm), 0, 0)
    row = lambda i: (i, 0)
    return pl.pallas_call(
        _combine_kernel,
        out_shape=jax.ShapeDtypeStruct((t, d), F32),
        grid=(t // tm,),
        in_specs=[pl.BlockSpec((1, 1, TOP_K * tm), lambda i: (i, 0, 0), memory_space=pltpu.SMEM),
                  pl.BlockSpec((tm, TOP_K), row),
                  pl.BlockSpec((tm, d), row),
                  pl.BlockSpec((1, 6, d), mod_map),
                  pl.BlockSpec(gains.shape, lambda i: (0, 0)),
                  pl.BlockSpec(memory_space=pl.ANY)],
        out_specs=pl.BlockSpec((tm, d), row),
        scratch_shapes=[pltpu.VMEM((TOP_K, tm, d), F32), pltpu.SemaphoreType.DMA(())],
        compiler_params=_cparams(("arbitrary",)),
        name="moe_combine_ctx" if is_ctx else "moe_combine_lat",
    )(pos3, gw, h, mod, gains, y_sorted)


def _rope_tables(s, lc):
    def lin(pos):
        n_freq = RET_DK // 2
        inv = ROPE_BASE ** (-jnp.arange(n_freq, dtype=F32) / n_freq)
        ang = pos.astype(F32)[:, None] * inv
        cos, sin = jnp.cos(ang), jnp.sin(ang)
        return jnp.concatenate([cos, cos], -1), jnp.concatenate([-sin, sin], -1)

    t = jnp.arange(s)
    n_freq = WIN_HD // 4
    inv = ROPE_BASE ** (-jnp.arange(n_freq, dtype=F32) / n_freq)
    rowa = (t // GRID_W).astype(F32)[:, None] * inv
    cola = (t % GRID_W).astype(F32)[:, None] * inv
    ang = jnp.concatenate([rowa, cola], axis=-1)
    cos, sin = jnp.cos(ang), jnp.sin(ang)
    ca = jnp.tile(jnp.concatenate([cos, cos], -1), (1, 2))
    sa = jnp.tile(jnp.concatenate([-sin, sin], -1), (1, 2))
    cr_l, sr_l = lin(lc + jnp.arange(s))
    cr_c, sr_c = lin(jnp.arange(lc))
    return (cr_l, sr_l, ca, sa), (cr_c, sr_c, ca[:lc], sa[:lc])


def kernel(x, c, ctx, c_ctx, w_mod, b_mod, norm_gains, w_in, ret_decay, win_sink, na_rpb, w_branch, w_out,
           ffn_w1, ffn_w3, ffn_w2, moe_router, moe_w1, moe_w3, moe_w2):
    batch, s, d = x.shape
    lc = ctx.shape[1]
    depth = w_mod.shape[0]

    c_all = jnp.zeros((16, d), F32).at[:batch].set(c).at[batch].set(c_ctx)
    mods = _modulation(c_all, w_mod, b_mod)[:, :batch + 1].reshape(depth, batch + 1, 6, d)

    tabs_l, tabs_c = _rope_tables(s, lc)
    log_gamma = jnp.log1p(-jnp.exp(ret_decay.astype(F32)))
    na_blocks, na_ids = _na_bias_pairs(na_rpb, s)

    h_l = x.reshape(batch * s, d)
    h_c = ctx.reshape(batch * lc, d)
    zero_router = jnp.zeros((d, N_EXPERTS), F32)

    for layer in range(depth):
        need_ctx = layer < depth - 1
        mod = mods[layer]
        gains = norm_gains[layer]
        w_in_b = w_in[layer].astype(BF16)
        wb_b = w_branch[layer].astype(BF16)
        wo_b = w_out[layer].astype(BF16)
        i = layer // 2
        is_moe = layer % 2 == 1
        if is_moe:
            w1, w3, w2 = moe_w1[i].astype(BF16), moe_w3[i].astype(BF16), moe_w2[i].astype(BF16)
            w_router = moe_router[i]
        else:
            w1, w3, w2 = ffn_w1[i].astype(BF16), ffn_w3[i].astype(BF16), ffn_w2[i].astype(BF16)
            w_router = zero_router

        p_ret_l, p_win_l, p_na_l, p_gt_l = _inproj(h_l, mod, gains, w_in_b, *tabs_l, is_ctx=False)
        p_ret_c, p_win_c, p_na_c, p_gt_c = _inproj(h_c, mod, gains, w_in_b, *tabs_c, is_ctx=True)

        y_ret_l, y_ret_c = _retention(log_gamma[layer], p_ret_l, p_ret_c, batch)
        y_win_l = _window(win_sink[layer], p_win_l, p_win_c, batch)
        y_na_l = _neighbourhood(p_na_l, p_na_c, na_blocks, na_ids, layer, batch)

        h_l, f_l, idx_l, gw_l = _merge(h_l, y_ret_l, y_win_l, y_na_l, p_gt_l, mod, gains, wb_b, wo_b,
                                       w_router, with_router=is_moe, is_ctx=False, per_batch=s)
        if need_ctx:
            y_win_c, y_na_c = _ctx_attention(win_sink[layer], p_win_c, p_na_c, batch)
            h_c, f_c, idx_c, gw_c = _merge(h_c, y_ret_c, y_win_c, y_na_c, p_gt_c, mod, gains, wb_b, wo_b,
                                           w_router, with_router=is_moe, is_ctx=True, per_batch=lc)

        if not is_moe:
            h_l = _ffn(h_l, f_l, mod, gains, w1, w3, w2, is_ctx=False, per_batch=s)
            if need_ctx:
                h_c = _ffn(h_c, f_c, mod, gains, w1, w3, w2, is_ctx=True, per_batch=lc)
            continue

        idx_all = jnp.concatenate([idx_l, idx_c], axis=0) if need_ctx else idx_l
        pos, tile_expert, n_active = _route_slots(idx_all, MOE_TILE)
        n_slots = tile_expert.shape[0] * MOE_TILE
        xs = jnp.zeros((n_slots, d), F32)
        n_l = TOP_K * h_l.shape[0]
        xs = _dispatch(f_l, pos[:n_l], xs)
        if need_ctx:
            xs = _dispatch(f_c, pos[n_l:], xs)
        y_sorted = _experts(xs, tile_expert, n_active, w1, w3, w2, tile=MOE_TILE, fc=MOE_FC)
        h_l = _combine(h_l, pos[:n_l], gw_l, y_sorted, mod, gains, is_ctx=False, per_batch=s)
        if need_ctx:
            h_c = _combine(h_c, pos[n_l:], gw_c, y_sorted, mod, gains, is_ctx=True, per_batch=lc)
    return h_l.reshape(batch, s, d)
```
